```python
import math
import jax, jax.numpy as jnp
from jax import lax
import numpy as np

D_MODEL = 1024
BATCH = 8
SEQ = 4096
DEPTH = 1

DA_HEADS = 4
DA_HEAD_DIM = 64
DA_V_DIM = 2 * DA_HEAD_DIM
DA_WIDTH = DA_HEADS * DA_V_DIM
Q_BLOCK = 128
ML_HEADS = 4
ML_HEAD_DIM = 128
ML_WIDTH = ML_HEADS * ML_HEAD_DIM
ML_CHUNK = 64
CONV_WIDTH = 4
N_GROUPS = 4
EXPERTS_PER_GROUP = 8
N_EXPERTS = N_GROUPS * EXPERTS_PER_GROUP
TOP_K_IN_GROUP = 2
EXPERT_FF = 512
MOE_BLOCK = 128
NORM_EPS = 1e-6

OFF_DA_Q = 0
OFF_DA_K = OFF_DA_Q + DA_HEADS * 2 * DA_HEAD_DIM
OFF_DA_V = OFF_DA_K + DA_HEADS * 2 * DA_HEAD_DIM
OFF_ML_QK = OFF_DA_V + DA_WIDTH
OFF_ML_V = OFF_ML_QK + 2 * ML_WIDTH
OFF_ML_O = OFF_ML_V + ML_WIDTH
OFF_ML_I = OFF_ML_O + ML_WIDTH
OFF_ML_F = OFF_ML_I + ML_HEADS
IN_WIDTH = OFF_ML_F + ML_HEADS

kernel_name = 'hybrid_diffattn_mlstm_hmoe'


def rms_norm(x, g):
    x32 = x.astype(jnp.float32)
    y = x32 * lax.rsqrt(jnp.mean(x32 * x32, axis=-1, keepdims=True) + NORM_EPS)
    return (y * g.astype(jnp.float32)).astype(x.dtype)


def alibi_slopes(n_heads):
    return jnp.asarray([2.0 ** (-8.0 * (i + 1) / n_heads) for i in range(n_heads)], jnp.float32)


def causal_conv(u, w, b):
    width = w.shape[0]
    s = u.shape[1]
    up = jnp.pad(u, ((0, 0), (width - 1, 0), (0, 0)))
    out = b
    for j in range(width):
        out = out + up[:, j:j + s] * w[j]
    return out


def diff_attention(q, k, v, lam, slopes):
    s_len = q.shape[3]
    scale = DA_HEAD_DIM ** -0.5
    outs = []
    for blk in range(s_len // Q_BLOCK):
        lo, hi = blk * Q_BLOCK, (blk + 1) * Q_BLOCK
        qb = q[:, :, :, lo:hi]
        kb = k[:, :, :, :hi]
        vb = v[:, :, :hi]
        sc = jnp.einsum('bhmqd,bhmkd->bhmqk', qb, kb).astype(jnp.float32) * scale
        dist = (jnp.arange(lo, hi)[:, None] - jnp.arange(hi)[None, :]).astype(jnp.float32)
        bias = -slopes[:, None, None, None] * dist
        sc = jnp.where(dist >= 0, sc + bias, -jnp.inf)
        p = jax.nn.softmax(sc, axis=-1)
        a = p[:, :, 0] - lam * p[:, :, 1]
        outs.append(jnp.einsum('bhqk,bhkd->bhqd', a.astype(v.dtype), vb))
    return jnp.concatenate(outs, axis=2)


def mlstm_chunkwise(q, k, v, i_pre, log_f):
    b_sz, n_h, s_len, dh = q.shape
    n_chunks = s_len // ML_CHUNK
    causal = jnp.tril(jnp.ones((ML_CHUNK, ML_CHUNK), dtype=bool))

    def to_chunks(t):
        t = t.reshape(b_sz, n_h, n_chunks, ML_CHUNK, *t.shape[3:])
        return jnp.moveaxis(t, 2, 0)

    def step(carry, xs):
        c_state, n_state, m_state = carry
        qc, kc, vc, ic, fc = xs
        b = jnp.cumsum(fc, axis=-1)
        d = jnp.where(causal, b[..., :, None] - b[..., None, :] + ic[..., None, :], -jnp.inf)
        inter = b + m_state[..., None]
        m_row = jnp.maximum(jnp.max(d, axis=-1), inter)
        sc = jnp.einsum('bhld,bhsd->bhls', qc, kc) * jnp.exp(d - m_row[..., None])
        w_inter = jnp.exp(inter - m_row)
        num = jnp.einsum('bhls,bhsd->bhld', sc, vc) + w_inter[..., None] * jnp.einsum('bhld,bhde->bhle', qc, c_state)
        den = jnp.sum(sc, axis=-1) + w_inter * jnp.einsum('bhld,bhd->bhl', qc, n_state)
        h = num / jnp.maximum(jnp.abs(den), jnp.exp(-m_row))[..., None]
        b_last = b[..., -1]
        w_log = b_last[..., None] - b + ic
        m_new = jnp.maximum(b_last + m_state, jnp.max(w_log, axis=-1))
        decay = jnp.exp(b_last + m_state - m_new)
        w_in = jnp.exp(w_log - m_new[..., None])
        c_new = decay[..., None, None] * c_state + jnp.einsum('bhs,bhsd,bhse->bhde', w_in, kc, vc)
        n_new = decay[..., None] * n_state + jnp.einsum('bhs,bhsd->bhd', w_in, kc)
        return (c_new, n_new, m_new), h

    init = (jnp.zeros((b_sz, n_h, dh, dh), jnp.float32),
            jnp.zeros((b_sz, n_h, dh), jnp.float32),
            jnp.zeros((b_sz, n_h), jnp.float32))
    xs = (to_chunks(q), to_chunks(k), to_chunks(v), to_chunks(i_pre), to_chunks(log_f))
    _, h = lax.scan(step, init, xs)
    return jnp.moveaxis(h, 0, 2).reshape(b_sz, n_h, s_len, dh)


def hierarchical_moe(xn, w_group, b_group, w_router, b_router, w1, w3, w2):
    n_tok, d = xn.shape
    g_prob = jax.nn.softmax(jnp.einsum('td,dg->tg', xn, w_group).astype(jnp.float32) + b_group.astype(jnp.float32), axis=-1)
    g_top, g_idx = lax.top_k(g_prob, 1)
    e_logits = (jnp.einsum('td,de->te', xn, w_router).astype(jnp.float32) + b_router.astype(jnp.float32))
    e_logits = e_logits.reshape(n_tok, N_GROUPS, EXPERTS_PER_GROUP)[jnp.arange(n_tok), g_idx[:, 0]]
    e_top, e_local = lax.top_k(jax.nn.softmax(e_logits, axis=-1), TOP_K_IN_GROUP)
    e_top = e_top / jnp.sum(e_top, axis=-1, keepdims=True)
    gate = g_top * e_top
    expert_id = g_idx * EXPERTS_PER_GROUP + e_local

    n_assign = n_tok * TOP_K_IN_GROUP
    flat_e = expert_id.reshape(n_assign)
    flat_tok = jnp.repeat(jnp.arange(n_tok, dtype=jnp.int32), TOP_K_IN_GROUP)
    flat_w = gate.reshape(n_assign)
    order = jnp.argsort(flat_e)
    sorted_e = flat_e[order]
    counts = jax.ops.segment_sum(jnp.ones(n_assign, jnp.int32), flat_e, num_segments=N_EXPERTS)
    padded = (counts + MOE_BLOCK - 1) // MOE_BLOCK * MOE_BLOCK
    start = jnp.cumsum(counts) - counts
    pend = jnp.cumsum(padded)
    pstart = pend - padded
    dest = pstart[sorted_e] + (jnp.arange(n_assign, dtype=jnp.int32) - start[sorted_e])
    cap = n_assign + N_EXPERTS * MOE_BLOCK
    n_blocks = cap // MOE_BLOCK
    buf_tok = jnp.zeros(cap, jnp.int32).at[dest].set(flat_tok[order])
    buf_w = jnp.zeros(cap, jnp.float32).at[dest].set(flat_w[order])
    blk_expert = jnp.minimum(jnp.searchsorted(pend, jnp.arange(n_blocks, dtype=jnp.int32) * MOE_BLOCK, side='right'),
                             N_EXPERTS - 1).astype(jnp.int32)
    x_blocks = xn[buf_tok].reshape(n_blocks, MOE_BLOCK, d)

    def run_block(args):
        xb, e = args
        hid = jax.nn.silu(xb @ w1[e]) * (xb @ w3[e])
        return hid @ w2[e]

    y_blocks = lax.map(run_block, (x_blocks, blk_expert))
    y = y_blocks.reshape(cap, d) * buf_w[:, None].astype(xn.dtype)
    return jnp.zeros_like(xn).at[buf_tok].add(y)


def setup_inputs(seed: int = 0) -> dict:
    key = jax.random.key(seed)
    ks = jax.random.split(key, 32)
    f32 = jnp.float32
    L = DEPTH
    D = D_MODEL

    def nrm(k, shape, scale):
        return jax.random.normal(k, shape, f32) * scale

    def gain(k, shape):
        return 1.0 + 0.02 * jax.random.normal(k, shape, f32)

    return {
        'x': nrm(ks[0], (BATCH, SEQ, D), 1.0),
        'attn_norm_g': gain(ks[1], (L, D)),
        'w_in': nrm(ks[2], (L, D, IN_WIDTH), D ** -0.5),
        'da_q_norm_g': gain(ks[3], (L, DA_HEAD_DIM)),
        'da_k_norm_g': gain(ks[4], (L, DA_HEAD_DIM)),
        'da_lambda_q1': nrm(ks[5], (L, DA_HEAD_DIM), 0.1),
        'da_lambda_k1': nrm(ks[6], (L, DA_HEAD_DIM), 0.1),
        'da_lambda_q2': nrm(ks[7], (L, DA_HEAD_DIM), 0.1),
        'da_lambda_k2': nrm(ks[8], (L, DA_HEAD_DIM), 0.1),
        'da_out_norm_g': gain(ks[9], (L, DA_V_DIM)),
        'ml_conv_w': nrm(ks[10], (L, CONV_WIDTH, 2 * ML_WIDTH), CONV_WIDTH ** -0.5),
        'ml_conv_b': nrm(ks[11], (L, 2 * ML_WIDTH), 0.02),
        'ml_i_bias': nrm(ks[12], (L, ML_HEADS), 0.1),
        'ml_f_bias': jnp.linspace(3.0, 6.0, ML_HEADS, dtype=f32)[None, :] + nrm(ks[13], (L, ML_HEADS), 0.1),
        'ml_out_norm_g': gain(ks[14], (L, ML_WIDTH)),
        'w_branch_da': nrm(ks[15], (L, DA_WIDTH, D), DA_WIDTH ** -0.5),
        'w_branch_ml': nrm(ks[16], (L, ML_WIDTH, D), ML_WIDTH ** -0.5),
        'w_gate': nrm(ks[17], (L, D, 2 * D), D ** -0.5),
        'b_gate': nrm(ks[18], (L, 2 * D), 0.02),
        'w_out': nrm(ks[19], (L, D, D), D ** -0.5),
        'ffn_norm_g': gain(ks[20], (L, D)),
        'w_group': nrm(ks[21], (L, D, N_GROUPS), D ** -0.5),
        'b_group': nrm(ks[22], (L, N_GROUPS), 0.01),
        'w_router': nrm(ks[23], (L, D, N_EXPERTS), D ** -0.5),
        'b_router': nrm(ks[24], (L, N_EXPERTS), 0.01),
        'w1': nrm(ks[25], (L, N_EXPERTS, D, EXPERT_FF), D ** -0.5),
        'w3': nrm(ks[26], (L, N_EXPERTS, D, EXPERT_FF), D ** -0.5),
        'w2': nrm(ks[27], (L, N_EXPERTS, EXPERT_FF, D), EXPERT_FF ** -0.5),
    }


def reference(x, attn_norm_g, w_in, da_q_norm_g, da_k_norm_g, da_lambda_q1, da_lambda_k1,
              da_lambda_q2, da_lambda_k2, da_out_norm_g, ml_conv_w, ml_conv_b, ml_i_bias,
              ml_f_bias, ml_out_norm_g, w_branch_da, w_branch_ml, w_gate, b_gate, w_out,
              ffn_norm_g, w_group, b_group, w_router, b_router, w1, w3, w2):
    f32 = jnp.float32
    b_sz, s_len, d = x.shape
    slopes = alibi_slopes(DA_HEADS)
    for layer in range(DEPTH):
        h = rms_norm(x, attn_norm_g[layer])
        proj = jnp.einsum('bsd,de->bse', h, w_in[layer])

        q_da = proj[..., OFF_DA_Q:OFF_DA_K].reshape(b_sz, s_len, DA_HEADS, 2, DA_HEAD_DIM).transpose(0, 2, 3, 1, 4)
        k_da = proj[..., OFF_DA_K:OFF_DA_V].reshape(b_sz, s_len, DA_HEADS, 2, DA_HEAD_DIM).transpose(0, 2, 3, 1, 4)
        v_da = proj[..., OFF_DA_V:OFF_ML_QK].reshape(b_sz, s_len, DA_HEADS, DA_V_DIM).transpose(0, 2, 1, 3)
        q_da = rms_norm(q_da, da_q_norm_g[layer])
        k_da = rms_norm(k_da, da_k_norm_g[layer])
        lam_init = 0.8 - 0.6 * math.exp(-0.3 * layer)
        lam = (jnp.exp(jnp.sum((da_lambda_q1[layer] * da_lambda_k1[layer]).astype(f32)))
               - jnp.exp(jnp.sum((da_lambda_q2[layer] * da_lambda_k2[layer]).astype(f32))) + lam_init)
        o_da = diff_attention(q_da, k_da, v_da, lam, slopes)
        o_da = rms_norm(o_da, da_out_norm_g[layer]) * (1.0 - lam_init)
        y_da = o_da.transpose(0, 2, 1, 3).reshape(b_sz, s_len, DA_WIDTH)

        qk = jax.nn.silu(causal_conv(proj[..., OFF_ML_QK:OFF_ML_V], ml_conv_w[layer], ml_conv_b[layer]))

        def to_heads(t):
            return t.reshape(b_sz, s_len, ML_HEADS, ML_HEAD_DIM).transpose(0, 2, 1, 3).astype(f32)

        q_ml = to_heads(qk[..., :ML_WIDTH]) * (ML_HEAD_DIM ** -0.5)
        k_ml = to_heads(qk[..., ML_WIDTH:])
        v_ml = to_heads(proj[..., OFF_ML_V:OFF_ML_O])
        i_pre = (proj[..., OFF_ML_I:OFF_ML_F] + ml_i_bias[layer]).astype(f32).transpose(0, 2, 1)
        log_f = jax.nn.log_sigmoid((proj[..., OFF_ML_F:IN_WIDTH] + ml_f_bias[layer]).astype(f32)).transpose(0, 2, 1)
        h_ml = mlstm_chunkwise(q_ml, k_ml, v_ml, i_pre, log_f)
        h_ml = rms_norm(h_ml, ml_out_norm_g[layer].reshape(ML_HEADS, 1, ML_HEAD_DIM))
        o_gate = jax.nn.sigmoid(proj[..., OFF_ML_O:OFF_ML_I].astype(f32))
        y_ml = (h_ml.transpose(0, 2, 1, 3).reshape(b_sz, s_len, ML_WIDTH) * o_gate).astype(x.dtype)

        gates = jax.nn.sigmoid(jnp.einsum('bsd,de->bse', h, w_gate[layer]) + b_gate[layer])
        mixed = (gates[..., :d] * jnp.einsum('bsc,cd->bsd', y_da, w_branch_da[layer])
                 + gates[..., d:] * jnp.einsum('bsc,cd->bsd', y_ml, w_branch_ml[layer]))
        x = x + jnp.einsum('bsd,de->bse', mixed, w_out[layer])

        hn = rms_norm(x, ffn_norm_g[layer]).reshape(b_sz * s_len, d)
        moe_out = hierarchical_moe(hn, w_group[layer], b_group[layer], w_router[layer], b_router[layer],
                                   w1[layer], w3[layer], w2[layer])
        x = x + moe_out.reshape(b_sz, s_len, d)
    return x
```

```python
import functools
import math

import jax
import jax.numpy as jnp
from jax import lax
from jax.experimental import pallas as pl
from jax.experimental.pallas import tpu as pltpu

F32 = jnp.float32
BF16 = jnp.bfloat16

D_MODEL = 1024
DA_HEADS = 4
DA_HEAD_DIM = 64
DA_V_DIM = 128
DA_WIDTH = 512
ML_HEADS = 4
ML_HEAD_DIM = 128
ML_WIDTH = 512
CONV_WIDTH = 4
N_GROUPS = 4
EXPERTS_PER_GROUP = 8
N_EXPERTS = 32
EXPERT_FF = 512
NORM_EPS = 1e-6
LAM_INIT = 0.8 - 0.6 * math.exp(-0.3 * 0)

LANES = 128
SUBLANES = 8
VMEM_LIMIT = 56 * 1024 * 1024

SEG_Q = 0
SEG_K = SEG_Q + 1024
SEG_V = SEG_K + 1024
SEG_MQK = SEG_V + 512
SEG_MV = SEG_MQK + 1024
SEG_MO = SEG_MV + 512
SEG_IF = SEG_MO + 512
W_ALL = SEG_IF + LANES

AUX_LO = DA_HEAD_DIM
AUX_HI = DA_HEAD_DIM + 1
POS_PERIOD = 512

TM_PROJ = 256
TQ_ATTN = 256
L_MLSTM = 128
TM_MERGE = 256
TM_ROWS = 1024
EXPERT_BLK = 256


def _cparams(sem):
    return pltpu.CompilerParams(dimension_semantics=sem, vmem_limit_bytes=VMEM_LIMIT)


def _lane_iota(shape):
    return lax.broadcasted_iota(jnp.int32, shape, len(shape) - 1)


def _row_iota(shape):
    return lax.broadcasted_iota(jnp.int32, shape, len(shape) - 2)


def _in_proj_kernel(x_ref, g_ref, w_ref, gq_ref, gk_ref,
                    qa_ref, ka_ref, v_ref, mqk_ref, mv_ref, mo_ref, gif_ref, *, tm):
    x = x_ref[0]
    ms = jnp.mean(x * x, axis=-1, keepdims=True)
    h = ((x * lax.rsqrt(ms + NORM_EPS)) * g_ref[...]).astype(BF16)

    def seg(lo, width):
        return jnp.dot(h, w_ref[:, lo:lo + width], preferred_element_type=F32)

    lane = _lane_iota((tm, LANES))
    pos = pl.program_id(1) * tm + _row_iota((tm, LANES))
    pos_lo = (pos & 255).astype(F32)
    pos_hi = (pos & 256).astype(F32)
    is_lo = lane == AUX_LO
    is_hi = lane == AUX_HI
    inv_dh = 1.0 / DA_HEAD_DIM

    for hd in range(DA_HEADS):
        slope = 2.0 ** (-8.0 * (hd + 1) / DA_HEADS)
        for mp in range(2):
            off = (hd * 2 + mp) * LANES
            q = seg(SEG_Q + off, LANES)
            qn = (q * lax.rsqrt(jnp.sum(q * q, axis=-1, keepdims=True) * inv_dh + NORM_EPS)) * gq_ref[...]
            qn = qn * (DA_HEAD_DIM ** -0.5)
            qn = jnp.where(is_lo | is_hi, slope, qn)
            qa_ref[0, hd, :, mp * LANES:(mp + 1) * LANES] = qn.astype(BF16)
            k = seg(SEG_K + off, LANES)
            kn = (k * lax.rsqrt(jnp.sum(k * k, axis=-1, keepdims=True) * inv_dh + NORM_EPS)) * gk_ref[...]
            kn = jnp.where(is_lo, pos_lo, jnp.where(is_hi, pos_hi, kn))
            ka_ref[0, hd, :, mp * LANES:(mp + 1) * LANES] = kn.astype(BF16)

    v_ref[0] = seg(SEG_V, DA_WIDTH).astype(BF16)
    mqk_ref[0] = seg(SEG_MQK, 2 * ML_WIDTH)
    mv_ref[0] = seg(SEG_MV, ML_WIDTH).astype(BF16)
    mo_ref[0] = seg(SEG_MO, ML_WIDTH)
    gif_ref[0] = seg(SEG_IF, LANES)


def _in_proj(x, g_attn, w_all, gq, gk):
    b_sz, s_len, d = x.shape
    tm = min(TM_PROJ, s_len)
    grid = (b_sz, s_len // tm)
    tok = lambda width: pl.BlockSpec((1, tm, width), lambda b, s: (b, s, 0))
    full = lambda shape: pl.BlockSpec(shape, lambda b, s: (0,) * len(shape))
    head_spec = pl.BlockSpec((1, DA_HEADS, tm, 2 * LANES), lambda b, s: (b, 0, s, 0))
    return pl.pallas_call(
        functools.partial(_in_proj_kernel, tm=tm),
        grid=grid,
        in_specs=[tok(d), full((1, d)), full((d, W_ALL)), full((1, LANES)), full((1, LANES))],
        out_specs=[head_spec, head_spec, tok(DA_WIDTH), tok(2 * ML_WIDTH), tok(ML_WIDTH), tok(ML_WIDTH),
                   tok(LANES)],
        out_shape=[
            jax.ShapeDtypeStruct((b_sz, DA_HEADS, s_len, 2 * LANES), BF16),
            jax.ShapeDtypeStruct((b_sz, DA_HEADS, s_len, 2 * LANES), BF16),
            jax.ShapeDtypeStruct((b_sz, s_len, DA_WIDTH), BF16),
            jax.ShapeDtypeStruct((b_sz, s_len, 2 * ML_WIDTH), F32),
            jax.ShapeDtypeStruct((b_sz, s_len, ML_WIDTH), BF16),
            jax.ShapeDtypeStruct((b_sz, s_len, ML_WIDTH), F32),
            jax.ShapeDtypeStruct((b_sz, s_len, LANES), F32),
        ],
        compiler_params=_cparams(("parallel", "parallel")),
        name="in_proj",
    )(x, g_attn, w_all, gq, gk)


def _attn_kernel(lam_ref, qa_ref, ka_ref, v_ref, go_ref, o_ref, *, tq):
    hd = pl.program_id(1)
    qi = pl.program_id(2)
    slope = jnp.where(hd == 0, 2.0 ** -2, jnp.where(hd == 1, 2.0 ** -4, jnp.where(hd == 2, 2.0 ** -6, 2.0 ** -8)))
    q = qa_ref[0, 0]
    qs = (q[:, :LANES], q[:, LANES:])
    row = _row_iota((tq, tq))
    col = _lane_iota((tq, tq))
    causal = col <= row

    def tile(kv, carry, masked):
        start = pl.multiple_of(kv * tq, tq)
        k = ka_ref[0, 0, pl.ds(start, tq), :]
        v = v_ref[0, pl.ds(start, tq), :]
        base = (start // POS_PERIOD) * POS_PERIOD - qi * tq
        c = (jnp.zeros((tq, 1), jnp.int32) + base).astype(F32) * slope
        out = []
        for mp in range(2):
            m_old, l_old, acc = carry[mp]
            s = lax.dot_general(qs[mp], k[:, mp * LANES:(mp + 1) * LANES],
                                (((1,), (1,)), ((), ())), preferred_element_type=F32)
            if masked:
                s = jnp.where(causal, s, -jnp.inf)
            m_new = jnp.maximum(m_old, jnp.max(s, axis=-1, keepdims=True) + c)
            alpha = jnp.exp(m_old - m_new)
            p = jnp.exp(s - (m_new - c))
            l_new = alpha * l_old + jnp.sum(p, axis=-1, keepdims=True)
            acc = alpha * acc + jnp.dot(p.astype(BF16), v, preferred_element_type=F32)
            out.append((m_new, l_new, acc))
        return tuple(out)

    init_one = (jnp.full((tq, 1), -jnp.inf, F32), jnp.zeros((tq, 1), F32), jnp.zeros((tq, DA_V_DIM), F32))
    carry = lax.fori_loop(0, qi, lambda kv, cr: tile(kv, cr, False), (init_one, init_one))
    carry = tile(qi, carry, True)

    lam_v = lam_ref[...]
    lam = (jnp.exp(jnp.sum(lam_v[0:1] * lam_v[1:2], axis=-1, keepdims=True))
           - jnp.exp(jnp.sum(lam_v[2:3] * lam_v[3:4], axis=-1, keepdims=True)) + LAM_INIT)
    (_, l1, a1), (_, l2, a2) = carry
    o = a1 / l1 - lam * (a2 / l2)
    o = (o * lax.rsqrt(jnp.mean(o * o, axis=-1, keepdims=True) + NORM_EPS)) * go_ref[...]
    o_ref[0] = (o * (1.0 - LAM_INIT)).astype(BF16)


def _attn(lam_vecs, qa, ka, v, g_out):
    b_sz, _, s_len, _ = qa.shape
    tq = min(TQ_ATTN, s_len)
    grid = (b_sz, DA_HEADS, s_len // tq)
    return pl.pallas_call(
        functools.partial(_attn_kernel, tq=tq),
        grid=grid,
        in_specs=[
            pl.BlockSpec((4, LANES), lambda b, h, i: (0, 0)),
            pl.BlockSpec((1, 1, tq, 2 * LANES), lambda b, h, i: (b, h, i, 0)),
            pl.BlockSpec((1, 1, s_len, 2 * LANES), lambda b, h, i: (b, h, 0, 0)),
            pl.BlockSpec((1, s_len, DA_V_DIM), lambda b, h, i: (b, 0, h)),
            pl.BlockSpec((1, DA_V_DIM), lambda b, h, i: (0, 0)),
        ],
        out_specs=pl.BlockSpec((1, tq, DA_V_DIM), lambda b, h, i: (b, i, h)),
        out_shape=jax.ShapeDtypeStruct((b_sz, s_len, DA_WIDTH), BF16),
        compiler_params=_cparams(("parallel", "parallel", "arbitrary")),
        name="attn",
    )(lam_vecs, qa, ka, v, g_out)


def _log_sigmoid(x):
    return jnp.minimum(x, 0.0) - jnp.log1p(jnp.exp(-jnp.abs(x)))


def _mlstm_kernel(mqk_ref, mv_ref, mo_ref, gif_ref, cw_ref, cb_ref, gb_ref, go_ref,
                  y_ref, buf_ref, c_ref, n_ref, m_ref, *, L):
    halo = SUBLANES

    @pl.when(pl.program_id(1) == 0)
    def _():
        buf_ref[0:halo, :] = jnp.zeros((halo, 2 * ML_WIDTH), F32)
        c_ref[...] = jnp.zeros_like(c_ref)
        n_ref[...] = jnp.zeros_like(n_ref)
        m_ref[...] = jnp.zeros_like(m_ref)

    buf_ref[halo:halo + L, :] = mqk_ref[0]
    conv = jnp.broadcast_to(cb_ref[...], (L, 2 * ML_WIDTH))
    for j in range(CONV_WIDTH):
        lo = halo - (CONV_WIDTH - 1) + j
        conv = conv + buf_ref[lo:lo + L, :] * cw_ref[j:j + 1, :]
    buf_ref[0:halo, :] = buf_ref[L:L + halo, :]
    qk = conv * jax.nn.sigmoid(conv)

    g = gif_ref[0] + gb_ref[...]
    lf = _log_sigmoid(g)
    g_t = g.T
    lf_t = lf.T

    row = _row_iota((L, L))
    col = _lane_iota((L, L))
    lower = col <= row

    for hd in range(ML_HEADS):
        sl = slice(hd * ML_HEAD_DIM, (hd + 1) * ML_HEAD_DIM)
        q = qk[:, sl] * (ML_HEAD_DIM ** -0.5)
        k = qk[:, ML_WIDTH + hd * ML_HEAD_DIM:ML_WIDTH + (hd + 1) * ML_HEAD_DIM]
        v = mv_ref[0, :, sl]
        i_col = g[:, hd:hd + 1]
        f_col = lf[:, ML_HEADS + hd:ML_HEADS + hd + 1]
        i_row = g_t[hd:hd + 1, :]
        f_row = lf_t[ML_HEADS + hd:ML_HEADS + hd + 1, :]

        b_col = jnp.sum(jnp.where(lower, f_row, 0.0), axis=-1, keepdims=True)
        b_row = jnp.sum(jnp.where(row <= col, f_col, 0.0), axis=0, keepdims=True)
        b_last = b_col[L - 1:L, :]

        m_state = m_ref[hd:hd + 1, 0:1]
        n_state = n_ref[hd:hd + 1, :]
        c_state = c_ref[hd]

        dmat = jnp.where(lower, b_col - b_row + i_row, -jnp.inf)
        r = jnp.max(dmat, axis=-1, keepdims=True)
        p = jnp.exp(dmat - r)
        qb = q.astype(BF16)
        kb = k.astype(BF16)
        sc = lax.dot_general(qb, kb, (((1,), (1,)), ((), ())), preferred_element_type=F32) * p
        num_a = jnp.dot(sc.astype(BF16), v, preferred_element_type=F32)
        den_a = jnp.sum(sc, axis=-1, keepdims=True)

        inter = b_col + m_state
        m_row = jnp.maximum(r, inter)
        e_a = jnp.exp(r - m_row)
        e_b = jnp.exp(inter - m_row)
        num = e_a * num_a + e_b * jnp.dot(qb, c_state.astype(BF16), preferred_element_type=F32)
        den = e_a * den_a + e_b * jnp.sum(q * n_state, axis=-1, keepdims=True)
        hval = num / jnp.maximum(jnp.abs(den), jnp.exp(-m_row))

        w_col = b_last - b_col + i_col
        a = jnp.max(w_col, axis=0, keepdims=True)
        m_new = jnp.maximum(b_last + m_state, a)
        kw = k * jnp.exp(w_col - a)
        d_c = lax.dot_general(kw.astype(BF16), v, (((0,), (0,)), ((), ())), preferred_element_type=F32)
        d_n = jnp.sum(kw, axis=0, keepdims=True)
        decay = jnp.exp(b_last + m_state - m_new)
        gain = jnp.exp(a - m_new)
        c_ref[hd] = decay * c_state + gain * d_c
        n_ref[hd:hd + 1, :] = decay * n_state + gain * d_n
        m_ref[hd:hd + 1, :] = jnp.broadcast_to(m_new, (1, LANES))

        hn = (hval * lax.rsqrt(jnp.mean(hval * hval, axis=-1, keepdims=True) + NORM_EPS)) * go_ref[:, sl]
        y_ref[0, :, sl] = (hn * jax.nn.sigmoid(mo_ref[0, :, sl])).astype(BF16)


def _mlstm(mqk, mv, mo, gif, conv_w, conv_b, gate_b, g_out):
    b_sz, s_len, _ = mqk.shape
    L = min(L_MLSTM, s_len)
    grid = (b_sz, s_len // L)
    tok = lambda width: pl.BlockSpec((1, L, width), lambda b, c: (b, c, 0))
    full = lambda shape: pl.BlockSpec(shape, lambda b, c: (0,) * len(shape))
    return pl.pallas_call(
        functools.partial(_mlstm_kernel, L=L),
        grid=grid,
        in_specs=[tok(2 * ML_WIDTH), tok(ML_WIDTH), tok(ML_WIDTH), tok(LANES),
                  full((CONV_WIDTH, 2 * ML_WIDTH)), full((1, 2 * ML_WIDTH)), full((1, LANES)),
                  full((1, ML_WIDTH))],
        out_specs=tok(ML_WIDTH),
        out_shape=jax.ShapeDtypeStruct((b_sz, s_len, ML_WIDTH), BF16),
        scratch_shapes=[
            pltpu.VMEM((L + SUBLANES, 2 * ML_WIDTH), F32),
            pltpu.VMEM((ML_HEADS, ML_HEAD_DIM, ML_HEAD_DIM), F32),
            pltpu.VMEM((SUBLANES, LANES), F32),
            pltpu.VMEM((SUBLANES, LANES), F32),
        ],
        compiler_params=_cparams(("parallel", "arbitrary")),
        name="mlstm",
    )(mqk, mv, mo, gif, conv_w, conv_b, gate_b, g_out)


def _merge_kernel(x_ref, yda_ref, yml_ref, ga_ref, wg_ref, bg_ref, wda_ref, wml_ref, wo_ref,
                  gf_ref, wr_ref, br_ref,
                  x1_ref, hn_ref, rc_ref, rt_ref, cnt_ref, carry_ref, *, tm):
    @pl.when(pl.program_id(0) == 0)
    def _():
        carry_ref[...] = jnp.zeros_like(carry_ref)

    x = x_ref[...]
    h = ((x * lax.rsqrt(jnp.mean(x * x, axis=-1, keepdims=True) + NORM_EPS)) * ga_ref[...]).astype(BF16)
    gates = jax.nn.sigmoid(jnp.dot(h, wg_ref[...], preferred_element_type=F32) + bg_ref[...])
    a = jnp.dot(yda_ref[...], wda_ref[...], preferred_element_type=F32)
    c = jnp.dot(yml_ref[...], wml_ref[...], preferred_element_type=F32)
    mixed = gates[:, :D_MODEL] * a + gates[:, D_MODEL:] * c
    x1 = x + jnp.dot(mixed.astype(BF16), wo_ref[...], preferred_element_type=F32)
    x1_ref[...] = x1
    hn = (x1 * lax.rsqrt(jnp.mean(x1 * x1, axis=-1, keepdims=True) + NORM_EPS)) * gf_ref[...]
    hn_ref[...] = hn

    logits = jnp.dot(hn.astype(BF16), wr_ref[...], preferred_element_type=F32) + br_ref[...]
    lane = _lane_iota((tm, LANES))
    neg = -jnp.inf
    big = jnp.int32(LANES)

    gl = jnp.where((lane >= N_EXPERTS) & (lane < N_EXPERTS + N_GROUPS), logits, neg)
    gmax = jnp.max(gl, axis=-1, keepdims=True)
    gsum = jnp.sum(jnp.exp(gl - gmax), axis=-1, keepdims=True)
    g_top = 1.0 / gsum
    g_idx = jnp.min(jnp.where(gl == gmax, lane, big), axis=-1, keepdims=True) - N_EXPERTS

    el = jnp.where((lane < N_EXPERTS) & ((lane >> 3) == g_idx), logits, neg)
    emax = jnp.max(el, axis=-1, keepdims=True)
    esum = jnp.sum(jnp.exp(el - emax), axis=-1, keepdims=True)
    e0 = jnp.min(jnp.where(el == emax, lane, big), axis=-1, keepdims=True)
    el2 = jnp.where(lane == e0, neg, el)
    emax2 = jnp.max(el2, axis=-1, keepdims=True)
    e1 = jnp.min(jnp.where(el2 == emax2, lane, big), axis=-1, keepdims=True)
    p0 = 1.0 / esum
    p1 = jnp.exp(emax2 - emax) / esum
    tot = p0 + p1
    w0 = g_top * (p0 / tot)
    w1 = g_top * (p1 / tot)

    hit0 = lane == e0
    hit1 = lane == e1
    onehot = (hit0 | hit1).astype(F32)
    before = (_lane_iota((tm, tm)) < _row_iota((tm, tm))).astype(BF16)
    prefix = jnp.dot(before, onehot.astype(BF16), preferred_element_type=F32) + carry_ref[...]
    r0 = jnp.sum(jnp.where(hit0, prefix, 0.0), axis=-1, keepdims=True)
    r1 = jnp.sum(jnp.where(hit1, prefix, 0.0), axis=-1, keepdims=True)
    carry_ref[...] = carry_ref[...] + jnp.sum(onehot, axis=0, keepdims=True)
    cnt_ref[...] = carry_ref[...]

    rc = jnp.where(lane == 0, e0.astype(F32),
         jnp.where(lane == 1, e1.astype(F32),
         jnp.where(lane == 2, r0,
         jnp.where(lane == 3, r1,
         jnp.where(lane == 4, w0,
         jnp.where(lane == 5, w1, 0.0))))))
    rc_ref[...] = rc
    rt_ref[...] = rc.T[0:SUBLANES, :].astype(jnp.int32)


def _merge(x2, yda, yml, g_attn, w_gate, b_gate, w_da, w_ml, w_out, g_ffn, w_rt, b_rt):
    n_tok, d = x2.shape
    tm = min(TM_MERGE, n_tok)
    grid = (n_tok // tm,)
    tok = lambda width: pl.BlockSpec((tm, width), lambda i: (i, 0))
    full = lambda shape: pl.BlockSpec(shape, lambda i: (0,) * len(shape))
    return pl.pallas_call(
        functools.partial(_merge_kernel, tm=tm),
        grid=grid,
        in_specs=[tok(d), tok(DA_WIDTH), tok(ML_WIDTH), full((1, d)), full((d, 2 * d)), full((1, 2 * d)),
                  full((DA_WIDTH, d)), full((ML_WIDTH, d)), full((d, d)), full((1, d)),
                  full((d, LANES)), full((1, LANES))],
        out_specs=[tok(d), tok(d), tok(LANES), pl.BlockSpec((SUBLANES, tm), lambda i: (0, i)),
                   full((1, LANES))],
        out_shape=[
            jax.ShapeDtypeStruct((n_tok, d), F32),
            jax.ShapeDtypeStruct((n_tok, d), F32),
            jax.ShapeDtypeStruct((n_tok, LANES), F32),
            jax.ShapeDtypeStruct((SUBLANES, n_tok), jnp.int32),
            jax.ShapeDtypeStruct((1, LANES), F32),
        ],
        scratch_shapes=[pltpu.VMEM((1, LANES), F32)],
        compiler_params=_cparams(("arbitrary",)),
        name="merge",
    )(x2, yda, yml, g_attn, w_gate, b_gate, w_da, w_ml, w_out, g_ffn, w_rt, b_rt)


def _load_indices(idx_refs, smem_refs, sem, base, tm):
    copies = [pltpu.make_async_copy(src.at[pl.ds(base, tm)], dst, sem.at[n])
              for n, (src, dst) in enumerate(zip(idx_refs, smem_refs))]
    for cp in copies:
        cp.start()
    for cp in copies:
        cp.wait()


def _dispatch_kernel(pstart_ref, e0_ref, e1_ref, r0_ref, r1_ref, hn_ref, zero_ref, xs_ref,
                     e0_s, e1_s, r0_s, r1_s, isem, sem, *, tm):
    del zero_ref
    base = pl.multiple_of(pl.program_id(0) * tm, tm)
    _load_indices((e0_ref, e1_ref, r0_ref, r1_ref), (e0_s, e1_s, r0_s, r1_s), isem, base, tm)

    def row_copy(t, dest):
        return pltpu.make_async_copy(hn_ref.at[pl.ds(t, 1)], xs_ref.at[pl.ds(dest, 1)], sem)

    def issue(t, _):
        row_copy(t, pstart_ref[e0_s[t]] + r0_s[t]).start()
        row_copy(t, pstart_ref[e1_s[t]] + r1_s[t]).start()
        return 0

    lax.fori_loop(0, tm, issue, 0)

    for _ in range(2):
        pltpu.make_async_copy(hn_ref, xs_ref.at[pl.ds(0, tm)], sem).wait()


def _dispatch(pstart, e0, e1, r0, r1, hn, cap):
    n_tok, d = hn.shape
    tm = min(TM_ROWS, n_tok)
    grid = (n_tok // tm,)
    any_spec = pl.BlockSpec(memory_space=pl.ANY)
    zeros = jnp.zeros((cap, d), F32)
    return pl.pallas_call(
        functools.partial(_dispatch_kernel, tm=tm),
        grid_spec=pltpu.PrefetchScalarGridSpec(
            num_scalar_prefetch=1,
            grid=grid,
            in_specs=[any_spec, any_spec, any_spec, any_spec,
                      pl.BlockSpec((tm, d), lambda i, ps: (i, 0)), any_spec],
            out_specs=any_spec,
            scratch_shapes=[pltpu.SMEM((tm,), jnp.int32)] * 4
            + [pltpu.SemaphoreType.DMA((4,)), pltpu.SemaphoreType.DMA],
        ),
        out_shape=jax.ShapeDtypeStruct((cap, d), F32),
        input_output_aliases={6: 0},
        compiler_params=_cparams(("arbitrary",)),
        name="dispatch",
    )(pstart, e0, e1, r0, r1, hn, zeros)


def _experts_kernel(be_ref, nb_ref, xs_ref, w1_ref, w3_ref, w2_ref, y_ref):
    del be_ref

    @pl.when(pl.program_id(0) < nb_ref[0])
    def _():
        xb = xs_ref[...].astype(BF16)
        a = jnp.dot(xb, w1_ref[0], preferred_element_type=F32)
        b = jnp.dot(xb, w3_ref[0], preferred_element_type=F32)
        hid = (a * jax.nn.sigmoid(a)) * b
        y_ref[...] = jnp.dot(hid.astype(BF16), w2_ref[0], preferred_element_type=F32)

    @pl.when(pl.program_id(0) >= nb_ref[0])
    def _():
        y_ref[...] = jnp.zeros_like(y_ref)


def _experts(blk_expert, n_blocks_used, xs, w1, w3, w2):
    cap, d = xs.shape
    n_blocks = cap // EXPERT_BLK
    rows = lambda i, be, nb: (jnp.minimum(i, nb[0] - 1), 0)
    wsel = lambda i, be, nb: (be[jnp.minimum(i, nb[0] - 1)], 0, 0)
    return pl.pallas_call(
        _experts_kernel,
        grid_spec=pltpu.PrefetchScalarGridSpec(
            num_scalar_prefetch=2,
            grid=(n_blocks,),
            in_specs=[pl.BlockSpec((EXPERT_BLK, d), rows),
                      pl.BlockSpec((1, d, EXPERT_FF), wsel),
                      pl.BlockSpec((1, d, EXPERT_FF), wsel),
                      pl.BlockSpec((1, EXPERT_FF, d), wsel)],
            out_specs=pl.BlockSpec((EXPERT_BLK, d), lambda i, be, nb: (i, 0)),
        ),
        out_shape=jax.ShapeDtypeStruct((cap, d), F32),
        compiler_params=_cparams(("arbitrary",)),
        name="experts",
    )(blk_expert, n_blocks_used, xs, w1, w3, w2)


def _combine_kernel(pstart_ref, e0_ref, e1_ref, r0_ref, r1_ref, x1_ref, rc_ref, ys_ref, o_ref,
                    e0_s, e1_s, r0_s, r1_s, ybuf, isem, sem, *, tm):
    base = pl.multiple_of(pl.program_id(0) * tm, tm)
    _load_indices((e0_ref, e1_ref, r0_ref, r1_ref), (e0_s, e1_s, r0_s, r1_s), isem, base, tm)

    def row_copy(slot, t, src):
        return pltpu.make_async_copy(ys_ref.at[pl.ds(src, 1)], ybuf.at[slot, pl.ds(t, 1)], sem)

    def issue(t, _):
        row_copy(0, t, pstart_ref[e0_s[t]] + r0_s[t]).start()
        row_copy(1, t, pstart_ref[e1_s[t]] + r1_s[t]).start()
        return 0

    lax.fori_loop(0, tm, issue, 0)

    for slot in range(2):
        pltpu.make_async_copy(ys_ref.at[pl.ds(0, tm)], ybuf.at[slot], sem).wait()

    rc = rc_ref[...]
    o_ref[...] = x1_ref[...] + (rc[:, 4:5] * ybuf[0] + rc[:, 5:6] * ybuf[1])


def _combine(pstart, e0, e1, r0, r1, x1, rc, ys):
    n_tok, d = x1.shape
    tm = min(TM_ROWS, n_tok)
    grid = (n_tok // tm,)
    any_spec = pl.BlockSpec(memory_space=pl.ANY)
    return pl.pallas_call(
        functools.partial(_combine_kernel, tm=tm),
        grid_spec=pltpu.PrefetchScalarGridSpec(
            num_scalar_prefetch=1,
            grid=grid,
            in_specs=[any_spec, any_spec, any_spec, any_spec,
                      pl.BlockSpec((tm, d), lambda i, ps: (i, 0)),
                      pl.BlockSpec((tm, LANES), lambda i, ps: (i, 0)),
                      any_spec],
            out_specs=pl.BlockSpec((tm, d), lambda i, ps: (i, 0)),
            scratch_shapes=[pltpu.SMEM((tm,), jnp.int32)] * 4
            + [pltpu.VMEM((2, tm, d), F32), pltpu.SemaphoreType.DMA((4,)), pltpu.SemaphoreType.DMA],
        ),
        out_shape=jax.ShapeDtypeStruct((n_tok, d), F32),
        compiler_params=_cparams(("arbitrary",)),
        name="combine",
    )(pstart, e0, e1, r0, r1, x1, rc, ys)


def _pad_lanes(vec, width=LANES):
    return jnp.zeros((1, width), F32).at[0, :vec.shape[0]].set(vec.astype(F32))


def _layout_w_in(w_in):
    d = w_in.shape[0]
    qk = w_in[:, :2 * 512].reshape(d, 2, DA_HEADS * 2, DA_HEAD_DIM)
    qk = jnp.concatenate([qk, jnp.zeros_like(qk)], axis=-1).reshape(d, 2 * 1024)
    gates = jnp.concatenate([w_in[:, 3584:3592], jnp.zeros((d, LANES - 2 * ML_HEADS), w_in.dtype)], axis=-1)
    return jnp.concatenate([qk, w_in[:, 1024:3584], gates], axis=-1).astype(BF16)


def kernel(x, attn_norm_g, w_in, da_q_norm_g, da_k_norm_g, da_lambda_q1, da_lambda_k1, da_lambda_q2, da_lambda_k2, da_out_norm_g, ml_conv_w, ml_conv_b, ml_i_bias, ml_f_bias, ml_out_norm_g, w_branch_da, w_branch_ml, w_gate, b_gate, w_out, ffn_norm_g, w_group, b_group, w_router, b_router, w1, w3, w2):
    b_sz, s_len, d = x.shape
    n_tok = b_sz * s_len
    ly = 0

    w_all = _layout_w_in(w_in[ly])
    g_attn = attn_norm_g[ly].reshape(1, d)
    gq = _pad_lanes(da_q_norm_g[ly])
    gk = _pad_lanes(da_k_norm_g[ly])
    lam_vecs = jnp.concatenate([_pad_lanes(v[ly]) for v in
                                (da_lambda_q1, da_lambda_k1, da_lambda_q2, da_lambda_k2)], axis=0)
    gate_b = _pad_lanes(jnp.concatenate([ml_i_bias[ly], ml_f_bias[ly]]))
    w_rt = jnp.concatenate([w_router[ly], w_group[ly],
                            jnp.zeros((d, LANES - N_EXPERTS - N_GROUPS), F32)], axis=-1).astype(BF16)
    b_rt = _pad_lanes(jnp.concatenate([b_router[ly], b_group[ly]]))

    qa, ka, v, mqk, mv, mo, gif = _in_proj(x, g_attn, w_all, gq, gk)
    y_da = _attn(lam_vecs, qa, ka, v, da_out_norm_g[ly].reshape(1, DA_V_DIM))
    y_ml = _mlstm(mqk, mv, mo, gif, ml_conv_w[ly], ml_conv_b[ly].reshape(1, -1), gate_b,
                  ml_out_norm_g[ly].reshape(1, ML_WIDTH))

    x1, hn, rc, rt, counts = _merge(
        x.reshape(n_tok, d), y_da.reshape(n_tok, DA_WIDTH), y_ml.reshape(n_tok, ML_WIDTH), g_attn,
        w_gate[ly].astype(BF16), b_gate[ly].reshape(1, -1), w_branch_da[ly].astype(BF16),
        w_branch_ml[ly].astype(BF16), w_out[ly].astype(BF16), ffn_norm_g[ly].reshape(1, d), w_rt, b_rt)

    cnt = counts[0, :N_EXPERTS].astype(jnp.int32)
    padded = (cnt + EXPERT_BLK - 1) // EXPERT_BLK * EXPERT_BLK
    pend = jnp.cumsum(padded)
    pstart = (pend - padded).astype(jnp.int32)
    cap = 2 * n_tok + N_EXPERTS * EXPERT_BLK
    n_blocks = cap // EXPERT_BLK
    blk_expert = jnp.minimum(
        jnp.searchsorted(pend, jnp.arange(n_blocks, dtype=jnp.int32) * EXPERT_BLK, side='right'),
        N_EXPERTS - 1).astype(jnp.int32)
    n_used = (pend[-1:] // EXPERT_BLK).astype(jnp.int32)
    e0, e1, r0, r1 = rt[0], rt[1], rt[2], rt[3]

    xs = _dispatch(pstart, e0, e1, r0, r1, hn, cap)
    ys = _experts(blk_expert, n_used, xs, w1[ly].astype(BF16), w3[ly].astype(BF16), w2[ly].astype(BF16))
    out = _combine(pstart, e0, e1, r0, r1, x1, rc, ys)
    return out.reshape(b_sz, s_len, d)
```

```python
import functools
import math

import jax
import jax.numpy as jnp
import numpy as np
from jax import lax
from jax.experimental import pallas as pl
from jax.experimental.pallas import tpu as pltpu

F32 = jnp.float32
BF16 = jnp.bfloat16

D_MODEL = 1024
DA_HEADS = 4
DA_HEAD_DIM = 64
DA_V_DIM = 128
DA_WIDTH = 512
ML_HEADS = 4
ML_HEAD_DIM = 128
ML_WIDTH = 512
CONV_WIDTH = 4
N_GROUPS = 4
EXPERTS_PER_GROUP = 8
N_EXPERTS = 32
EXPERT_FF = 512
NORM_EPS = 1e-6
LAM_INIT = 0.8 - 0.6 * math.exp(-0.3 * 0)

LANES = 128
SUBLANES = 8
VMEM_LIMIT = 56 * 1024 * 1024

SEG_K = 0
SEG_MQK = SEG_K + 1024
SEG_MV = SEG_MQK + 1024
SEG_MO = SEG_MV + 512
SEG_IF = SEG_MO + 512
W_ALL = SEG_IF + LANES

LOG2E = math.log2(math.e)
ALIBI_PIECES = 4
AUX0 = DA_HEAD_DIM
POS_PERIOD = 512
DV_AUG = DA_V_DIM + 16


def _bf16_round(val):
    bits = np.float32(val).view(np.uint32)
    bits = (bits + np.uint32(0x7FFF) + ((bits >> np.uint32(16)) & np.uint32(1))) & np.uint32(0xFFFF0000)
    return float(bits.view(np.float32))


def _alibi_pieces(hd):
    rest, pieces = 2.0 ** (-8.0 * (hd + 1) / DA_HEADS) * LOG2E, []
    for _ in range(ALIBI_PIECES):
        pieces.append(_bf16_round(rest))
        rest -= pieces[-1]
    return pieces

TM_PROJ = 256
TQ_ATTN = 512
L_MLSTM = 128
TM_MERGE = 256
TM_ROWS = 1024
EXPERT_BLK = 256


def _cparams(sem):
    return pltpu.CompilerParams(dimension_semantics=sem, vmem_limit_bytes=VMEM_LIMIT)


def _lane_iota(shape):
    return lax.broadcasted_iota(jnp.int32, shape, len(shape) - 1)


def _row_iota(shape):
    return lax.broadcasted_iota(jnp.int32, shape, len(shape) - 2)


def _in_proj_kernel(x_ref, g_ref, w_ref, wt_ref, gq_ref, gk_ref, aux_ref,
                    qt_ref, ka_ref, vt_ref, mqk_ref, mv_ref, mo_ref, gif_ref, *, tm):
    x = x_ref[0]
    ms = jnp.mean(x * x, axis=-1, keepdims=True)
    h = ((x * lax.rsqrt(ms + NORM_EPS)) * g_ref[...]).astype(BF16)

    def seg(lo, width):
        return jnp.dot(h, w_ref[:, lo:lo + width], preferred_element_type=F32)

    inv_dh = 1.0 / DA_HEAD_DIM

    t_all = lax.dot_general(wt_ref[...], h, (((1,), (1,)), ((), ())), preferred_element_type=F32)
    for hd in range(DA_HEADS):
        for mp in range(2):
            off = (hd * 2 + mp) * LANES
            q = t_all[off:off + LANES, :]
            qn = (q * lax.rsqrt(jnp.sum(q * q, axis=0, keepdims=True) * inv_dh + NORM_EPS)) * gq_ref[...]
            qn = qn * (DA_HEAD_DIM ** -0.5 * LOG2E) + aux_ref[hd]
            qt_ref[0, hd, mp * LANES:(mp + 1) * LANES, :] = qn.astype(BF16)
        v_lo = 2 * DA_HEADS * LANES + hd * DA_V_DIM
        vt_ref[0, hd, 0:DA_V_DIM, :] = t_all[v_lo:v_lo + DA_V_DIM, :].astype(BF16)
        ones_row = (_row_iota((DV_AUG - DA_V_DIM, tm)) == 0).astype(BF16)
        vt_ref[0, hd, DA_V_DIM:DV_AUG, :] = ones_row

    lane = _lane_iota((tm, LANES))
    pos = pl.program_id(1) * tm + _row_iota((tm, LANES))
    pos_lo = (pos & 255).astype(F32)
    pos_hi = (pos & 256).astype(F32)
    is_aux = (lane >= AUX0) & (lane < AUX0 + 2 * ALIBI_PIECES)
    is_lo = is_aux & ((lane & 1) == 0)
    is_hi = is_aux & ((lane & 1) == 1)
    for hd in range(DA_HEADS):
        for mp in range(2):
            off = (hd * 2 + mp) * LANES
            k = seg(SEG_K + off, LANES)
            kn = (k * lax.rsqrt(jnp.sum(k * k, axis=-1, keepdims=True) * inv_dh + NORM_EPS)) * gk_ref[...]
            kn = jnp.where(is_lo, pos_lo, jnp.where(is_hi, pos_hi, kn))
            ka_ref[0, hd, :, mp * LANES:(mp + 1) * LANES] = kn.astype(BF16)

    mqk_ref[0] = seg(SEG_MQK, 2 * ML_WIDTH)
    mv_ref[0] = seg(SEG_MV, ML_WIDTH).astype(BF16)
    mo_ref[0] = seg(SEG_MO, ML_WIDTH)
    gif_ref[0] = seg(SEG_IF, LANES)


def _in_proj(x, g_attn, w_all, w_t, gq_col, gk):
    b_sz, s_len, d = x.shape
    tm = min(TM_PROJ, s_len)
    grid = (b_sz, s_len // tm)
    tok = lambda width: pl.BlockSpec((1, tm, width), lambda b, s: (b, s, 0))
    full = lambda shape: pl.BlockSpec(shape, lambda b, s: (0,) * len(shape))
    n_t = w_t.shape[0]
    aux = np.zeros((DA_HEADS, LANES, 1), np.float32)
    for hd in range(DA_HEADS):
        aux[hd, AUX0:AUX0 + 2 * ALIBI_PIECES, 0] = np.repeat(_alibi_pieces(hd), 2)
    return pl.pallas_call(
        functools.partial(_in_proj_kernel, tm=tm),
        grid=grid,
        in_specs=[tok(d), full((1, d)), full((d, W_ALL)), full((n_t, d)), full((LANES, 1)), full((1, LANES)),
                  full((DA_HEADS, LANES, 1))],
        out_specs=[pl.BlockSpec((1, DA_HEADS, 2 * LANES, tm), lambda b, s: (b, 0, 0, s)),
                   pl.BlockSpec((1, DA_HEADS, tm, 2 * LANES), lambda b, s: (b, 0, s, 0)),
                   pl.BlockSpec((1, DA_HEADS, DV_AUG, tm), lambda b, s: (b, 0, 0, s)),
                   tok(2 * ML_WIDTH), tok(ML_WIDTH), tok(ML_WIDTH), tok(LANES)],
        out_shape=[
            jax.ShapeDtypeStruct((b_sz, DA_HEADS, 2 * LANES, s_len), BF16),
            jax.ShapeDtypeStruct((b_sz, DA_HEADS, s_len, 2 * LANES), BF16),
            jax.ShapeDtypeStruct((b_sz, DA_HEADS, DV_AUG, s_len), BF16),
            jax.ShapeDtypeStruct((b_sz, s_len, 2 * ML_WIDTH), F32),
            jax.ShapeDtypeStruct((b_sz, s_len, ML_WIDTH), BF16),
            jax.ShapeDtypeStruct((b_sz, s_len, ML_WIDTH), F32),
            jax.ShapeDtypeStruct((b_sz, s_len, LANES), F32),
        ],
        compiler_params=_cparams(("parallel", "parallel")),
        name="in_proj",
    )(x, g_attn, w_all, w_t, gq_col, gk, jnp.asarray(aux))


def _attn_kernel(lam_ref, qt_ref, ka_ref, vt_ref, go_ref, o_ref, acc_ref, *, tq):
    hd = pl.program_id(1)
    qi = pl.program_id(2)
    slopes = [sum(_alibi_pieces(h)) for h in range(DA_HEADS)]
    slope = jnp.where(hd == 0, slopes[0], jnp.where(hd == 1, slopes[1], jnp.where(hd == 2, slopes[2], slopes[3])))
    qt = qt_ref[0, 0]
    qts = (qt[:LANES], qt[LANES:])
    causal = _row_iota((tq, tq)) <= _lane_iota((tq, tq))
    acc_ref[...] = jnp.zeros_like(acc_ref)

    def tile(kv, carry, masked):
        start = pl.multiple_of(kv * tq, tq)
        k = ka_ref[0, 0, pl.ds(start, tq), :]
        vt = vt_ref[0, 0, :, pl.ds(start, tq)]
        base = (start // POS_PERIOD) * POS_PERIOD - qi * tq
        c = (jnp.zeros((1, tq), jnp.int32) + base).astype(F32) * slope
        scores = [jnp.dot(k[:, mp * LANES:(mp + 1) * LANES], qts[mp], preferred_element_type=F32)
                  for mp in range(2)]
        probs, out = [], []
        for mp in range(2):
            s = scores[mp]
            if masked:
                s = jnp.where(causal, s, -jnp.inf)
            m_new = jnp.maximum(carry[mp], jnp.max(s, axis=0, keepdims=True) + c)
            probs.append((jnp.exp2(carry[mp] - m_new), jnp.exp2(s - (m_new - c)).astype(BF16)))
            out.append(m_new)
        for mp in range(2):
            alpha, p = probs[mp]
            acc_ref[mp] = alpha * acc_ref[mp] + jnp.dot(vt, p, preferred_element_type=F32)
        return tuple(out)

    init_one = jnp.full((1, tq), -jnp.inf, F32)
    carry = lax.fori_loop(0, qi, lambda kv, cr: tile(kv, cr, False), (init_one, init_one))
    tile(qi, carry, True)

    lam_v = lam_ref[...]
    lam = (jnp.exp(jnp.sum(lam_v[0:1] * lam_v[1:2], axis=-1, keepdims=True))
           - jnp.exp(jnp.sum(lam_v[2:3] * lam_v[3:4], axis=-1, keepdims=True)) + LAM_INIT)
    l1 = acc_ref[0, DA_V_DIM:DA_V_DIM + 1, :]
    l2 = acc_ref[1, DA_V_DIM:DA_V_DIM + 1, :]
    o = acc_ref[0, 0:DA_V_DIM, :] / l1 - lam * (acc_ref[1, 0:DA_V_DIM, :] / l2)
    o = (o * lax.rsqrt(jnp.mean(o * o, axis=0, keepdims=True) + NORM_EPS)) * go_ref[...]
    o_ref[0] = (o * (1.0 - LAM_INIT)).T.astype(BF16)


def _attn(lam_vecs, qt, ka, vt, g_out_col):
    b_sz, _, s_len, _ = ka.shape
    tq = min(TQ_ATTN, s_len)
    grid = (b_sz, DA_HEADS, s_len // tq)
    return pl.pallas_call(
        functools.partial(_attn_kernel, tq=tq),
        grid=grid,
        in_specs=[
            pl.BlockSpec((4, LANES), lambda b, h, i: (0, 0)),
            pl.BlockSpec((1, 1, 2 * LANES, tq), lambda b, h, i: (b, h, 0, i)),
            pl.BlockSpec((1, 1, s_len, 2 * LANES), lambda b, h, i: (b, h, 0, 0)),
            pl.BlockSpec((1, 1, DV_AUG, s_len), lambda b, h, i: (b, h, 0, 0)),
            pl.BlockSpec((DA_V_DIM, 1), lambda b, h, i: (0, 0)),
        ],
        out_specs=pl.BlockSpec((1, tq, DA_V_DIM), lambda b, h, i: (b, i, h)),
        out_shape=jax.ShapeDtypeStruct((b_sz, s_len, DA_WIDTH), BF16),
        scratch_shapes=[pltpu.VMEM((2, DV_AUG, tq), F32)],
        compiler_params=_cparams(("parallel", "parallel", "arbitrary")),
        name="attn",
    )(lam_vecs, qt, ka, vt, g_out_col)


def _log_sigmoid(x):
    return jnp.minimum(x, 0.0) - jnp.log1p(jnp.exp(-jnp.abs(x)))


def _mlstm_kernel(mqk_ref, mv_ref, mo_ref, gif_ref, cw_ref, cb_ref, gb_ref, go_ref,
                  y_ref, buf_ref, c_ref, n_ref, m_ref, *, L):
    halo = SUBLANES

    @pl.when(pl.program_id(1) == 0)
    def _():
        buf_ref[0:halo, :] = jnp.zeros((halo, 2 * ML_WIDTH), F32)
        c_ref[...] = jnp.zeros_like(c_ref)
        n_ref[...] = jnp.zeros_like(n_ref)
        m_ref[...] = jnp.zeros_like(m_ref)

    buf_ref[halo:halo + L, :] = mqk_ref[0]
    conv = jnp.broadcast_to(cb_ref[...], (L, 2 * ML_WIDTH))
    for j in range(CONV_WIDTH):
        lo = halo - (CONV_WIDTH - 1) + j
        conv = conv + buf_ref[lo:lo + L, :] * cw_ref[j:j + 1, :]
    buf_ref[0:halo, :] = buf_ref[L:L + halo, :]
    qk = conv * jax.nn.sigmoid(conv)

    g = gif_ref[0] + gb_ref[...]
    lf = _log_sigmoid(g)
    g_t = g.T
    lf_t = lf.T

    row = _row_iota((L, L))
    col = _lane_iota((L, L))
    lower = col <= row

    for hd in range(ML_HEADS):
        sl = slice(hd * ML_HEAD_DIM, (hd + 1) * ML_HEAD_DIM)
        q = qk[:, sl] * (ML_HEAD_DIM ** -0.5)
        k = qk[:, ML_WIDTH + hd * ML_HEAD_DIM:ML_WIDTH + (hd + 1) * ML_HEAD_DIM]
        v = mv_ref[0, :, sl]
        i_col = g[:, hd:hd + 1]
        f_col = lf[:, ML_HEADS + hd:ML_HEADS + hd + 1]
        i_row = g_t[hd:hd + 1, :]
        f_row = lf_t[ML_HEADS + hd:ML_HEADS + hd + 1, :]

        b_col = jnp.sum(jnp.where(lower, f_row, 0.0), axis=-1, keepdims=True)
        b_row = jnp.sum(jnp.where(row <= col, f_col, 0.0), axis=0, keepdims=True)
        b_last = b_col[L - 1:L, :]

        m_state = m_ref[hd:hd + 1, 0:1]
        n_state = n_ref[hd:hd + 1, :]
        c_state = c_ref[hd]

        dmat = jnp.where(lower, b_col - b_row + i_row, -jnp.inf)
        r = jnp.max(dmat, axis=-1, keepdims=True)
        p = jnp.exp(dmat - r)
        qb = q.astype(BF16)
        kb = k.astype(BF16)
        sc = lax.dot_general(qb, kb, (((1,), (1,)), ((), ())), preferred_element_type=F32) * p
        num_a = jnp.dot(sc.astype(BF16), v, preferred_element_type=F32)
        den_a = jnp.sum(sc, axis=-1, keepdims=True)

        inter = b_col + m_state
        m_row = jnp.maximum(r, inter)
        e_a = jnp.exp(r - m_row)
        e_b = jnp.exp(inter - m_row)
        num = e_a * num_a + e_b * jnp.dot(qb, c_state.astype(BF16), preferred_element_type=F32)
        den = e_a * den_a + e_b * jnp.sum(q * n_state, axis=-1, keepdims=True)
        hval = num / jnp.maximum(jnp.abs(den), jnp.exp(-m_row))

        w_col = b_last - b_col + i_col
        a = jnp.max(w_col, axis=0, keepdims=True)
        m_new = jnp.maximum(b_last + m_state, a)
        kw = k * jnp.exp(w_col - a)
        d_c = lax.dot_general(kw.astype(BF16), v, (((0,), (0,)), ((), ())), preferred_element_type=F32)
        d_n = jnp.sum(kw, axis=0, keepdims=True)
        decay = jnp.exp(b_last + m_state - m_new)
        gain = jnp.exp(a - m_new)
        c_ref[hd] = decay * c_state + gain * d_c
        n_ref[hd:hd + 1, :] = decay * n_state + gain * d_n
        m_ref[hd:hd + 1, :] = jnp.broadcast_to(m_new, (1, LANES))

        hn = (hval * lax.rsqrt(jnp.mean(hval * hval, axis=-1, keepdims=True) + NORM_EPS)) * go_ref[:, sl]
        y_ref[0, :, sl] = (hn * jax.nn.sigmoid(mo_ref[0, :, sl])).astype(BF16)


def _mlstm(mqk, mv, mo, gif, conv_w, conv_b, gate_b, g_out):
    b_sz, s_len, _ = mqk.shape
    L = min(L_MLSTM, s_len)
    grid = (b_sz, s_len // L)
    tok = lambda width: pl.BlockSpec((1, L, width), lambda b, c: (b, c, 0))
    full = lambda shape: pl.BlockSpec(shape, lambda b, c: (0,) * len(shape))
    return pl.pallas_call(
        functools.partial(_mlstm_kernel, L=L),
        grid=grid,
        in_specs=[tok(2 * ML_WIDTH), tok(ML_WIDTH), tok(ML_WIDTH), tok(LANES),
                  full((CONV_WIDTH, 2 * ML_WIDTH)), full((1, 2 * ML_WIDTH)), full((1, LANES)),
                  full((1, ML_WIDTH))],
        out_specs=tok(ML_WIDTH),
        out_shape=jax.ShapeDtypeStruct((b_sz, s_len, ML_WIDTH), BF16),
        scratch_shapes=[
            pltpu.VMEM((L + SUBLANES, 2 * ML_WIDTH), F32),
            pltpu.VMEM((ML_HEADS, ML_HEAD_DIM, ML_HEAD_DIM), F32),
            pltpu.VMEM((SUBLANES, LANES), F32),
            pltpu.VMEM((SUBLANES, LANES), F32),
        ],
        compiler_params=_cparams(("parallel", "arbitrary")),
        name="mlstm",
    )(mqk, mv, mo, gif, conv_w, conv_b, gate_b, g_out)


def _merge_kernel(x_ref, yda_ref, yml_ref, ga_ref, wg_ref, bg_ref, wda_ref, wml_ref, wo_ref,
                  gf_ref, wr_ref, br_ref,
                  x1_ref, hn_ref, rc_ref, rt_ref, cnt_ref, carry_ref, *, tm):
    @pl.when(pl.program_id(0) == 0)
    def _():
        carry_ref[...] = jnp.zeros_like(carry_ref)

    x = x_ref[...]
    h = ((x * lax.rsqrt(jnp.mean(x * x, axis=-1, keepdims=True) + NORM_EPS)) * ga_ref[...]).astype(BF16)
    gates = jax.nn.sigmoid(jnp.dot(h, wg_ref[...], preferred_element_type=F32) + bg_ref[...])
    a = jnp.dot(yda_ref[...], wda_ref[...], preferred_element_type=F32)
    c = jnp.dot(yml_ref[...], wml_ref[...], preferred_element_type=F32)
    mixed = gates[:, :D_MODEL] * a + gates[:, D_MODEL:] * c
    x1 = x + jnp.dot(mixed.astype(BF16), wo_ref[...], preferred_element_type=F32)
    x1_ref[...] = x1
    hn = (x1 * lax.rsqrt(jnp.mean(x1 * x1, axis=-1, keepdims=True) + NORM_EPS)) * gf_ref[...]
    hn_ref[...] = hn

    logits = jnp.dot(hn.astype(BF16), wr_ref[...], preferred_element_type=F32) + br_ref[...]
    lane = _lane_iota((tm, LANES))
    neg = -jnp.inf
    big = jnp.int32(LANES)

    gl = jnp.where((lane >= N_EXPERTS) & (lane < N_EXPERTS + N_GROUPS), logits, neg)
    gmax = jnp.max(gl, axis=-1, keepdims=True)
    gsum = jnp.sum(jnp.exp(gl - gmax), axis=-1, keepdims=True)
    g_top = 1.0 / gsum
    g_idx = jnp.min(jnp.where(gl == gmax, lane, big), axis=-1, keepdims=True) - N_EXPERTS

    el = jnp.where((lane < N_EXPERTS) & ((lane >> 3) == g_idx), logits, neg)
    emax = jnp.max(el, axis=-1, keepdims=True)
    esum = jnp.sum(jnp.exp(el - emax), axis=-1, keepdims=True)
    e0 = jnp.min(jnp.where(el == emax, lane, big), axis=-1, keepdims=True)
    el2 = jnp.where(lane == e0, neg, el)
    emax2 = jnp.max(el2, axis=-1, keepdims=True)
    e1 = jnp.min(jnp.where(el2 == emax2, lane, big), axis=-1, keepdims=True)
    p0 = 1.0 / esum
    p1 = jnp.exp(emax2 - emax) / esum
    tot = p0 + p1
    w0 = g_top * (p0 / tot)
    w1 = g_top * (p1 / tot)

    hit0 = lane == e0
    hit1 = lane == e1
    onehot = (hit0 | hit1).astype(F32)
    before = (_lane_iota((tm, tm)) < _row_iota((tm, tm))).astype(BF16)
    prefix = jnp.dot(before, onehot.astype(BF16), preferred_element_type=F32) + carry_ref[...]
    r0 = jnp.sum(jnp.where(hit0, prefix, 0.0), axis=-1, keepdims=True)
    r1 = jnp.sum(jnp.where(hit1, prefix, 0.0), axis=-1, keepdims=True)
    carry_ref[...] = carry_ref[...] + jnp.sum(onehot, axis=0, keepdims=True)
    cnt_ref[...] = carry_ref[...]

    rc = jnp.where(lane == 0, e0.astype(F32),
         jnp.where(lane == 1, e1.astype(F32),
         jnp.where(lane == 2, r0,
         jnp.where(lane == 3, r1,
         jnp.where(lane == 4, w0,
         jnp.where(lane == 5, w1, 0.0))))))
    rc_ref[...] = rc
    rt_ref[...] = rc.T[0:SUBLANES, :].astype(jnp.int32)


def _merge(x2, yda, yml, g_attn, w_gate, b_gate, w_da, w_ml, w_out, g_ffn, w_rt, b_rt):
    n_tok, d = x2.shape
    tm = min(TM_MERGE, n_tok)
    grid = (n_tok // tm,)
    tok = lambda width: pl.BlockSpec((tm, width), lambda i: (i, 0))
    full = lambda shape: pl.BlockSpec(shape, lambda i: (0,) * len(shape))
    return pl.pallas_call(
        functools.partial(_merge_kernel, tm=tm),
        grid=grid,
        in_specs=[tok(d), tok(DA_WIDTH), tok(ML_WIDTH), full((1, d)), full((d, 2 * d)), full((1, 2 * d)),
                  full((DA_WIDTH, d)), full((ML_WIDTH, d)), full((d, d)), full((1, d)),
                  full((d, LANES)), full((1, LANES))],
        out_specs=[tok(d), tok(d), tok(LANES), pl.BlockSpec((SUBLANES, tm), lambda i: (0, i)),
                   full((1, LANES))],
        out_shape=[
            jax.ShapeDtypeStruct((n_tok, d), F32),
            jax.ShapeDtypeStruct((n_tok, d), F32),
            jax.ShapeDtypeStruct((n_tok, LANES), F32),
            jax.ShapeDtypeStruct((SUBLANES, n_tok), jnp.int32),
            jax.ShapeDtypeStruct((1, LANES), F32),
        ],
        scratch_shapes=[pltpu.VMEM((1, LANES), F32)],
        compiler_params=_cparams(("arbitrary",)),
        name="merge",
    )(x2, yda, yml, g_attn, w_gate, b_gate, w_da, w_ml, w_out, g_ffn, w_rt, b_rt)


def _load_indices(idx_refs, smem_refs, sem, base, tm):
    copies = [pltpu.make_async_copy(src.at[pl.ds(base, tm)], dst, sem.at[n])
              for n, (src, dst) in enumerate(zip(idx_refs, smem_refs))]
    for cp in copies:
        cp.start()
    for cp in copies:
        cp.wait()


def _dispatch_kernel(pstart_ref, e0_ref, e1_ref, r0_ref, r1_ref, hn_ref, zero_ref, xs_ref,
                     e0_s, e1_s, r0_s, r1_s, isem, sem, *, tm):
    del zero_ref
    base = pl.multiple_of(pl.program_id(0) * tm, tm)
    _load_indices((e0_ref, e1_ref, r0_ref, r1_ref), (e0_s, e1_s, r0_s, r1_s), isem, base, tm)

    def row_copy(t, dest):
        return pltpu.make_async_copy(hn_ref.at[pl.ds(t, 1)], xs_ref.at[pl.ds(dest, 1)], sem)

    def issue(t, _):
        row_copy(t, pstart_ref[e0_s[t]] + r0_s[t]).start()
        row_copy(t, pstart_ref[e1_s[t]] + r1_s[t]).start()
        return 0

    lax.fori_loop(0, tm, issue, 0)

    for _ in range(2):
        pltpu.make_async_copy(hn_ref, xs_ref.at[pl.ds(0, tm)], sem).wait()


def _dispatch(pstart, e0, e1, r0, r1, hn, cap):
    n_tok, d = hn.shape
    tm = min(TM_ROWS, n_tok)
    grid = (n_tok // tm,)
    any_spec = pl.BlockSpec(memory_space=pl.ANY)
    zeros = jnp.zeros((cap, d), F32)
    return pl.pallas_call(
        functools.partial(_dispatch_kernel, tm=tm),
        grid_spec=pltpu.PrefetchScalarGridSpec(
            num_scalar_prefetch=1,
            grid=grid,
            in_specs=[any_spec, any_spec, any_spec, any_spec,
                      pl.BlockSpec((tm, d), lambda i, ps: (i, 0)), any_spec],
            out_specs=any_spec,
            scratch_shapes=[pltpu.SMEM((tm,), jnp.int32)] * 4
            + [pltpu.SemaphoreType.DMA((4,)), pltpu.SemaphoreType.DMA],
        ),
        out_shape=jax.ShapeDtypeStruct((cap, d), F32),
        input_output_aliases={6: 0},
        compiler_params=_cparams(("arbitrary",)),
        name="dispatch",
    )(pstart, e0, e1, r0, r1, hn, zeros)


def _experts_kernel(be_ref, nb_ref, xs_ref, w1_ref, w3_ref, w2_ref, y_ref):
    del be_ref

    @pl.when(pl.program_id(0) < nb_ref[0])
    def _():
        xb = xs_ref[...].astype(BF16)
        a = jnp.dot(xb, w1_ref[0], preferred_element_type=F32)
        b = jnp.dot(xb, w3_ref[0], preferred_element_type=F32)
        hid = (a * jax.nn.sigmoid(a)) * b
        y_ref[...] = jnp.dot(hid.astype(BF16), w2_ref[0], preferred_element_type=F32)

    @pl.when(pl.program_id(0) >= nb_ref[0])
    def _():
        y_ref[...] = jnp.zeros_like(y_ref)


def _experts(blk_expert, n_blocks_used, xs, w1, w3, w2):
    cap, d = xs.shape
    n_blocks = cap // EXPERT_BLK
    rows = lambda i, be, nb: (jnp.minimum(i, nb[0] - 1), 0)
    wsel = lambda i, be, nb: (be[jnp.minimum(i, nb[0] - 1)], 0, 0)
    return pl.pallas_call(
        _experts_kernel,
        grid_spec=pltpu.PrefetchScalarGridSpec(
            num_scalar_prefetch=2,
            grid=(n_blocks,),
            in_specs=[pl.BlockSpec((EXPERT_BLK, d), rows),
                      pl.BlockSpec((1, d, EXPERT_FF), wsel),
                      pl.BlockSpec((1, d, EXPERT_FF), wsel),
                      pl.BlockSpec((1, EXPERT_FF, d), wsel)],
            out_specs=pl.BlockSpec((EXPERT_BLK, d), lambda i, be, nb: (i, 0)),
        ),
        out_shape=jax.ShapeDtypeStruct((cap, d), F32),
        compiler_params=_cparams(("arbitrary",)),
        name="experts",
    )(blk_expert, n_blocks_used, xs, w1, w3, w2)


def _combine_kernel(pstart_ref, e0_ref, e1_ref, r0_ref, r1_ref, x1_ref, rc_ref, ys_ref, o_ref,
                    e0_s, e1_s, r0_s, r1_s, ybuf, isem, sem, *, tm):
    base = pl.multiple_of(pl.program_id(0) * tm, tm)
    _load_indices((e0_ref, e1_ref, r0_ref, r1_ref), (e0_s, e1_s, r0_s, r1_s), isem, base, tm)

    def row_copy(slot, t, src):
        return pltpu.make_async_copy(ys_ref.at[pl.ds(src, 1)], ybuf.at[slot, pl.ds(t, 1)], sem)

    def issue(t, _):
        row_copy(0, t, pstart_ref[e0_s[t]] + r0_s[t]).start()
        row_copy(1, t, pstart_ref[e1_s[t]] + r1_s[t]).start()
        return 0

    lax.fori_loop(0, tm, issue, 0)

    for slot in range(2):
        pltpu.make_async_copy(ys_ref.at[pl.ds(0, tm)], ybuf.at[slot], sem).wait()

    rc = rc_ref[...]
    o_ref[...] = x1_ref[...] + (rc[:, 4:5] * ybuf[0] + rc[:, 5:6] * ybuf[1])


def _combine(pstart, e0, e1, r0, r1, x1, rc, ys):
    n_tok, d = x1.shape
    tm = min(TM_ROWS, n_tok)
    grid = (n_tok // tm,)
    any_spec = pl.BlockSpec(memory_space=pl.ANY)
    return pl.pallas_call(
        functools.partial(_combine_kernel, tm=tm),
        grid_spec=pltpu.PrefetchScalarGridSpec(
            num_scalar_prefetch=1,
            grid=grid,
            in_specs=[any_spec, any_spec, any_spec, any_spec,
                      pl.BlockSpec((tm, d), lambda i, ps: (i, 0)),
                      pl.BlockSpec((tm, LANES), lambda i, ps: (i, 0)),
                      any_spec],
            out_specs=pl.BlockSpec((tm, d), lambda i, ps: (i, 0)),
            scratch_shapes=[pltpu.SMEM((tm,), jnp.int32)] * 4
            + [pltpu.VMEM((2, tm, d), F32), pltpu.SemaphoreType.DMA((4,)), pltpu.SemaphoreType.DMA],
        ),
        out_shape=jax.ShapeDtypeStruct((n_tok, d), F32),
        compiler_params=_cparams(("arbitrary",)),
        name="combine",
    )(pstart, e0, e1, r0, r1, x1, rc, ys)


def _pad_lanes(vec, width=LANES):
    return jnp.zeros((1, width), F32).at[0, :vec.shape[0]].set(vec.astype(F32))


def _layout_w_in(w_in):
    d = w_in.shape[0]
    qk = w_in[:, :2 * 512].reshape(d, 2, DA_HEADS * 2, DA_HEAD_DIM)
    qk = jnp.concatenate([qk, jnp.zeros_like(qk)], axis=-1).reshape(d, 2, 1024)
    gates = jnp.concatenate([w_in[:, 3584:3592], jnp.zeros((d, LANES - 2 * ML_HEADS), w_in.dtype)], axis=-1)
    w_all = jnp.concatenate([qk[:, 1], w_in[:, 1536:3584], gates], axis=-1).astype(BF16)
    w_t = jnp.concatenate([qk[:, 0], w_in[:, 1024:1536]], axis=-1).T.astype(BF16)
    return w_all, w_t


def kernel(x, attn_norm_g, w_in, da_q_norm_g, da_k_norm_g, da_lambda_q1, da_lambda_k1, da_lambda_q2, da_lambda_k2, da_out_norm_g, ml_conv_w, ml_conv_b, ml_i_bias, ml_f_bias, ml_out_norm_g, w_branch_da, w_branch_ml, w_gate, b_gate, w_out, ffn_norm_g, w_group, b_group, w_router, b_router, w1, w3, w2):
    b_sz, s_len, d = x.shape
    n_tok = b_sz * s_len
    ly = 0

    w_all, w_t = _layout_w_in(w_in[ly])
    g_attn = attn_norm_g[ly].reshape(1, d)
    gq_col = _pad_lanes(da_q_norm_g[ly]).reshape(LANES, 1)
    gk = _pad_lanes(da_k_norm_g[ly])
    lam_vecs = jnp.concatenate([_pad_lanes(v[ly]) for v in
                                (da_lambda_q1, da_lambda_k1, da_lambda_q2, da_lambda_k2)], axis=0)
    gate_b = _pad_lanes(jnp.concatenate([ml_i_bias[ly], ml_f_bias[ly]]))
    w_rt = jnp.concatenate([w_router[ly], w_group[ly],
                            jnp.zeros((d, LANES - N_EXPERTS - N_GROUPS), F32)], axis=-1).astype(BF16)
    b_rt = _pad_lanes(jnp.concatenate([b_router[ly], b_group[ly]]))

    qt, ka, vt, mqk, mv, mo, gif = _in_proj(x, g_attn, w_all, w_t, gq_col, gk)
    y_da = _attn(lam_vecs, qt, ka, vt, da_out_norm_g[ly].reshape(DA_V_DIM, 1))
    y_ml = _mlstm(mqk, mv, mo, gif, ml_conv_w[ly], ml_conv_b[ly].reshape(1, -1), gate_b,
                  ml_out_norm_g[ly].reshape(1, ML_WIDTH))

    x1, hn, rc, rt, counts = _merge(
        x.reshape(n_tok, d), y_da.reshape(n_tok, DA_WIDTH), y_ml.reshape(n_tok, ML_WIDTH), g_attn,
        w_gate[ly].astype(BF16), b_gate[ly].reshape(1, -1), w_branch_da[ly].astype(BF16),
        w_branch_ml[ly].astype(BF16), w_out[ly].astype(BF16), ffn_norm_g[ly].reshape(1, d), w_rt, b_rt)

    cnt = counts[0, :N_EXPERTS].astype(jnp.int32)
    padded = (cnt + EXPERT_BLK - 1) // EXPERT_BLK * EXPERT_BLK
    pend = jnp.cumsum(padded)
    pstart = (pend - padded).astype(jnp.int32)
    cap = 2 * n_tok + N_EXPERTS * EXPERT_BLK
    n_blocks = cap // EXPERT_BLK
    blk_row0 = jnp.arange(n_blocks, dtype=jnp.int32) * EXPERT_BLK
    blk_expert = jnp.minimum(jnp.sum((pend[None, :] <= blk_row0[:, None]).astype(jnp.int32), axis=1),
                             N_EXPERTS - 1)
    n_used = (pend[-1:] // EXPERT_BLK).astype(jnp.int32)
    e0, e1, r0, r1 = rt[0], rt[1], rt[2], rt[3]

    xs = _dispatch(pstart, e0, e1, r0, r1, hn, cap)
    ys = _experts(blk_expert, n_used, xs, w1[ly].astype(BF16), w3[ly].astype(BF16), w2[ly].astype(BF16))
    out = _combine(pstart, e0, e1, r0, r1, x1, rc, ys)
    return out.reshape(b_sz, s_len, d)
```

```python
import functools
import math

import jax
import jax.numpy as jnp
import numpy as np
from jax import lax
from jax.experimental import pallas as pl
from jax.experimental.pallas import tpu as pltpu

F32 = jnp.float32
BF16 = jnp.bfloat16

D_MODEL = 1024
DA_HEADS = 4
DA_HEAD_DIM = 64
DA_V_DIM = 128
DA_WIDTH = 512
ML_HEADS = 4
ML_HEAD_DIM = 128
ML_WIDTH = 512
CONV_WIDTH = 4
N_GROUPS = 4
EXPERTS_PER_GROUP = 8
N_EXPERTS = 32
EXPERT_FF = 512
NORM_EPS = 1e-6
LAM_INIT = 0.8 - 0.6 * math.exp(-0.3 * 0)

LANES = 128
SUBLANES = 8
VMEM_LIMIT = 56 * 1024 * 1024

SEG_K = 0
SEG_MQK = SEG_K + 1024
SEG_MV = SEG_MQK + 1024
SEG_MO = SEG_MV + 512
SEG_IF = SEG_MO + 512
W_ALL = SEG_IF + LANES

LOG2E = math.log2(math.e)
ALIBI_PIECES = 4
AUX0 = DA_HEAD_DIM
POS_PERIOD = 512
DV_AUG = DA_V_DIM + 16


def _bf16_round(val):
    bits = np.float32(val).view(np.uint32)
    bits = (bits + np.uint32(0x7FFF) + ((bits >> np.uint32(16)) & np.uint32(1))) & np.uint32(0xFFFF0000)
    return float(bits.view(np.float32))


def _alibi_pieces(hd):
    rest, pieces = 2.0 ** (-8.0 * (hd + 1) / DA_HEADS) * LOG2E, []
    for _ in range(ALIBI_PIECES):
        pieces.append(_bf16_round(rest))
        rest -= pieces[-1]
    return pieces

TM_PROJ = 256
TQ_ATTN = 512
ATTN_HEADS = 4
L_MLSTM = 128
TM_MERGE = 256
TM_ROWS = 1024
EXPERT_BLK = 256
ISSUE_UNROLL = 8


def _cparams(sem):
    return pltpu.CompilerParams(dimension_semantics=sem, vmem_limit_bytes=VMEM_LIMIT)


def _lane_iota(shape):
    return lax.broadcasted_iota(jnp.int32, shape, len(shape) - 1)


def _row_iota(shape):
    return lax.broadcasted_iota(jnp.int32, shape, len(shape) - 2)


def _in_proj_kernel(x_ref, g_ref, w_ref, wt_ref, gq_ref, gk_ref, aux_ref,
                    qt_ref, ka_ref, vt_ref, mqk_ref, mv_ref, mo_ref, gif_ref, *, tm):
    x = x_ref[0]
    ms = jnp.mean(x * x, axis=-1, keepdims=True)
    h = ((x * lax.rsqrt(ms + NORM_EPS)) * g_ref[...]).astype(BF16)

    def seg(lo, width):
        return jnp.dot(h, w_ref[:, lo:lo + width], preferred_element_type=F32)

    inv_dh = 1.0 / DA_HEAD_DIM

    t_all = lax.dot_general(wt_ref[...], h, (((1,), (1,)), ((), ())), preferred_element_type=F32)
    for hd in range(DA_HEADS):
        for mp in range(2):
            off = (hd * 2 + mp) * LANES
            q = t_all[off:off + LANES, :]
            qn = (q * lax.rsqrt(jnp.sum(q * q, axis=0, keepdims=True) * inv_dh + NORM_EPS)) * gq_ref[...]
            qn = qn * (DA_HEAD_DIM ** -0.5 * LOG2E) + aux_ref[hd]
            qt_ref[0, hd, mp * LANES:(mp + 1) * LANES, :] = qn.astype(BF16)
        v_lo = 2 * DA_HEADS * LANES + hd * DA_V_DIM
        vt_ref[0, hd, 0:DA_V_DIM, :] = t_all[v_lo:v_lo + DA_V_DIM, :].astype(BF16)
        ones_row = (_row_iota((DV_AUG - DA_V_DIM, tm)) == 0).astype(BF16)
        vt_ref[0, hd, DA_V_DIM:DV_AUG, :] = ones_row

    lane = _lane_iota((tm, LANES))
    pos = pl.program_id(1) * tm + _row_iota((tm, LANES))
    pos_lo = (pos & 255).astype(F32)
    pos_hi = (pos & 256).astype(F32)
    is_aux = (lane >= AUX0) & (lane < AUX0 + 2 * ALIBI_PIECES)
    is_lo = is_aux & ((lane & 1) == 0)
    is_hi = is_aux & ((lane & 1) == 1)
    for hd in range(DA_HEADS):
        for mp in range(2):
            off = (hd * 2 + mp) * LANES
            k = seg(SEG_K + off, LANES)
            kn = (k * lax.rsqrt(jnp.sum(k * k, axis=-1, keepdims=True) * inv_dh + NORM_EPS)) * gk_ref[...]
            kn = jnp.where(is_lo, pos_lo, jnp.where(is_hi, pos_hi, kn))
            ka_ref[0, hd, :, mp * LANES:(mp + 1) * LANES] = kn.astype(BF16)

    mqk_ref[0] = seg(SEG_MQK, 2 * ML_WIDTH)
    mv_ref[0] = seg(SEG_MV, ML_WIDTH).astype(BF16)
    mo_ref[0] = seg(SEG_MO, ML_WIDTH)
    gif_ref[0] = seg(SEG_IF, LANES)


def _in_proj(x, g_attn, w_all, w_t, gq_col, gk):
    b_sz, s_len, d = x.shape
    tm = min(TM_PROJ, s_len)
    grid = (b_sz, s_len // tm)
    tok = lambda width: pl.BlockSpec((1, tm, width), lambda b, s: (b, s, 0))
    full = lambda shape: pl.BlockSpec(shape, lambda b, s: (0,) * len(shape))
    n_t = w_t.shape[0]
    aux = np.zeros((DA_HEADS, LANES, 1), np.float32)
    for hd in range(DA_HEADS):
        aux[hd, AUX0:AUX0 + 2 * ALIBI_PIECES, 0] = np.repeat(_alibi_pieces(hd), 2)
    return pl.pallas_call(
        functools.partial(_in_proj_kernel, tm=tm),
        grid=grid,
        in_specs=[tok(d), full((1, d)), full((d, W_ALL)), full((n_t, d)), full((LANES, 1)), full((1, LANES)),
                  full((DA_HEADS, LANES, 1))],
        out_specs=[pl.BlockSpec((1, DA_HEADS, 2 * LANES, tm), lambda b, s: (b, 0, 0, s)),
                   pl.BlockSpec((1, DA_HEADS, tm, 2 * LANES), lambda b, s: (b, 0, s, 0)),
                   pl.BlockSpec((1, DA_HEADS, DV_AUG, tm), lambda b, s: (b, 0, 0, s)),
                   tok(2 * ML_WIDTH), tok(ML_WIDTH), tok(ML_WIDTH), tok(LANES)],
        out_shape=[
            jax.ShapeDtypeStruct((b_sz, DA_HEADS, 2 * LANES, s_len), BF16),
            jax.ShapeDtypeStruct((b_sz, DA_HEADS, s_len, 2 * LANES), BF16),
            jax.ShapeDtypeStruct((b_sz, DA_HEADS, DV_AUG, s_len), BF16),
            jax.ShapeDtypeStruct((b_sz, s_len, 2 * ML_WIDTH), F32),
            jax.ShapeDtypeStruct((b_sz, s_len, ML_WIDTH), BF16),
            jax.ShapeDtypeStruct((b_sz, s_len, ML_WIDTH), F32),
            jax.ShapeDtypeStruct((b_sz, s_len, LANES), F32),
        ],
        compiler_params=_cparams(("parallel", "parallel")),
        name="in_proj",
    )(x, g_attn, w_all, w_t, gq_col, gk, jnp.asarray(aux))


def _attn_kernel(lam_ref, qt_ref, ka_ref, vt_ref, go_ref, o_ref, acc_ref, *, tq):
    hp = pl.program_id(1)
    qi = pl.program_id(2)
    slopes = [sum(_alibi_pieces(h)) for h in range(DA_HEADS)]
    pair_slopes = []
    for hh in range(ATTN_HEADS):
        sel = slopes[hh]
        for step in range(1, DA_HEADS // ATTN_HEADS):
            sel = jnp.where(hp == step, slopes[step * ATTN_HEADS + hh], sel)
        pair_slopes.append(sel)
    chains = [(hh, mp) for hh in range(ATTN_HEADS) for mp in range(2)]
    qts = {(hh, mp): qt_ref[0, hh, mp * LANES:(mp + 1) * LANES, :] for hh, mp in chains}
    causal = _row_iota((tq, tq)) <= _lane_iota((tq, tq))
    acc_ref[...] = jnp.zeros_like(acc_ref)

    def tile(kv, carry, masked):
        start = pl.multiple_of(kv * tq, tq)
        base = ((jnp.zeros((1, tq), jnp.int32) + (start // POS_PERIOD) * POS_PERIOD) - qi * tq).astype(F32)
        scores = {}
        for hh in range(ATTN_HEADS):
            k = ka_ref[0, hh, pl.ds(start, tq), :]
            for mp in range(2):
                scores[hh, mp] = jnp.dot(k[:, mp * LANES:(mp + 1) * LANES], qts[hh, mp],
                                         preferred_element_type=F32)
        out = {}
        for hh in range(ATTN_HEADS):
            c = base * pair_slopes[hh]
            probs = []
            for mp in range(2):
                s = scores[hh, mp]
                if masked:
                    s = jnp.where(causal, s, -jnp.inf)
                m_old = carry[chains.index((hh, mp))]
                m_new = jnp.maximum(m_old, jnp.max(s, axis=0, keepdims=True) + c)
                probs.append((jnp.exp2(m_old - m_new), jnp.exp2(s - (m_new - c)).astype(BF16)))
                out[hh, mp] = m_new
            vt = vt_ref[0, hh, :, pl.ds(start, tq)]
            for mp in range(2):
                alpha, p = probs[mp]
                acc_ref[hh, mp] = alpha * acc_ref[hh, mp] + jnp.dot(vt, p, preferred_element_type=F32)
        return tuple(out[ch] for ch in chains)

    init = tuple(jnp.full((1, tq), -jnp.inf, F32) for _ in chains)
    carry = lax.fori_loop(0, qi, lambda kv, cr: tile(kv, cr, False), init)
    tile(qi, carry, True)

    lam_v = lam_ref[...]
    lam = (jnp.exp(jnp.sum(lam_v[0:1] * lam_v[1:2], axis=-1, keepdims=True))
           - jnp.exp(jnp.sum(lam_v[2:3] * lam_v[3:4], axis=-1, keepdims=True)) + LAM_INIT)
    for hh in range(ATTN_HEADS):
        l1 = acc_ref[hh, 0, DA_V_DIM:DA_V_DIM + 1, :]
        l2 = acc_ref[hh, 1, DA_V_DIM:DA_V_DIM + 1, :]
        o = acc_ref[hh, 0, 0:DA_V_DIM, :] / l1 - lam * (acc_ref[hh, 1, 0:DA_V_DIM, :] / l2)
        o = (o * lax.rsqrt(jnp.mean(o * o, axis=0, keepdims=True) + NORM_EPS)) * go_ref[...]
        o_ref[0, :, hh * DA_V_DIM:(hh + 1) * DA_V_DIM] = (o * (1.0 - LAM_INIT)).T.astype(BF16)


def _attn(lam_vecs, qt, ka, vt, g_out_col):
    b_sz, _, s_len, _ = ka.shape
    tq = min(TQ_ATTN, s_len)
    grid = (b_sz, DA_HEADS // ATTN_HEADS, s_len // tq)
    return pl.pallas_call(
        functools.partial(_attn_kernel, tq=tq),
        grid=grid,
        in_specs=[
            pl.BlockSpec((4, LANES), lambda b, h, i: (0, 0)),
            pl.BlockSpec((1, ATTN_HEADS, 2 * LANES, tq), lambda b, h, i: (b, h, 0, i)),
            pl.BlockSpec((1, ATTN_HEADS, s_len, 2 * LANES), lambda b, h, i: (b, h, 0, 0)),
            pl.BlockSpec((1, ATTN_HEADS, DV_AUG, s_len), lambda b, h, i: (b, h, 0, 0)),
            pl.BlockSpec((DA_V_DIM, 1), lambda b, h, i: (0, 0)),
        ],
        out_specs=pl.BlockSpec((1, tq, ATTN_HEADS * DA_V_DIM), lambda b, h, i: (b, i, h)),
        out_shape=jax.ShapeDtypeStruct((b_sz, s_len, DA_WIDTH), BF16),
        scratch_shapes=[pltpu.VMEM((ATTN_HEADS, 2, DV_AUG, tq), F32)],
        compiler_params=_cparams(("parallel", "parallel", "arbitrary")),
        name="attn",
    )(lam_vecs, qt, ka, vt, g_out_col)


def _log_sigmoid(x):
    return jnp.minimum(x, 0.0) - jnp.log1p(jnp.exp(-jnp.abs(x)))


def _mlstm_kernel(mqk_ref, mv_ref, mo_ref, gif_ref, cw_ref, cb_ref, gb_ref, go_ref,
                  y_ref, buf_ref, c_ref, n_ref, m_ref, *, L):
    halo = SUBLANES

    @pl.when(pl.program_id(1) == 0)
    def _():
        buf_ref[0:halo, :] = jnp.zeros((halo, 2 * ML_WIDTH), F32)
        c_ref[...] = jnp.zeros_like(c_ref)
        n_ref[...] = jnp.zeros_like(n_ref)
        m_ref[...] = jnp.zeros_like(m_ref)

    buf_ref[halo:halo + L, :] = mqk_ref[0]
    conv = jnp.broadcast_to(cb_ref[...], (L, 2 * ML_WIDTH))
    for j in range(CONV_WIDTH):
        lo = halo - (CONV_WIDTH - 1) + j
        conv = conv + buf_ref[lo:lo + L, :] * cw_ref[j:j + 1, :]
    buf_ref[0:halo, :] = buf_ref[L:L + halo, :]
    qk = conv * jax.nn.sigmoid(conv)

    g = gif_ref[0] + gb_ref[...]
    lf = _log_sigmoid(g)
    g_t = g.T
    lf_t = lf.T

    row = _row_iota((L, L))
    col = _lane_iota((L, L))
    lower = col <= row

    for hd in range(ML_HEADS):
        sl = slice(hd * ML_HEAD_DIM, (hd + 1) * ML_HEAD_DIM)
        q = qk[:, sl] * (ML_HEAD_DIM ** -0.5)
        k = qk[:, ML_WIDTH + hd * ML_HEAD_DIM:ML_WIDTH + (hd + 1) * ML_HEAD_DIM]
        v = mv_ref[0, :, sl]
        i_col = g[:, hd:hd + 1]
        f_col = lf[:, ML_HEADS + hd:ML_HEADS + hd + 1]
        i_row = g_t[hd:hd + 1, :]
        f_row = lf_t[ML_HEADS + hd:ML_HEADS + hd + 1, :]

        b_col = jnp.sum(jnp.where(lower, f_row, 0.0), axis=-1, keepdims=True)
        b_row = jnp.sum(jnp.where(row <= col, f_col, 0.0), axis=0, keepdims=True)
        b_last = b_col[L - 1:L, :]

        m_state = m_ref[hd:hd + 1, 0:1]
        n_state = n_ref[hd:hd + 1, :]
        c_state = c_ref[hd]

        dmat = jnp.where(lower, b_col - b_row + i_row, -jnp.inf)
        r = jnp.max(dmat, axis=-1, keepdims=True)
        p = jnp.exp(dmat - r)
        qb = q.astype(BF16)
        kb = k.astype(BF16)
        sc = lax.dot_general(qb, kb, (((1,), (1,)), ((), ())), preferred_element_type=F32) * p
        num_a = jnp.dot(sc.astype(BF16), v, preferred_element_type=F32)
        den_a = jnp.sum(sc, axis=-1, keepdims=True)

        inter = b_col + m_state
        m_row = jnp.maximum(r, inter)
        e_a = jnp.exp(r - m_row)
        e_b = jnp.exp(inter - m_row)
        num = e_a * num_a + e_b * jnp.dot(qb, c_state.astype(BF16), preferred_element_type=F32)
        den = e_a * den_a + e_b * jnp.sum(q * n_state, axis=-1, keepdims=True)
        hval = num / jnp.maximum(jnp.abs(den), jnp.exp(-m_row))

        w_col = b_last - b_col + i_col
        a = jnp.max(w_col, axis=0, keepdims=True)
        m_new = jnp.maximum(b_last + m_state, a)
        kw = k * jnp.exp(w_col - a)
        d_c = lax.dot_general(kw.astype(BF16), v, (((0,), (0,)), ((), ())), preferred_element_type=F32)
        d_n = jnp.sum(kw, axis=0, keepdims=True)
        decay = jnp.exp(b_last + m_state - m_new)
        gain = jnp.exp(a - m_new)
        c_ref[hd] = decay * c_state + gain * d_c
        n_ref[hd:hd + 1, :] = decay * n_state + gain * d_n
        m_ref[hd:hd + 1, :] = jnp.broadcast_to(m_new, (1, LANES))

        hn = (hval * lax.rsqrt(jnp.mean(hval * hval, axis=-1, keepdims=True) + NORM_EPS)) * go_ref[:, sl]
        y_ref[0, :, sl] = (hn * jax.nn.sigmoid(mo_ref[0, :, sl])).astype(BF16)


def _mlstm(mqk, mv, mo, gif, conv_w, conv_b, gate_b, g_out):
    b_sz, s_len, _ = mqk.shape
    L = min(L_MLSTM, s_len)
    grid = (b_sz, s_len // L)
    tok = lambda width: pl.BlockSpec((1, L, width), lambda b, c: (b, c, 0))
    full = lambda shape: pl.BlockSpec(shape, lambda b, c: (0,) * len(shape))
    return pl.pallas_call(
        functools.partial(_mlstm_kernel, L=L),
        grid=grid,
        in_specs=[tok(2 * ML_WIDTH), tok(ML_WIDTH), tok(ML_WIDTH), tok(LANES),
                  full((CONV_WIDTH, 2 * ML_WIDTH)), full((1, 2 * ML_WIDTH)), full((1, LANES)),
                  full((1, ML_WIDTH))],
        out_specs=tok(ML_WIDTH),
        out_shape=jax.ShapeDtypeStruct((b_sz, s_len, ML_WIDTH), BF16),
        scratch_shapes=[
            pltpu.VMEM((L + SUBLANES, 2 * ML_WIDTH), F32),
            pltpu.VMEM((ML_HEADS, ML_HEAD_DIM, ML_HEAD_DIM), F32),
            pltpu.VMEM((SUBLANES, LANES), F32),
            pltpu.VMEM((SUBLANES, LANES), F32),
        ],
        compiler_params=_cparams(("parallel", "arbitrary")),
        name="mlstm",
    )(mqk, mv, mo, gif, conv_w, conv_b, gate_b, g_out)


def _merge_kernel(x_ref, yda_ref, yml_ref, ga_ref, wg_ref, bg_ref, wda_ref, wml_ref, wo_ref,
                  gf_ref, wr_ref, br_ref,
                  x1_ref, hn_ref, rc_ref, rt_ref, cnt_ref, carry_ref, *, tm):
    @pl.when(pl.program_id(0) == 0)
    def _():
        carry_ref[...] = jnp.zeros_like(carry_ref)

    x = x_ref[...]
    h = ((x * lax.rsqrt(jnp.mean(x * x, axis=-1, keepdims=True) + NORM_EPS)) * ga_ref[...]).astype(BF16)
    gates = jax.nn.sigmoid(jnp.dot(h, wg_ref[...], preferred_element_type=F32) + bg_ref[...])
    a = jnp.dot(yda_ref[...], wda_ref[...], preferred_element_type=F32)
    c = jnp.dot(yml_ref[...], wml_ref[...], preferred_element_type=F32)
    mixed = gates[:, :D_MODEL] * a + gates[:, D_MODEL:] * c
    x1 = x + jnp.dot(mixed.astype(BF16), wo_ref[...], preferred_element_type=F32)
    x1_ref[...] = x1
    hn = (x1 * lax.rsqrt(jnp.mean(x1 * x1, axis=-1, keepdims=True) + NORM_EPS)) * gf_ref[...]
    hn_ref[...] = hn

    logits = jnp.dot(hn.astype(BF16), wr_ref[...], preferred_element_type=F32) + br_ref[...]
    lane = _lane_iota((tm, LANES))
    neg = -jnp.inf
    big = jnp.int32(LANES)

    gl = jnp.where((lane >= N_EXPERTS) & (lane < N_EXPERTS + N_GROUPS), logits, neg)
    gmax = jnp.max(gl, axis=-1, keepdims=True)
    gsum = jnp.sum(jnp.exp(gl - gmax), axis=-1, keepdims=True)
    g_top = 1.0 / gsum
    g_idx = jnp.min(jnp.where(gl == gmax, lane, big), axis=-1, keepdims=True) - N_EXPERTS

    el = jnp.where((lane < N_EXPERTS) & ((lane >> 3) == g_idx), logits, neg)
    emax = jnp.max(el, axis=-1, keepdims=True)
    esum = jnp.sum(jnp.exp(el - emax), axis=-1, keepdims=True)
    e0 = jnp.min(jnp.where(el == emax, lane, big), axis=-1, keepdims=True)
    el2 = jnp.where(lane == e0, neg, el)
    emax2 = jnp.max(el2, axis=-1, keepdims=True)
    e1 = jnp.min(jnp.where(el2 == emax2, lane, big), axis=-1, keepdims=True)
    p0 = 1.0 / esum
    p1 = jnp.exp(emax2 - emax) / esum
    tot = p0 + p1
    w0 = g_top * (p0 / tot)
    w1 = g_top * (p1 / tot)

    hit0 = lane == e0
    hit1 = lane == e1
    onehot = (hit0 | hit1).astype(F32)
    before = (_lane_iota((tm, tm)) < _row_iota((tm, tm))).astype(BF16)
    prefix = jnp.dot(before, onehot.astype(BF16), preferred_element_type=F32) + carry_ref[...]
    r0 = jnp.sum(jnp.where(hit0, prefix, 0.0), axis=-1, keepdims=True)
    r1 = jnp.sum(jnp.where(hit1, prefix, 0.0), axis=-1, keepdims=True)
    carry_ref[...] = carry_ref[...] + jnp.sum(onehot, axis=0, keepdims=True)
    cnt_ref[...] = carry_ref[...]

    rc = jnp.where(lane == 0, e0.astype(F32),
         jnp.where(lane == 1, e1.astype(F32),
         jnp.where(lane == 2, r0,
         jnp.where(lane == 3, r1,
         jnp.where(lane == 4, w0,
         jnp.where(lane == 5, w1, 0.0))))))
    rc_ref[...] = rc
    rt_ref[...] = rc.T[0:SUBLANES, :].astype(jnp.int32)


def _merge(x2, yda, yml, g_attn, w_gate, b_gate, w_da, w_ml, w_out, g_ffn, w_rt, b_rt):
    n_tok, d = x2.shape
    tm = min(TM_MERGE, n_tok)
    grid = (n_tok // tm,)
    tok = lambda width: pl.BlockSpec((tm, width), lambda i: (i, 0))
    full = lambda shape: pl.BlockSpec(shape, lambda i: (0,) * len(shape))
    return pl.pallas_call(
        functools.partial(_merge_kernel, tm=tm),
        grid=grid,
        in_specs=[tok(d), tok(DA_WIDTH), tok(ML_WIDTH), full((1, d)), full((d, 2 * d)), full((1, 2 * d)),
                  full((DA_WIDTH, d)), full((ML_WIDTH, d)), full((d, d)), full((1, d)),
                  full((d, LANES)), full((1, LANES))],
        out_specs=[tok(d), tok(d), tok(LANES), pl.BlockSpec((SUBLANES, tm), lambda i: (0, i)),
                   full((1, LANES))],
        out_shape=[
            jax.ShapeDtypeStruct((n_tok, d), F32),
            jax.ShapeDtypeStruct((n_tok, d), F32),
            jax.ShapeDtypeStruct((n_tok, LANES), F32),
            jax.ShapeDtypeStruct((SUBLANES, n_tok), jnp.int32),
            jax.ShapeDtypeStruct((1, LANES), F32),
        ],
        scratch_shapes=[pltpu.VMEM((1, LANES), F32)],
        compiler_params=_cparams(("arbitrary",)),
        name="merge",
    )(x2, yda, yml, g_attn, w_gate, b_gate, w_da, w_ml, w_out, g_ffn, w_rt, b_rt)


def _load_indices(idx_refs, smem_refs, sem, base, tm):
    copies = [pltpu.make_async_copy(src.at[pl.ds(base, tm)], dst, sem.at[n])
              for n, (src, dst) in enumerate(zip(idx_refs, smem_refs))]
    for cp in copies:
        cp.start()
    for cp in copies:
        cp.wait()


def _dispatch_kernel(pstart_ref, e0_ref, e1_ref, r0_ref, r1_ref, hn_ref, zero_ref, xs_ref,
                     e0_s, e1_s, r0_s, r1_s, isem, sem, *, tm):
    del zero_ref
    base = pl.multiple_of(pl.program_id(0) * tm, tm)
    _load_indices((e0_ref, e1_ref, r0_ref, r1_ref), (e0_s, e1_s, r0_s, r1_s), isem, base, tm)

    def row_copy(t, dest):
        return pltpu.make_async_copy(hn_ref.at[pl.ds(t, 1)], xs_ref.at[pl.ds(dest, 1)], sem)

    def issue(t, _):
        row_copy(t, pstart_ref[e0_s[t]] + r0_s[t]).start(priority=0)
        row_copy(t, pstart_ref[e1_s[t]] + r1_s[t]).start(priority=1)
        return 0

    lax.fori_loop(0, tm, issue, 0, unroll=ISSUE_UNROLL)

    for _ in range(2):
        pltpu.make_async_copy(hn_ref, xs_ref.at[pl.ds(0, tm)], sem).wait()


def _dispatch(pstart, e0, e1, r0, r1, hn, cap):
    n_tok, d = hn.shape
    tm = min(TM_ROWS, n_tok)
    grid = (n_tok // tm,)
    any_spec = pl.BlockSpec(memory_space=pl.ANY)
    zeros = jnp.zeros((cap, d), F32)
    return pl.pallas_call(
        functools.partial(_dispatch_kernel, tm=tm),
        grid_spec=pltpu.PrefetchScalarGridSpec(
            num_scalar_prefetch=1,
            grid=grid,
            in_specs=[any_spec, any_spec, any_spec, any_spec,
                      pl.BlockSpec((tm, d), lambda i, ps: (i, 0)), any_spec],
            out_specs=any_spec,
            scratch_shapes=[pltpu.SMEM((tm,), jnp.int32)] * 4
            + [pltpu.SemaphoreType.DMA((4,)), pltpu.SemaphoreType.DMA],
        ),
        out_shape=jax.ShapeDtypeStruct((cap, d), F32),
        input_output_aliases={6: 0},
        compiler_params=_cparams(("arbitrary",)),
        name="dispatch",
    )(pstart, e0, e1, r0, r1, hn, zeros)


def _experts_kernel(be_ref, nb_ref, xs_ref, w1_ref, w3_ref, w2_ref, y_ref):
    del be_ref

    @pl.when(pl.program_id(0) < nb_ref[0])
    def _():
        xb = xs_ref[...].astype(BF16)
        a = jnp.dot(xb, w1_ref[0], preferred_element_type=F32)
        b = jnp.dot(xb, w3_ref[0], preferred_element_type=F32)
        hid = (a * jax.nn.sigmoid(a)) * b
        y_ref[...] = jnp.dot(hid.astype(BF16), w2_ref[0], preferred_element_type=F32)

    @pl.when(pl.program_id(0) >= nb_ref[0])
    def _():
        y_ref[...] = jnp.zeros_like(y_ref)


def _experts(blk_expert, n_blocks_used, xs, w1, w3, w2):
    cap, d = xs.shape
    n_blocks = cap // EXPERT_BLK
    rows = lambda i, be, nb: (jnp.minimum(i, nb[0] - 1), 0)
    wsel = lambda i, be, nb: (be[jnp.minimum(i, nb[0] - 1)], 0, 0)
    return pl.pallas_call(
        _experts_kernel,
        grid_spec=pltpu.PrefetchScalarGridSpec(
            num_scalar_prefetch=2,
            grid=(n_blocks,),
            in_specs=[pl.BlockSpec((EXPERT_BLK, d), rows),
                      pl.BlockSpec((1, d, EXPERT_FF), wsel),
                      pl.BlockSpec((1, d, EXPERT_FF), wsel),
                      pl.BlockSpec((1, EXPERT_FF, d), wsel)],
            out_specs=pl.BlockSpec((EXPERT_BLK, d), lambda i, be, nb: (i, 0)),
        ),
        out_shape=jax.ShapeDtypeStruct((cap, d), F32),
        compiler_params=_cparams(("arbitrary",)),
        name="experts",
    )(blk_expert, n_blocks_used, xs, w1, w3, w2)


def _combine_kernel(pstart_ref, e0_ref, e1_ref, r0_ref, r1_ref, x1_ref, rc_ref, ys_ref, o_ref,
                    e0_s, e1_s, r0_s, r1_s, ybuf, isem, sem, *, tm):
    base = pl.multiple_of(pl.program_id(0) * tm, tm)
    _load_indices((e0_ref, e1_ref, r0_ref, r1_ref), (e0_s, e1_s, r0_s, r1_s), isem, base, tm)

    def row_copy(slot, t, src):
        return pltpu.make_async_copy(ys_ref.at[pl.ds(src, 1)], ybuf.at[slot, pl.ds(t, 1)], sem)

    def issue(t, _):
        row_copy(0, t, pstart_ref[e0_s[t]] + r0_s[t]).start(priority=0)
        row_copy(1, t, pstart_ref[e1_s[t]] + r1_s[t]).start(priority=1)
        return 0

    lax.fori_loop(0, tm, issue, 0, unroll=ISSUE_UNROLL)

    for slot in range(2):
        pltpu.make_async_copy(ys_ref.at[pl.ds(0, tm)], ybuf.at[slot], sem).wait()

    rc = rc_ref[...]
    o_ref[...] = x1_ref[...] + (rc[:, 4:5] * ybuf[0] + rc[:, 5:6] * ybuf[1])


def _combine(pstart, e0, e1, r0, r1, x1, rc, ys):
    n_tok, d = x1.shape
    tm = min(TM_ROWS, n_tok)
    grid = (n_tok // tm,)
    any_spec = pl.BlockSpec(memory_space=pl.ANY)
    return pl.pallas_call(
        functools.partial(_combine_kernel, tm=tm),
        grid_spec=pltpu.PrefetchScalarGridSpec(
            num_scalar_prefetch=1,
            grid=grid,
            in_specs=[any_spec, any_spec, any_spec, any_spec,
                      pl.BlockSpec((tm, d), lambda i, ps: (i, 0)),
                      pl.BlockSpec((tm, LANES), lambda i, ps: (i, 0)),
                      any_spec],
            out_specs=pl.BlockSpec((tm, d), lambda i, ps: (i, 0)),
            scratch_shapes=[pltpu.SMEM((tm,), jnp.int32)] * 4
            + [pltpu.VMEM((2, tm, d), F32), pltpu.SemaphoreType.DMA((4,)), pltpu.SemaphoreType.DMA],
        ),
        out_shape=jax.ShapeDtypeStruct((n_tok, d), F32),
        compiler_params=_cparams(("arbitrary",)),
        name="combine",
    )(pstart, e0, e1, r0, r1, x1, rc, ys)


def _pad_lanes(vec, width=LANES):
    return jnp.zeros((1, width), F32).at[0, :vec.shape[0]].set(vec.astype(F32))


def _layout_w_in(w_in):
    d = w_in.shape[0]
    qk = w_in[:, :2 * 512].reshape(d, 2, DA_HEADS * 2, DA_HEAD_DIM)
    qk = jnp.concatenate([qk, jnp.zeros_like(qk)], axis=-1).reshape(d, 2, 1024)
    gates = jnp.concatenate([w_in[:, 3584:3592], jnp.zeros((d, LANES - 2 * ML_HEADS), w_in.dtype)], axis=-1)
    w_all = jnp.concatenate([qk[:, 1], w_in[:, 1536:3584], gates], axis=-1).astype(BF16)
    w_t = jnp.concatenate([qk[:, 0], w_in[:, 1024:1536]], axis=-1).T.astype(BF16)
    return w_all, w_t


def kernel(x, attn_norm_g, w_in, da_q_norm_g, da_k_norm_g, da_lambda_q1, da_lambda_k1, da_lambda_q2, da_lambda_k2, da_out_norm_g, ml_conv_w, ml_conv_b, ml_i_bias, ml_f_bias, ml_out_norm_g, w_branch_da, w_branch_ml, w_gate, b_gate, w_out, ffn_norm_g, w_group, b_group, w_router, b_router, w1, w3, w2):
    b_sz, s_len, d = x.shape
    n_tok = b_sz * s_len
    ly = 0

    w_all, w_t = _layout_w_in(w_in[ly])
    g_attn = attn_norm_g[ly].reshape(1, d)
    gq_col = _pad_lanes(da_q_norm_g[ly]).reshape(LANES, 1)
    gk = _pad_lanes(da_k_norm_g[ly])
    lam_vecs = jnp.concatenate([_pad_lanes(v[ly]) for v in
                                (da_lambda_q1, da_lambda_k1, da_lambda_q2, da_lambda_k2)], axis=0)
    gate_b = _pad_lanes(jnp.concatenate([ml_i_bias[ly], ml_f_bias[ly]]))
    w_rt = jnp.concatenate([w_router[ly], w_group[ly],
                            jnp.zeros((d, LANES - N_EXPERTS - N_GROUPS), F32)], axis=-1).astype(BF16)
    b_rt = _pad_lanes(jnp.concatenate([b_router[ly], b_group[ly]]))

    qt, ka, vt, mqk, mv, mo, gif = _in_proj(x, g_attn, w_all, w_t, gq_col, gk)
    y_da = _attn(lam_vecs, qt, ka, vt, da_out_norm_g[ly].reshape(DA_V_DIM, 1))
    y_ml = _mlstm(mqk, mv, mo, gif, ml_conv_w[ly], ml_conv_b[ly].reshape(1, -1), gate_b,
                  ml_out_norm_g[ly].reshape(1, ML_WIDTH))

    x1, hn, rc, rt, counts = _merge(
        x.reshape(n_tok, d), y_da.reshape(n_tok, DA_WIDTH), y_ml.reshape(n_tok, ML_WIDTH), g_attn,
        w_gate[ly].astype(BF16), b_gate[ly].reshape(1, -1), w_branch_da[ly].astype(BF16),
        w_branch_ml[ly].astype(BF16), w_out[ly].astype(BF16), ffn_norm_g[ly].reshape(1, d), w_rt, b_rt)

    cnt = counts[0, :N_EXPERTS].astype(jnp.int32)
    padded = (cnt + EXPERT_BLK - 1) // EXPERT_BLK * EXPERT_BLK
    pend = jnp.cumsum(padded)
    pstart = (pend - padded).astype(jnp.int32)
    cap = 2 * n_tok + N_EXPERTS * EXPERT_BLK
    n_blocks = cap // EXPERT_BLK
    blk_row0 = jnp.arange(n_blocks, dtype=jnp.int32) * EXPERT_BLK
    blk_expert = jnp.minimum(jnp.sum((pend[None, :] <= blk_row0[:, None]).astype(jnp.int32), axis=1),
                             N_EXPERTS - 1)
    n_used = (pend[-1:] // EXPERT_BLK).astype(jnp.int32)
    e0, e1, r0, r1 = rt[0], rt[1], rt[2], rt[3]

    xs = _dispatch(pstart, e0, e1, r0, r1, hn, cap)
    ys = _experts(blk_expert, n_used, xs, w1[ly].astype(BF16), w3[ly].astype(BF16), w2[ly].astype(BF16))
    out = _combine(pstart, e0, e1, r0, r1, x1, rc, ys)
    return out.reshape(b_sz, s_len, d)
```

```python
import functools
import math

import jax
import jax.numpy as jnp
import numpy as np
from jax import lax
from jax.experimental import pallas as pl
from jax.experimental.pallas import tpu as pltpu

F32 = jnp.float32
BF16 = jnp.bfloat16

D_MODEL = 1024
DA_HEADS = 4
DA_HEAD_DIM = 64
DA_V_DIM = 128
DA_WIDTH = 512
ML_HEADS = 4
ML_HEAD_DIM = 128
ML_WIDTH = 512
CONV_WIDTH = 4
N_GROUPS = 4
EXPERTS_PER_GROUP = 8
N_EXPERTS = 32
EXPERT_FF = 512
NORM_EPS = 1e-6
LAM_INIT = 0.8 - 0.6 * math.exp(-0.3 * 0)

LANES = 128
SUBLANES = 8
VMEM_LIMIT = 56 * 1024 * 1024

SEG_K = 0
SEG_MQK = SEG_K + 1024
SEG_MV = SEG_MQK + 1024
SEG_MO = SEG_MV + 512
SEG_IF = SEG_MO + 512
W_ALL = SEG_IF + LANES

LOG2E = math.log2(math.e)
ALIBI_PIECES = 4
AUX0 = DA_HEAD_DIM
POS_PERIOD = 512
DV_AUG = DA_V_DIM + 16


def _bf16_round(val):
    bits = np.float32(val).view(np.uint32)
    bits = (bits + np.uint32(0x7FFF) + ((bits >> np.uint32(16)) & np.uint32(1))) & np.uint32(0xFFFF0000)
    return float(bits.view(np.float32))


def _alibi_pieces(hd):
    rest, pieces = 2.0 ** (-8.0 * (hd + 1) / DA_HEADS) * LOG2E, []
    for _ in range(ALIBI_PIECES):
        pieces.append(_bf16_round(rest))
        rest -= pieces[-1]
    return pieces

TM_PROJ = 512
TQ_ATTN = 512
ATTN_HEADS = 4
L_MLSTM = 256
TM_MERGE = 512
TM_ROWS = 1024
EXPERT_BLK = 256
ISSUE_UNROLL = 8


def _cparams(sem):
    return pltpu.CompilerParams(dimension_semantics=sem, vmem_limit_bytes=VMEM_LIMIT)


def _lane_iota(shape):
    return lax.broadcasted_iota(jnp.int32, shape, len(shape) - 1)


def _row_iota(shape):
    return lax.broadcasted_iota(jnp.int32, shape, len(shape) - 2)


def _in_proj_kernel(x_ref, g_ref, w_ref, wt_ref, gq_ref, gk_ref, aux_ref,
                    qt_ref, ka_ref, vt_ref, mqk_ref, mv_ref, mo_ref, gif_ref, *, tm):
    x = x_ref[0]
    ms = jnp.mean(x * x, axis=-1, keepdims=True)
    h = ((x * lax.rsqrt(ms + NORM_EPS)) * g_ref[...]).astype(BF16)

    def seg(lo, width):
        return jnp.dot(h, w_ref[:, lo:lo + width], preferred_element_type=F32)

    inv_dh = 1.0 / DA_HEAD_DIM

    t_all = lax.dot_general(wt_ref[...], h, (((1,), (1,)), ((), ())), preferred_element_type=F32)
    for hd in range(DA_HEADS):
        for mp in range(2):
            off = (hd * 2 + mp) * LANES
            q = t_all[off:off + LANES, :]
            qn = (q * lax.rsqrt(jnp.sum(q * q, axis=0, keepdims=True) * inv_dh + NORM_EPS)) * gq_ref[...]
            qn = qn * (DA_HEAD_DIM ** -0.5 * LOG2E) + aux_ref[hd]
            qt_ref[0, hd, mp * LANES:(mp + 1) * LANES, :] = qn.astype(BF16)
        v_lo = 2 * DA_HEADS * LANES + hd * DA_V_DIM
        vt_ref[0, hd, 0:DA_V_DIM, :] = t_all[v_lo:v_lo + DA_V_DIM, :].astype(BF16)
        ones_row = (_row_iota((DV_AUG - DA_V_DIM, tm)) == 0).astype(BF16)
        vt_ref[0, hd, DA_V_DIM:DV_AUG, :] = ones_row

    lane = _lane_iota((tm, LANES))
    pos = pl.program_id(1) * tm + _row_iota((tm, LANES))
    pos_lo = (pos & 255).astype(F32)
    pos_hi = (pos & 256).astype(F32)
    is_aux = (lane >= AUX0) & (lane < AUX0 + 2 * ALIBI_PIECES)
    is_lo = is_aux & ((lane & 1) == 0)
    is_hi = is_aux & ((lane & 1) == 1)
    for hd in range(DA_HEADS):
        for mp in range(2):
            off = (hd * 2 + mp) * LANES
            k = seg(SEG_K + off, LANES)
            kn = (k * lax.rsqrt(jnp.sum(k * k, axis=-1, keepdims=True) * inv_dh + NORM_EPS)) * gk_ref[...]
            kn = jnp.where(is_lo, pos_lo, jnp.where(is_hi, pos_hi, kn))
            ka_ref[0, hd, :, mp * LANES:(mp + 1) * LANES] = kn.astype(BF16)

    mqk_ref[0] = seg(SEG_MQK, 2 * ML_WIDTH)
    mv_ref[0] = seg(SEG_MV, ML_WIDTH).astype(BF16)
    mo_ref[0] = seg(SEG_MO, ML_WIDTH)
    gif_ref[0] = seg(SEG_IF, LANES)


def _in_proj(x, g_attn, w_all, w_t, gq_col, gk):
    b_sz, s_len, d = x.shape
    tm = min(TM_PROJ, s_len)
    grid = (b_sz, s_len // tm)
    tok = lambda width: pl.BlockSpec((1, tm, width), lambda b, s: (b, s, 0))
    full = lambda shape: pl.BlockSpec(shape, lambda b, s: (0,) * len(shape))
    n_t = w_t.shape[0]
    aux = np.zeros((DA_HEADS, LANES, 1), np.float32)
    for hd in range(DA_HEADS):
        aux[hd, AUX0:AUX0 + 2 * ALIBI_PIECES, 0] = np.repeat(_alibi_pieces(hd), 2)
    return pl.pallas_call(
        functools.partial(_in_proj_kernel, tm=tm),
        grid=grid,
        in_specs=[tok(d), full((1, d)), full((d, W_ALL)), full((n_t, d)), full((LANES, 1)), full((1, LANES)),
                  full((DA_HEADS, LANES, 1))],
        out_specs=[pl.BlockSpec((1, DA_HEADS, 2 * LANES, tm), lambda b, s: (b, 0, 0, s)),
                   pl.BlockSpec((1, DA_HEADS, tm, 2 * LANES), lambda b, s: (b, 0, s, 0)),
                   pl.BlockSpec((1, DA_HEADS, DV_AUG, tm), lambda b, s: (b, 0, 0, s)),
                   tok(2 * ML_WIDTH), tok(ML_WIDTH), tok(ML_WIDTH), tok(LANES)],
        out_shape=[
            jax.ShapeDtypeStruct((b_sz, DA_HEADS, 2 * LANES, s_len), BF16),
            jax.ShapeDtypeStruct((b_sz, DA_HEADS, s_len, 2 * LANES), BF16),
            jax.ShapeDtypeStruct((b_sz, DA_HEADS, DV_AUG, s_len), BF16),
            jax.ShapeDtypeStruct((b_sz, s_len, 2 * ML_WIDTH), F32),
            jax.ShapeDtypeStruct((b_sz, s_len, ML_WIDTH), BF16),
            jax.ShapeDtypeStruct((b_sz, s_len, ML_WIDTH), F32),
            jax.ShapeDtypeStruct((b_sz, s_len, LANES), F32),
        ],
        compiler_params=_cparams(("parallel", "parallel")),
        name="in_proj",
    )(x, g_attn, w_all, w_t, gq_col, gk, jnp.asarray(aux))


def _attn_kernel(lam_ref, qt_ref, ka_ref, vt_ref, go_ref, o_ref, acc_ref, *, tq):
    hp = pl.program_id(1)
    qi = pl.program_id(2)
    slopes = [sum(_alibi_pieces(h)) for h in range(DA_HEADS)]
    pair_slopes = []
    for hh in range(ATTN_HEADS):
        sel = slopes[hh]
        for step in range(1, DA_HEADS // ATTN_HEADS):
            sel = jnp.where(hp == step, slopes[step * ATTN_HEADS + hh], sel)
        pair_slopes.append(sel)
    chains = [(hh, mp) for hh in range(ATTN_HEADS) for mp in range(2)]
    qts = {(hh, mp): qt_ref[0, hh, mp * LANES:(mp + 1) * LANES, :] for hh, mp in chains}
    causal = _row_iota((tq, tq)) <= _lane_iota((tq, tq))
    acc_ref[...] = jnp.zeros_like(acc_ref)

    def tile(kv, carry, masked):
        start = pl.multiple_of(kv * tq, tq)
        base = ((jnp.zeros((1, tq), jnp.int32) + (start // POS_PERIOD) * POS_PERIOD) - qi * tq).astype(F32)
        scores = {}
        for hh in range(ATTN_HEADS):
            k = ka_ref[0, hh, pl.ds(start, tq), :]
            for mp in range(2):
                scores[hh, mp] = jnp.dot(k[:, mp * LANES:(mp + 1) * LANES], qts[hh, mp],
                                         preferred_element_type=F32)
        out = {}
        for hh in range(ATTN_HEADS):
            c = base * pair_slopes[hh]
            probs = []
            for mp in range(2):
                s = scores[hh, mp]
                if masked:
                    s = jnp.where(causal, s, -jnp.inf)
                m_old = carry[chains.index((hh, mp))]
                m_new = jnp.maximum(m_old, jnp.max(s, axis=0, keepdims=True) + c)
                probs.append((jnp.exp2(m_old - m_new), jnp.exp2(s - (m_new - c)).astype(BF16)))
                out[hh, mp] = m_new
            vt = vt_ref[0, hh, :, pl.ds(start, tq)]
            for mp in range(2):
                alpha, p = probs[mp]
                acc_ref[hh, mp] = alpha * acc_ref[hh, mp] + jnp.dot(vt, p, preferred_element_type=F32)
        return tuple(out[ch] for ch in chains)

    init = tuple(jnp.full((1, tq), -jnp.inf, F32) for _ in chains)
    carry = lax.fori_loop(0, qi, lambda kv, cr: tile(kv, cr, False), init)
    tile(qi, carry, True)

    lam_v = lam_ref[...]
    lam = (jnp.exp(jnp.sum(lam_v[0:1] * lam_v[1:2], axis=-1, keepdims=True))
           - jnp.exp(jnp.sum(lam_v[2:3] * lam_v[3:4], axis=-1, keepdims=True)) + LAM_INIT)
    for hh in range(ATTN_HEADS):
        l1 = acc_ref[hh, 0, DA_V_DIM:DA_V_DIM + 1, :]
        l2 = acc_ref[hh, 1, DA_V_DIM:DA_V_DIM + 1, :]
        o = acc_ref[hh, 0, 0:DA_V_DIM, :] / l1 - lam * (acc_ref[hh, 1, 0:DA_V_DIM, :] / l2)
        o = (o * lax.rsqrt(jnp.mean(o * o, axis=0, keepdims=True) + NORM_EPS)) * go_ref[...]
        o_ref[0, :, hh * DA_V_DIM:(hh + 1) * DA_V_DIM] = (o * (1.0 - LAM_INIT)).T.astype(BF16)


def _attn(lam_vecs, qt, ka, vt, g_out_col):
    b_sz, _, s_len, _ = ka.shape
    tq = min(TQ_ATTN, s_len)
    grid = (b_sz, DA_HEADS // ATTN_HEADS, s_len // tq)
    return pl.pallas_call(
        functools.partial(_attn_kernel, tq=tq),
        grid=grid,
        in_specs=[
            pl.BlockSpec((4, LANES), lambda b, h, i: (0, 0)),
            pl.BlockSpec((1, ATTN_HEADS, 2 * LANES, tq), lambda b, h, i: (b, h, 0, i)),
            pl.BlockSpec((1, ATTN_HEADS, s_len, 2 * LANES), lambda b, h, i: (b, h, 0, 0)),
            pl.BlockSpec((1, ATTN_HEADS, DV_AUG, s_len), lambda b, h, i: (b, h, 0, 0)),
            pl.BlockSpec((DA_V_DIM, 1), lambda b, h, i: (0, 0)),
        ],
        out_specs=pl.BlockSpec((1, tq, ATTN_HEADS * DA_V_DIM), lambda b, h, i: (b, i, h)),
        out_shape=jax.ShapeDtypeStruct((b_sz, s_len, DA_WIDTH), BF16),
        scratch_shapes=[pltpu.VMEM((ATTN_HEADS, 2, DV_AUG, tq), F32)],
        compiler_params=_cparams(("parallel", "parallel", "arbitrary")),
        name="attn",
    )(lam_vecs, qt, ka, vt, g_out_col)


def _log_sigmoid(x):
    return jnp.minimum(x, 0.0) - jnp.log1p(jnp.exp(-jnp.abs(x)))


def _mlstm_kernel(mqk_ref, mv_ref, mo_ref, gif_ref, cw_ref, cb_ref, gb_ref, go_ref,
                  y_ref, buf_ref, c_ref, n_ref, m_ref, *, L):
    halo = SUBLANES

    @pl.when(pl.program_id(1) == 0)
    def _():
        buf_ref[0:halo, :] = jnp.zeros((halo, 2 * ML_WIDTH), F32)
        c_ref[...] = jnp.zeros_like(c_ref)
        n_ref[...] = jnp.zeros_like(n_ref)
        m_ref[...] = jnp.zeros_like(m_ref)

    buf_ref[halo:halo + L, :] = mqk_ref[0]
    conv = jnp.broadcast_to(cb_ref[...], (L, 2 * ML_WIDTH))
    for j in range(CONV_WIDTH):
        lo = halo - (CONV_WIDTH - 1) + j
        conv = conv + buf_ref[lo:lo + L, :] * cw_ref[j:j + 1, :]
    buf_ref[0:halo, :] = buf_ref[L:L + halo, :]
    qk = conv * jax.nn.sigmoid(conv)

    g = gif_ref[0] + gb_ref[...]
    lf = _log_sigmoid(g)
    g_t = g.T
    lf_t = lf.T

    row = _row_iota((L, L))
    col = _lane_iota((L, L))
    lower = col <= row

    for hd in range(ML_HEADS):
        sl = slice(hd * ML_HEAD_DIM, (hd + 1) * ML_HEAD_DIM)
        q = qk[:, sl] * (ML_HEAD_DIM ** -0.5)
        k = qk[:, ML_WIDTH + hd * ML_HEAD_DIM:ML_WIDTH + (hd + 1) * ML_HEAD_DIM]
        v = mv_ref[0, :, sl]
        i_col = g[:, hd:hd + 1]
        f_col = lf[:, ML_HEADS + hd:ML_HEADS + hd + 1]
        i_row = g_t[hd:hd + 1, :]
        f_row = lf_t[ML_HEADS + hd:ML_HEADS + hd + 1, :]

        b_col = jnp.sum(jnp.where(lower, f_row, 0.0), axis=-1, keepdims=True)
        b_row = jnp.sum(jnp.where(row <= col, f_col, 0.0), axis=0, keepdims=True)
        b_last = b_col[L - 1:L, :]

        m_state = m_ref[hd:hd + 1, 0:1]
        n_state = n_ref[hd:hd + 1, :]
        c_state = c_ref[hd]

        dmat = jnp.where(lower, b_col - b_row + i_row, -jnp.inf)
        r = jnp.max(dmat, axis=-1, keepdims=True)
        p = jnp.exp(dmat - r)
        qb = q.astype(BF16)
        kb = k.astype(BF16)
        sc = lax.dot_general(qb, kb, (((1,), (1,)), ((), ())), preferred_element_type=F32) * p
        num_a = jnp.dot(sc.astype(BF16), v, preferred_element_type=F32)
        den_a = jnp.sum(sc, axis=-1, keepdims=True)

        inter = b_col + m_state
        m_row = jnp.maximum(r, inter)
        e_a = jnp.exp(r - m_row)
        e_b = jnp.exp(inter - m_row)
        num = e_a * num_a + e_b * jnp.dot(qb, c_state.astype(BF16), preferred_element_type=F32)
        den = e_a * den_a + e_b * jnp.sum(q * n_state, axis=-1, keepdims=True)
        hval = num / jnp.maximum(jnp.abs(den), jnp.exp(-m_row))

        w_col = b_last - b_col + i_col
        a = jnp.max(w_col, axis=0, keepdims=True)
        m_new = jnp.maximum(b_last + m_state, a)
        kw = k * jnp.exp(w_col - a)
        d_c = lax.dot_general(kw.astype(BF16), v, (((0,), (0,)), ((), ())), preferred_element_type=F32)
        d_n = jnp.sum(kw, axis=0, keepdims=True)
        decay = jnp.exp(b_last + m_state - m_new)
        gain = jnp.exp(a - m_new)
        c_ref[hd] = decay * c_state + gain * d_c
        n_ref[hd:hd + 1, :] = decay * n_state + gain * d_n
        m_ref[hd:hd + 1, :] = jnp.broadcast_to(m_new, (1, LANES))

        hn = (hval * lax.rsqrt(jnp.mean(hval * hval, axis=-1, keepdims=True) + NORM_EPS)) * go_ref[:, sl]
        y_ref[0, :, sl] = (hn * jax.nn.sigmoid(mo_ref[0, :, sl])).astype(BF16)


def _mlstm(mqk, mv, mo, gif, conv_w, conv_b, gate_b, g_out):
    b_sz, s_len, _ = mqk.shape
    L = min(L_MLSTM, s_len)
    grid = (b_sz, s_len // L)
    tok = lambda width: pl.BlockSpec((1, L, width), lambda b, c: (b, c, 0))
    full = lambda shape: pl.BlockSpec(shape, lambda b, c: (0,) * len(shape))
    return pl.pallas_call(
        functools.partial(_mlstm_kernel, L=L),
        grid=grid,
        in_specs=[tok(2 * ML_WIDTH), tok(ML_WIDTH), tok(ML_WIDTH), tok(LANES),
                  full((CONV_WIDTH, 2 * ML_WIDTH)), full((1, 2 * ML_WIDTH)), full((1, LANES)),
                  full((1, ML_WIDTH))],
        out_specs=tok(ML_WIDTH),
        out_shape=jax.ShapeDtypeStruct((b_sz, s_len, ML_WIDTH), BF16),
        scratch_shapes=[
            pltpu.VMEM((L + SUBLANES, 2 * ML_WIDTH), F32),
            pltpu.VMEM((ML_HEADS, ML_HEAD_DIM, ML_HEAD_DIM), F32),
            pltpu.VMEM((SUBLANES, LANES), F32),
            pltpu.VMEM((SUBLANES, LANES), F32),
        ],
        compiler_params=_cparams(("parallel", "arbitrary")),
        name="mlstm",
    )(mqk, mv, mo, gif, conv_w, conv_b, gate_b, g_out)


def _merge_kernel(x_ref, yda_ref, yml_ref, ga_ref, wg_ref, bg_ref, wda_ref, wml_ref, wo_ref,
                  gf_ref, wr_ref, br_ref,
                  x1_ref, hn_ref, rc_ref, rt_ref, cnt_ref, carry_ref, *, tm):
    @pl.when(pl.program_id(0) == 0)
    def _():
        carry_ref[...] = jnp.zeros_like(carry_ref)

    x = x_ref[...]
    h = ((x * lax.rsqrt(jnp.mean(x * x, axis=-1, keepdims=True) + NORM_EPS)) * ga_ref[...]).astype(BF16)
    gates = jax.nn.sigmoid(jnp.dot(h, wg_ref[...], preferred_element_type=F32) + bg_ref[...])
    a = jnp.dot(yda_ref[...], wda_ref[...], preferred_element_type=F32)
    c = jnp.dot(yml_ref[...], wml_ref[...], preferred_element_type=F32)
    mixed = gates[:, :D_MODEL] * a + gates[:, D_MODEL:] * c
    x1 = x + jnp.dot(mixed.astype(BF16), wo_ref[...], preferred_element_type=F32)
    x1_ref[...] = x1
    hn = (x1 * lax.rsqrt(jnp.mean(x1 * x1, axis=-1, keepdims=True) + NORM_EPS)) * gf_ref[...]
    hn_ref[...] = hn

    logits = jnp.dot(hn.astype(BF16), wr_ref[...], preferred_element_type=F32) + br_ref[...]
    lane = _lane_iota((tm, LANES))
    neg = -jnp.inf
    big = jnp.int32(LANES)

    gl = jnp.where((lane >= N_EXPERTS) & (lane < N_EXPERTS + N_GROUPS), logits, neg)
    gmax = jnp.max(gl, axis=-1, keepdims=True)
    gsum = jnp.sum(jnp.exp(gl - gmax), axis=-1, keepdims=True)
    g_top = 1.0 / gsum
    g_idx = jnp.min(jnp.where(gl == gmax, lane, big), axis=-1, keepdims=True) - N_EXPERTS

    el = jnp.where((lane < N_EXPERTS) & ((lane >> 3) == g_idx), logits, neg)
    emax = jnp.max(el, axis=-1, keepdims=True)
    esum = jnp.sum(jnp.exp(el - emax), axis=-1, keepdims=True)
    e0 = jnp.min(jnp.where(el == emax, lane, big), axis=-1, keepdims=True)
    el2 = jnp.where(lane == e0, neg, el)
    emax2 = jnp.max(el2, axis=-1, keepdims=True)
    e1 = jnp.min(jnp.where(el2 == emax2, lane, big), axis=-1, keepdims=True)
    p0 = 1.0 / esum
    p1 = jnp.exp(emax2 - emax) / esum
    tot = p0 + p1
    w0 = g_top * (p0 / tot)
    w1 = g_top * (p1 / tot)

    hit0 = lane == e0
    hit1 = lane == e1
    onehot = (hit0 | hit1).astype(F32)
    before = (_lane_iota((tm, tm)) < _row_iota((tm, tm))).astype(BF16)
    prefix = jnp.dot(before, onehot.astype(BF16), preferred_element_type=F32) + carry_ref[...]
    r0 = jnp.sum(jnp.where(hit0, prefix, 0.0), axis=-1, keepdims=True)
    r1 = jnp.sum(jnp.where(hit1, prefix, 0.0), axis=-1, keepdims=True)
    carry_ref[...] = carry_ref[...] + jnp.sum(onehot, axis=0, keepdims=True)
    cnt_ref[...] = carry_ref[...]

    rc = jnp.where(lane == 0, e0.astype(F32),
         jnp.where(lane == 1, e1.astype(F32),
         jnp.where(lane == 2, r0,
         jnp.where(lane == 3, r1,
         jnp.where(lane == 4, w0,
         jnp.where(lane == 5, w1, 0.0))))))
    rc_ref[...] = rc
    rt_ref[...] = rc.T[0:SUBLANES, :].astype(jnp.int32)


def _merge(x2, yda, yml, g_attn, w_gate, b_gate, w_da, w_ml, w_out, g_ffn, w_rt, b_rt):
    n_tok, d = x2.shape
    tm = min(TM_MERGE, n_tok)
    grid = (n_tok // tm,)
    tok = lambda width: pl.BlockSpec((tm, width), lambda i: (i, 0))
    full = lambda shape: pl.BlockSpec(shape, lambda i: (0,) * len(shape))
    return pl.pallas_call(
        functools.partial(_merge_kernel, tm=tm),
        grid=grid,
        in_specs=[tok(d), tok(DA_WIDTH), tok(ML_WIDTH), full((1, d)), full((d, 2 * d)), full((1, 2 * d)),
                  full((DA_WIDTH, d)), full((ML_WIDTH, d)), full((d, d)), full((1, d)),
                  full((d, LANES)), full((1, LANES))],
        out_specs=[tok(d), tok(d), tok(LANES), pl.BlockSpec((SUBLANES, tm), lambda i: (0, i)),
                   full((1, LANES))],
        out_shape=[
            jax.ShapeDtypeStruct((n_tok, d), F32),
            jax.ShapeDtypeStruct((n_tok, d), F32),
            jax.ShapeDtypeStruct((n_tok, LANES), F32),
            jax.ShapeDtypeStruct((SUBLANES, n_tok), jnp.int32),
            jax.ShapeDtypeStruct((1, LANES), F32),
        ],
        scratch_shapes=[pltpu.VMEM((1, LANES), F32)],
        compiler_params=_cparams(("arbitrary",)),
        name="merge",
    )(x2, yda, yml, g_attn, w_gate, b_gate, w_da, w_ml, w_out, g_ffn, w_rt, b_rt)


def _load_indices(idx_refs, smem_refs, sem, base, tm):
    copies = [pltpu.make_async_copy(src.at[pl.ds(base, tm)], dst, sem.at[n])
              for n, (src, dst) in enumerate(zip(idx_refs, smem_refs))]
    for cp in copies:
        cp.start()
    for cp in copies:
        cp.wait()


def _dispatch_kernel(pstart_ref, e0_ref, e1_ref, r0_ref, r1_ref, hn_ref, xs_ref,
                     e0_s, e1_s, r0_s, r1_s, zblk, isem, sem, zsem, *, tm, n_blocks):

    @pl.when(pl.program_id(0) == 0)
    def _():
        zblk[...] = jnp.zeros_like(zblk)

        def zero_copy(row0):
            return pltpu.make_async_copy(zblk, xs_ref.at[pl.ds(pl.multiple_of(row0, EXPERT_BLK), EXPERT_BLK)], zsem)

        def last_block(e, n):
            end = pstart_ref[N_EXPERTS + e]
            has = end > pstart_ref[e]

            @pl.when(has)
            def _():
                zero_copy(end - EXPERT_BLK).start()

            return n + has.astype(jnp.int32)

        def tail_block(j, n):
            zero_copy(j * EXPERT_BLK).start()
            return n + 1

        n = lax.fori_loop(0, N_EXPERTS, last_block, 0)
        n = lax.fori_loop(pstart_ref[2 * N_EXPERTS], n_blocks, tail_block, n)

        def drain(_, c):
            zero_copy(0).wait()
            return c

        lax.fori_loop(0, n, drain, 0)

    base = pl.multiple_of(pl.program_id(0) * tm, tm)
    _load_indices((e0_ref, e1_ref, r0_ref, r1_ref), (e0_s, e1_s, r0_s, r1_s), isem, base, tm)

    def row_copy(t, dest):
        return pltpu.make_async_copy(hn_ref.at[pl.ds(t, 1)], xs_ref.at[pl.ds(dest, 1)], sem)

    def issue(t, _):
        row_copy(t, pstart_ref[e0_s[t]] + r0_s[t]).start(priority=0)
        row_copy(t, pstart_ref[e1_s[t]] + r1_s[t]).start(priority=1)
        return 0

    lax.fori_loop(0, tm, issue, 0, unroll=ISSUE_UNROLL)

    for _ in range(2):
        pltpu.make_async_copy(hn_ref, xs_ref.at[pl.ds(0, tm)], sem).wait()


def _dispatch(pstart, e0, e1, r0, r1, hn, cap):
    n_tok, d = hn.shape
    tm = min(TM_ROWS, n_tok)
    grid = (n_tok // tm,)
    any_spec = pl.BlockSpec(memory_space=pl.ANY)
    return pl.pallas_call(
        functools.partial(_dispatch_kernel, tm=tm, n_blocks=cap // EXPERT_BLK),
        grid_spec=pltpu.PrefetchScalarGridSpec(
            num_scalar_prefetch=1,
            grid=grid,
            in_specs=[any_spec, any_spec, any_spec, any_spec,
                      pl.BlockSpec((tm, d), lambda i, ps: (i, 0))],
            out_specs=any_spec,
            scratch_shapes=[pltpu.SMEM((tm,), jnp.int32)] * 4
            + [pltpu.VMEM((EXPERT_BLK, d), F32), pltpu.SemaphoreType.DMA((4,)), pltpu.SemaphoreType.DMA,
               pltpu.SemaphoreType.DMA],
        ),
        out_shape=jax.ShapeDtypeStruct((cap, d), F32),
        compiler_params=_cparams(("arbitrary",)),
        name="dispatch",
    )(pstart, e0, e1, r0, r1, hn)


def _experts_kernel(be_ref, nb_ref, xs_ref, w1_ref, w3_ref, w2_ref, y_ref, w1b, w3b, w2b):
    i = pl.program_id(0)
    used = i < nb_ref[0]

    @pl.when(used & ((i == 0) | (be_ref[i] != be_ref[jnp.maximum(i - 1, 0)])))
    def _():
        w1b[...] = w1_ref[0].astype(BF16)
        w3b[...] = w3_ref[0].astype(BF16)
        w2b[...] = w2_ref[0].astype(BF16)

    @pl.when(used)
    def _():
        xb = xs_ref[...].astype(BF16)
        a = jnp.dot(xb, w1b[...], preferred_element_type=F32)
        b = jnp.dot(xb, w3b[...], preferred_element_type=F32)
        hid = (a * jax.nn.sigmoid(a)) * b
        y_ref[...] = jnp.dot(hid.astype(BF16), w2b[...], preferred_element_type=F32)

    @pl.when(jnp.logical_not(used))
    def _():
        y_ref[...] = jnp.zeros_like(y_ref)


def _experts(blk_expert, n_blocks_used, xs, w1, w3, w2):
    cap, d = xs.shape
    n_blocks = cap // EXPERT_BLK
    rows = lambda i, be, nb: (jnp.minimum(i, nb[0] - 1), 0)
    wsel = lambda i, be, nb: (be[jnp.minimum(i, nb[0] - 1)], 0, 0)
    return pl.pallas_call(
        _experts_kernel,
        grid_spec=pltpu.PrefetchScalarGridSpec(
            num_scalar_prefetch=2,
            grid=(n_blocks,),
            in_specs=[pl.BlockSpec((EXPERT_BLK, d), rows),
                      pl.BlockSpec((1, d, EXPERT_FF), wsel),
                      pl.BlockSpec((1, d, EXPERT_FF), wsel),
                      pl.BlockSpec((1, EXPERT_FF, d), wsel)],
            out_specs=pl.BlockSpec((EXPERT_BLK, d), lambda i, be, nb: (i, 0)),
            scratch_shapes=[pltpu.VMEM((d, EXPERT_FF), BF16), pltpu.VMEM((d, EXPERT_FF), BF16),
                            pltpu.VMEM((EXPERT_FF, d), BF16)],
        ),
        out_shape=jax.ShapeDtypeStruct((cap, d), F32),
        compiler_params=_cparams(("arbitrary",)),
        name="experts",
    )(blk_expert, n_blocks_used, xs, w1, w3, w2)


def _combine_kernel(pstart_ref, e0_ref, e1_ref, r0_ref, r1_ref, x1_ref, rc_ref, ys_ref, o_ref,
                    e0_s, e1_s, r0_s, r1_s, ybuf, isem, sem, *, tm):
    base = pl.multiple_of(pl.program_id(0) * tm, tm)
    _load_indices((e0_ref, e1_ref, r0_ref, r1_ref), (e0_s, e1_s, r0_s, r1_s), isem, base, tm)

    def row_copy(slot, t, src):
        return pltpu.make_async_copy(ys_ref.at[pl.ds(src, 1)], ybuf.at[slot, pl.ds(t, 1)], sem)

    def issue(t, _):
        row_copy(0, t, pstart_ref[e0_s[t]] + r0_s[t]).start(priority=0)
        row_copy(1, t, pstart_ref[e1_s[t]] + r1_s[t]).start(priority=1)
        return 0

    lax.fori_loop(0, tm, issue, 0, unroll=ISSUE_UNROLL)

    for slot in range(2):
        pltpu.make_async_copy(ys_ref.at[pl.ds(0, tm)], ybuf.at[slot], sem).wait()

    rc = rc_ref[...]
    o_ref[...] = x1_ref[...] + (rc[:, 4:5] * ybuf[0] + rc[:, 5:6] * ybuf[1])


def _combine(pstart, e0, e1, r0, r1, x1, rc, ys):
    n_tok, d = x1.shape
    tm = min(TM_ROWS, n_tok)
    grid = (n_tok // tm,)
    any_spec = pl.BlockSpec(memory_space=pl.ANY)
    return pl.pallas_call(
        functools.partial(_combine_kernel, tm=tm),
        grid_spec=pltpu.PrefetchScalarGridSpec(
            num_scalar_prefetch=1,
            grid=grid,
            in_specs=[any_spec, any_spec, any_spec, any_spec,
                      pl.BlockSpec((tm, d), lambda i, ps: (i, 0)),
                      pl.BlockSpec((tm, LANES), lambda i, ps: (i, 0)),
                      any_spec],
            out_specs=pl.BlockSpec((tm, d), lambda i, ps: (i, 0)),
            scratch_shapes=[pltpu.SMEM((tm,), jnp.int32)] * 4
            + [pltpu.VMEM((2, tm, d), F32), pltpu.SemaphoreType.DMA((4,)), pltpu.SemaphoreType.DMA],
        ),
        out_shape=jax.ShapeDtypeStruct((n_tok, d), F32),
        compiler_params=_cparams(("arbitrary",)),
        name="combine",
    )(pstart, e0, e1, r0, r1, x1, rc, ys)


def _pad_lanes(vec, width=LANES):
    return jnp.zeros((1, width), F32).at[0, :vec.shape[0]].set(vec.astype(F32))


def _layout_w_in(w_in):
    d = w_in.shape[0]
    qk = w_in[:, :2 * 512].reshape(d, 2, DA_HEADS * 2, DA_HEAD_DIM)
    qk = jnp.concatenate([qk, jnp.zeros_like(qk)], axis=-1).reshape(d, 2, 1024)
    gates = jnp.concatenate([w_in[:, 3584:3592], jnp.zeros((d, LANES - 2 * ML_HEADS), w_in.dtype)], axis=-1)
    w_all = jnp.concatenate([qk[:, 1], w_in[:, 1536:3584], gates], axis=-1).astype(BF16)
    w_t = jnp.concatenate([qk[:, 0], w_in[:, 1024:1536]], axis=-1).T.astype(BF16)
    return w_all, w_t


def kernel(x, attn_norm_g, w_in, da_q_norm_g, da_k_norm_g, da_lambda_q1, da_lambda_k1, da_lambda_q2, da_lambda_k2, da_out_norm_g, ml_conv_w, ml_conv_b, ml_i_bias, ml_f_bias, ml_out_norm_g, w_branch_da, w_branch_ml, w_gate, b_gate, w_out, ffn_norm_g, w_group, b_group, w_router, b_router, w1, w3, w2):
    b_sz, s_len, d = x.shape
    n_tok = b_sz * s_len
    ly = 0

    w_all, w_t = _layout_w_in(w_in[ly])
    g_attn = attn_norm_g[ly].reshape(1, d)
    gq_col = _pad_lanes(da_q_norm_g[ly]).reshape(LANES, 1)
    gk = _pad_lanes(da_k_norm_g[ly])
    lam_vecs = jnp.concatenate([_pad_lanes(v[ly]) for v in
                                (da_lambda_q1, da_lambda_k1, da_lambda_q2, da_lambda_k2)], axis=0)
    gate_b = _pad_lanes(jnp.concatenate([ml_i_bias[ly], ml_f_bias[ly]]))
    w_rt = jnp.concatenate([w_router[ly], w_group[ly],
                            jnp.zeros((d, LANES - N_EXPERTS - N_GROUPS), F32)], axis=-1).astype(BF16)
    b_rt = _pad_lanes(jnp.concatenate([b_router[ly], b_group[ly]]))

    qt, ka, vt, mqk, mv, mo, gif = _in_proj(x, g_attn, w_all, w_t, gq_col, gk)
    y_da = _attn(lam_vecs, qt, ka, vt, da_out_norm_g[ly].reshape(DA_V_DIM, 1))
    y_ml = _mlstm(mqk, mv, mo, gif, ml_conv_w[ly], ml_conv_b[ly].reshape(1, -1), gate_b,
                  ml_out_norm_g[ly].reshape(1, ML_WIDTH))

    x1, hn, rc, rt, counts = _merge(
        x.reshape(n_tok, d), y_da.reshape(n_tok, DA_WIDTH), y_ml.reshape(n_tok, ML_WIDTH), g_attn,
        w_gate[ly].astype(BF16), b_gate[ly].reshape(1, -1), w_branch_da[ly].astype(BF16),
        w_branch_ml[ly].astype(BF16), w_out[ly].astype(BF16), ffn_norm_g[ly].reshape(1, d), w_rt, b_rt)

    cnt = counts[0, :N_EXPERTS].astype(jnp.int32)
    padded = (cnt + EXPERT_BLK - 1) // EXPERT_BLK * EXPERT_BLK
    pend = jnp.cumsum(padded)
    pstart = (pend - padded).astype(jnp.int32)
    cap = 2 * n_tok + N_EXPERTS * EXPERT_BLK
    n_blocks = cap // EXPERT_BLK
    blk_row0 = jnp.arange(n_blocks, dtype=jnp.int32) * EXPERT_BLK
    blk_expert = jnp.minimum(jnp.sum((pend[None, :] <= blk_row0[:, None]).astype(jnp.int32), axis=1),
                             N_EXPERTS - 1)
    n_used = (pend[-1:] // EXPERT_BLK).astype(jnp.int32)
    layout = jnp.concatenate([pstart, pend.astype(jnp.int32), n_used])
    e0, e1, r0, r1 = rt[0], rt[1], rt[2], rt[3]

    xs = _dispatch(layout, e0, e1, r0, r1, hn, cap)
    ys = _experts(blk_expert, n_used, xs, w1[ly], w3[ly], w2[ly])
    out = _combine(layout, e0, e1, r0, r1, x1, rc, ys)
    return out.reshape(b_sz, s_len, d)
```

```python
import functools
import math

import jax
import jax.numpy as jnp
import numpy as np
from jax import lax
from jax.experimental import pallas as pl
from jax.experimental.pallas import tpu as pltpu

F32 = jnp.float32
BF16 = jnp.bfloat16

D_MODEL = 1024
DA_HEADS = 4
DA_HEAD_DIM = 64
DA_V_DIM = 128
DA_WIDTH = 512
ML_HEADS = 4
ML_HEAD_DIM = 128
ML_WIDTH = 512
CONV_WIDTH = 4
N_GROUPS = 4
EXPERTS_PER_GROUP = 8
N_EXPERTS = 32
EXPERT_FF = 512
NORM_EPS = 1e-6
LAM_INIT = 0.8 - 0.6 * math.exp(-0.3 * 0)

LANES = 128
SUBLANES = 8
VMEM_LIMIT = 56 * 1024 * 1024

SEG_K = 0
SEG_MQK = SEG_K + 512
SEG_MV = SEG_MQK + 1024
SEG_MO = SEG_MV + 512
SEG_IF = SEG_MO + 512
W_ALL = SEG_IF + LANES

LOG2E = math.log2(math.e)
ALIBI_PIECES = 4
AUX0 = DA_HEAD_DIM
POS_PERIOD = 512
DV_AUG = DA_V_DIM + 16


def _bf16_round(val):
    bits = np.float32(val).view(np.uint32)
    bits = (bits + np.uint32(0x7FFF) + ((bits >> np.uint32(16)) & np.uint32(1))) & np.uint32(0xFFFF0000)
    return float(bits.view(np.float32))


def _alibi_pieces(hd):
    rest, pieces = 2.0 ** (-8.0 * (hd + 1) / DA_HEADS) * LOG2E, []
    for _ in range(ALIBI_PIECES):
        pieces.append(_bf16_round(rest))
        rest -= pieces[-1]
    return pieces

TM_PROJ = 512
TQ_ATTN = 512
ATTN_HEADS = 4
L_MLSTM = 256
TM_MERGE = 512
TM_ROWS = 1024
EXPERT_BLK = 256
ISSUE_UNROLL = 8


def _cparams(sem):
    return pltpu.CompilerParams(dimension_semantics=sem, vmem_limit_bytes=VMEM_LIMIT)


def _lane_iota(shape):
    return lax.broadcasted_iota(jnp.int32, shape, len(shape) - 1)


def _row_iota(shape):
    return lax.broadcasted_iota(jnp.int32, shape, len(shape) - 2)


def _in_proj_kernel(x_ref, g_ref, w_ref, wt_ref, gq_ref, gk_ref, aux_ref,
                    qt_ref, ka_ref, vt_ref, mqk_ref, mv_ref, mo_ref, gif_ref, *, tm):
    x = x_ref[0]
    ms = jnp.mean(x * x, axis=-1, keepdims=True)
    h = ((x * lax.rsqrt(ms + NORM_EPS)) * g_ref[...]).astype(BF16)

    def seg(lo, width):
        return jnp.dot(h, w_ref[:, lo:lo + width], preferred_element_type=F32)

    inv_dh = 1.0 / DA_HEAD_DIM

    t_all = lax.dot_general(wt_ref[...], h, (((1,), (1,)), ((), ())), preferred_element_type=F32)
    dh = DA_HEAD_DIM
    for hd in range(DA_HEADS):
        aux_rows = jnp.broadcast_to(aux_ref[hd, dh:LANES, :], (LANES - dh, tm)).astype(BF16)
        for mp in range(2):
            off = (hd * 2 + mp) * dh
            q = t_all[off:off + dh, :]
            qn = (q * lax.rsqrt(jnp.sum(q * q, axis=0, keepdims=True) * inv_dh + NORM_EPS)) * gq_ref[0:dh, :]
            qt_ref[0, hd, mp * LANES:mp * LANES + dh, :] = (qn * (dh ** -0.5 * LOG2E)).astype(BF16)
            qt_ref[0, hd, mp * LANES + dh:(mp + 1) * LANES, :] = aux_rows
        v_lo = DA_HEADS * 2 * dh + hd * DA_V_DIM
        vt_ref[0, hd, 0:DA_V_DIM, :] = t_all[v_lo:v_lo + DA_V_DIM, :].astype(BF16)
        ones_row = (_row_iota((DV_AUG - DA_V_DIM, tm)) == 0).astype(BF16)
        vt_ref[0, hd, DA_V_DIM:DV_AUG, :] = ones_row

    lane = _lane_iota((tm, LANES))
    pos = pl.program_id(1) * tm + _row_iota((tm, LANES))
    pos_lo = (pos & 255).astype(F32)
    pos_hi = (pos & 256).astype(F32)
    is_aux = (lane >= AUX0) & (lane < AUX0 + 2 * ALIBI_PIECES)
    is_lo = is_aux & ((lane & 1) == 0)
    is_hi = is_aux & ((lane & 1) == 1)
    real = lane < dh
    k_all = seg(SEG_K, DA_HEADS * 2 * dh)
    for hd in range(DA_HEADS):
        k_pair = k_all[:, hd * LANES:(hd + 1) * LANES]
        for mp in range(2):
            k = jnp.where(real, k_pair if mp == 0 else pltpu.roll(k_pair, dh, axis=1), 0.0)
            kn = (k * lax.rsqrt(jnp.sum(k * k, axis=-1, keepdims=True) * inv_dh + NORM_EPS)) * gk_ref[...]
            kn = jnp.where(is_lo, pos_lo, jnp.where(is_hi, pos_hi, kn))
            ka_ref[0, hd, :, mp * LANES:(mp + 1) * LANES] = kn.astype(BF16)

    mqk_ref[0] = seg(SEG_MQK, 2 * ML_WIDTH)
    mv_ref[0] = seg(SEG_MV, ML_WIDTH).astype(BF16)
    mo_ref[0] = seg(SEG_MO, ML_WIDTH)
    gif_ref[0] = seg(SEG_IF, LANES)


def _in_proj(x, g_attn, w_all, w_t, gq_col, gk):
    b_sz, s_len, d = x.shape
    tm = min(TM_PROJ, s_len)
    grid = (b_sz, s_len // tm)
    tok = lambda width: pl.BlockSpec((1, tm, width), lambda b, s: (b, s, 0))
    full = lambda shape: pl.BlockSpec(shape, lambda b, s: (0,) * len(shape))
    n_t = w_t.shape[0]
    aux = np.zeros((DA_HEADS, LANES, 1), np.float32)
    for hd in range(DA_HEADS):
        aux[hd, AUX0:AUX0 + 2 * ALIBI_PIECES, 0] = np.repeat(_alibi_pieces(hd), 2)
    return pl.pallas_call(
        functools.partial(_in_proj_kernel, tm=tm),
        grid=grid,
        in_specs=[tok(d), full((1, d)), full((d, W_ALL)), full((n_t, d)), full((LANES, 1)), full((1, LANES)),
                  full((DA_HEADS, LANES, 1))],
        out_specs=[pl.BlockSpec((1, DA_HEADS, 2 * LANES, tm), lambda b, s: (b, 0, 0, s)),
                   pl.BlockSpec((1, DA_HEADS, tm, 2 * LANES), lambda b, s: (b, 0, s, 0)),
                   pl.BlockSpec((1, DA_HEADS, DV_AUG, tm), lambda b, s: (b, 0, 0, s)),
                   tok(2 * ML_WIDTH), tok(ML_WIDTH), tok(ML_WIDTH), tok(LANES)],
        out_shape=[
            jax.ShapeDtypeStruct((b_sz, DA_HEADS, 2 * LANES, s_len), BF16),
            jax.ShapeDtypeStruct((b_sz, DA_HEADS, s_len, 2 * LANES), BF16),
            jax.ShapeDtypeStruct((b_sz, DA_HEADS, DV_AUG, s_len), BF16),
            jax.ShapeDtypeStruct((b_sz, s_len, 2 * ML_WIDTH), F32),
            jax.ShapeDtypeStruct((b_sz, s_len, ML_WIDTH), BF16),
            jax.ShapeDtypeStruct((b_sz, s_len, ML_WIDTH), F32),
            jax.ShapeDtypeStruct((b_sz, s_len, LANES), F32),
        ],
        compiler_params=_cparams(("parallel", "parallel")),
        name="in_proj",
    )(x, g_attn, w_all, w_t, gq_col, gk, jnp.asarray(aux))


def _attn_kernel(lam_ref, qt_ref, ka_ref, vt_ref, go_ref, o_ref, acc_ref, *, tq):
    hp = pl.program_id(1)
    qi = pl.program_id(2)
    slopes = [sum(_alibi_pieces(h)) for h in range(DA_HEADS)]
    pair_slopes = []
    for hh in range(ATTN_HEADS):
        sel = slopes[hh]
        for step in range(1, DA_HEADS // ATTN_HEADS):
            sel = jnp.where(hp == step, slopes[step * ATTN_HEADS + hh], sel)
        pair_slopes.append(sel)
    chains = [(hh, mp) for hh in range(ATTN_HEADS) for mp in range(2)]
    qts = {(hh, mp): qt_ref[0, hh, mp * LANES:(mp + 1) * LANES, :] for hh, mp in chains}
    causal = _row_iota((tq, tq)) <= _lane_iota((tq, tq))
    acc_ref[...] = jnp.zeros_like(acc_ref)

    def tile(kv, carry, masked):
        start = pl.multiple_of(kv * tq, tq)
        base = ((jnp.zeros((1, tq), jnp.int32) + (start // POS_PERIOD) * POS_PERIOD) - qi * tq).astype(F32)
        scores = {}
        for hh in range(ATTN_HEADS):
            k = ka_ref[0, hh, pl.ds(start, tq), :]
            for mp in range(2):
                scores[hh, mp] = jnp.dot(k[:, mp * LANES:(mp + 1) * LANES], qts[hh, mp],
                                         preferred_element_type=F32)
        out = {}
        for hh in range(ATTN_HEADS):
            c = base * pair_slopes[hh]
            probs = []
            for mp in range(2):
                s = scores[hh, mp]
                if masked:
                    s = jnp.where(causal, s, -jnp.inf)
                m_old = carry[chains.index((hh, mp))]
                m_new = jnp.maximum(m_old, jnp.max(s, axis=0, keepdims=True) + c)
                probs.append((jnp.exp2(m_old - m_new), jnp.exp2(s - (m_new - c)).astype(BF16)))
                out[hh, mp] = m_new
            vt = vt_ref[0, hh, :, pl.ds(start, tq)]
            for mp in range(2):
                alpha, p = probs[mp]
                acc_ref[hh, mp] = alpha * acc_ref[hh, mp] + jnp.dot(vt, p, preferred_element_type=F32)
        return tuple(out[ch] for ch in chains)

    init = tuple(jnp.full((1, tq), -jnp.inf, F32) for _ in chains)
    carry = lax.fori_loop(0, qi, lambda kv, cr: tile(kv, cr, False), init)
    tile(qi, carry, True)

    lam_v = lam_ref[...]
    lam = (jnp.exp(jnp.sum(lam_v[0:1] * lam_v[1:2], axis=-1, keepdims=True))
           - jnp.exp(jnp.sum(lam_v[2:3] * lam_v[3:4], axis=-1, keepdims=True)) + LAM_INIT)
    for hh in range(ATTN_HEADS):
        l1 = acc_ref[hh, 0, DA_V_DIM:DA_V_DIM + 1, :]
        l2 = acc_ref[hh, 1, DA_V_DIM:DA_V_DIM + 1, :]
        o = acc_ref[hh, 0, 0:DA_V_DIM, :] / l1 - lam * (acc_ref[hh, 1, 0:DA_V_DIM, :] / l2)
        o = (o * lax.rsqrt(jnp.mean(o * o, axis=0, keepdims=True) + NORM_EPS)) * go_ref[...]
        o_ref[0, :, hh * DA_V_DIM:(hh + 1) * DA_V_DIM] = (o * (1.0 - LAM_INIT)).T.astype(BF16)


def _attn(lam_vecs, qt, ka, vt, g_out_col):
    b_sz, _, s_len, _ = ka.shape
    tq = min(TQ_ATTN, s_len)
    grid = (b_sz, DA_HEADS // ATTN_HEADS, s_len // tq)
    return pl.pallas_call(
        functools.partial(_attn_kernel, tq=tq),
        grid=grid,
        in_specs=[
            pl.BlockSpec((4, LANES), lambda b, h, i: (0, 0)),
            pl.BlockSpec((1, ATTN_HEADS, 2 * LANES, tq), lambda b, h, i: (b, h, 0, i)),
            pl.BlockSpec((1, ATTN_HEADS, s_len, 2 * LANES), lambda b, h, i: (b, h, 0, 0)),
            pl.BlockSpec((1, ATTN_HEADS, DV_AUG, s_len), lambda b, h, i: (b, h, 0, 0)),
            pl.BlockSpec((DA_V_DIM, 1), lambda b, h, i: (0, 0)),
        ],
        out_specs=pl.BlockSpec((1, tq, ATTN_HEADS * DA_V_DIM), lambda b, h, i: (b, i, h)),
        out_shape=jax.ShapeDtypeStruct((b_sz, s_len, DA_WIDTH), BF16),
        scratch_shapes=[pltpu.VMEM((ATTN_HEADS, 2, DV_AUG, tq), F32)],
        compiler_params=_cparams(("parallel", "parallel", "arbitrary")),
        name="attn",
    )(lam_vecs, qt, ka, vt, g_out_col)


def _log_sigmoid(x):
    return jnp.minimum(x, 0.0) - jnp.log1p(jnp.exp(-jnp.abs(x)))


def _mlstm_kernel(mqk_ref, mv_ref, mo_ref, gif_ref, cw_ref, cb_ref, gb_ref, go_ref,
                  y_ref, buf_ref, c_ref, n_ref, m_ref, *, L):
    halo = SUBLANES

    @pl.when(pl.program_id(1) == 0)
    def _():
        buf_ref[0:halo, :] = jnp.zeros((halo, 2 * ML_WIDTH), F32)
        c_ref[...] = jnp.zeros_like(c_ref)
        n_ref[...] = jnp.zeros_like(n_ref)
        m_ref[...] = jnp.zeros_like(m_ref)

    buf_ref[halo:halo + L, :] = mqk_ref[0]
    conv = jnp.broadcast_to(cb_ref[...], (L, 2 * ML_WIDTH))
    for j in range(CONV_WIDTH):
        lo = halo - (CONV_WIDTH - 1) + j
        conv = conv + buf_ref[lo:lo + L, :] * cw_ref[j:j + 1, :]
    buf_ref[0:halo, :] = buf_ref[L:L + halo, :]
    qk = conv * jax.nn.sigmoid(conv)

    g = gif_ref[0] + gb_ref[...]
    lf = _log_sigmoid(g)
    g_t = g.T
    lf_t = lf.T

    row = _row_iota((L, L))
    col = _lane_iota((L, L))
    lower = col <= row

    for hd in range(ML_HEADS):
        sl = slice(hd * ML_HEAD_DIM, (hd + 1) * ML_HEAD_DIM)
        q = qk[:, sl] * (ML_HEAD_DIM ** -0.5)
        k = qk[:, ML_WIDTH + hd * ML_HEAD_DIM:ML_WIDTH + (hd + 1) * ML_HEAD_DIM]
        v = mv_ref[0, :, sl]
        i_col = g[:, hd:hd + 1]
        f_col = lf[:, ML_HEADS + hd:ML_HEADS + hd + 1]
        i_row = g_t[hd:hd + 1, :]
        f_row = lf_t[ML_HEADS + hd:ML_HEADS + hd + 1, :]

        b_col = jnp.sum(jnp.where(lower, f_row, 0.0), axis=-1, keepdims=True)
        b_row = jnp.sum(jnp.where(row <= col, f_col, 0.0), axis=0, keepdims=True)
        b_last = b_col[L - 1:L, :]

        m_state = m_ref[hd:hd + 1, 0:1]
        n_state = n_ref[hd:hd + 1, :]
        c_state = c_ref[hd]

        dmat = jnp.where(lower, b_col - b_row + i_row, -jnp.inf)
        r = jnp.max(dmat, axis=-1, keepdims=True)
        p = jnp.exp(dmat - r)
        qb = q.astype(BF16)
        kb = k.astype(BF16)
        sc = lax.dot_general(qb, kb, (((1,), (1,)), ((), ())), preferred_element_type=F32) * p
        num_a = jnp.dot(sc.astype(BF16), v, preferred_element_type=F32)
        den_a = jnp.sum(sc, axis=-1, keepdims=True)

        inter = b_col + m_state
        m_row = jnp.maximum(r, inter)
        e_a = jnp.exp(r - m_row)
        e_b = jnp.exp(inter - m_row)
        num = e_a * num_a + e_b * jnp.dot(qb, c_state.astype(BF16), preferred_element_type=F32)
        den = e_a * den_a + e_b * jnp.sum(q * n_state, axis=-1, keepdims=True)
        hval = num / jnp.maximum(jnp.abs(den), jnp.exp(-m_row))

        w_col = b_last - b_col + i_col
        a = jnp.max(w_col, axis=0, keepdims=True)
        m_new = jnp.maximum(b_last + m_state, a)
        kw = k * jnp.exp(w_col - a)
        d_c = lax.dot_general(kw.astype(BF16), v, (((0,), (0,)), ((), ())), preferred_element_type=F32)
        d_n = jnp.sum(kw, axis=0, keepdims=True)
        decay = jnp.exp(b_last + m_state - m_new)
        gain = jnp.exp(a - m_new)
        c_ref[hd] = decay * c_state + gain * d_c
        n_ref[hd:hd + 1, :] = decay * n_state + gain * d_n
        m_ref[hd:hd + 1, :] = jnp.broadcast_to(m_new, (1, LANES))

        hn = (hval * lax.rsqrt(jnp.mean(hval * hval, axis=-1, keepdims=True) + NORM_EPS)) * go_ref[:, sl]
        y_ref[0, :, sl] = (hn * jax.nn.sigmoid(mo_ref[0, :, sl])).astype(BF16)


def _mlstm(mqk, mv, mo, gif, conv_w, conv_b, gate_b, g_out):
    b_sz, s_len, _ = mqk.shape
    L = min(L_MLSTM, s_len)
    grid = (b_sz, s_len // L)
    tok = lambda width: pl.BlockSpec((1, L, width), lambda b, c: (b, c, 0))
    full = lambda shape: pl.BlockSpec(shape, lambda b, c: (0,) * len(shape))
    return pl.pallas_call(
        functools.partial(_mlstm_kernel, L=L),
        grid=grid,
        in_specs=[tok(2 * ML_WIDTH), tok(ML_WIDTH), tok(ML_WIDTH), tok(LANES),
                  full((CONV_WIDTH, 2 * ML_WIDTH)), full((1, 2 * ML_WIDTH)), full((1, LANES)),
                  full((1, ML_WIDTH))],
        out_specs=tok(ML_WIDTH),
        out_shape=jax.ShapeDtypeStruct((b_sz, s_len, ML_WIDTH), BF16),
        scratch_shapes=[
            pltpu.VMEM((L + SUBLANES, 2 * ML_WIDTH), F32),
            pltpu.VMEM((ML_HEADS, ML_HEAD_DIM, ML_HEAD_DIM), F32),
            pltpu.VMEM((SUBLANES, LANES), F32),
            pltpu.VMEM((SUBLANES, LANES), F32),
        ],
        compiler_params=_cparams(("parallel", "arbitrary")),
        name="mlstm",
    )(mqk, mv, mo, gif, conv_w, conv_b, gate_b, g_out)


def _merge_kernel(x_ref, yda_ref, yml_ref, ga_ref, wg_ref, bg_ref, wda_ref, wml_ref, wo_ref,
                  gf_ref, wr_ref, br_ref,
                  x1_ref, hn_ref, rc_ref, rt_ref, cnt_ref, carry_ref, *, tm):
    @pl.when(pl.program_id(0) == 0)
    def _():
        carry_ref[...] = jnp.zeros_like(carry_ref)

    x = x_ref[...]
    h = ((x * lax.rsqrt(jnp.mean(x * x, axis=-1, keepdims=True) + NORM_EPS)) * ga_ref[...]).astype(BF16)
    gates = jax.nn.sigmoid(jnp.dot(h, wg_ref[...], preferred_element_type=F32) + bg_ref[...])
    a = jnp.dot(yda_ref[...], wda_ref[...], preferred_element_type=F32)
    c = jnp.dot(yml_ref[...], wml_ref[...], preferred_element_type=F32)
    mixed = gates[:, :D_MODEL] * a + gates[:, D_MODEL:] * c
    x1 = x + jnp.dot(mixed.astype(BF16), wo_ref[...], preferred_element_type=F32)
    x1_ref[...] = x1
    hn = (x1 * lax.rsqrt(jnp.mean(x1 * x1, axis=-1, keepdims=True) + NORM_EPS)) * gf_ref[...]
    hn_ref[...] = hn

    logits = jnp.dot(hn.astype(BF16), wr_ref[...], preferred_element_type=F32) + br_ref[...]
    lane = _lane_iota((tm, LANES))
    neg = -jnp.inf
    big = jnp.int32(LANES)

    gl = jnp.where((lane >= N_EXPERTS) & (lane < N_EXPERTS + N_GROUPS), logits, neg)
    gmax = jnp.max(gl, axis=-1, keepdims=True)
    gsum = jnp.sum(jnp.exp(gl - gmax), axis=-1, keepdims=True)
    g_top = 1.0 / gsum
    g_idx = jnp.min(jnp.where(gl == gmax, lane, big), axis=-1, keepdims=True) - N_EXPERTS

    el = jnp.where((lane < N_EXPERTS) & ((lane >> 3) == g_idx), logits, neg)
    emax = jnp.max(el, axis=-1, keepdims=True)
    esum = jnp.sum(jnp.exp(el - emax), axis=-1, keepdims=True)
    e0 = jnp.min(jnp.where(el == emax, lane, big), axis=-1, keepdims=True)
    el2 = jnp.where(lane == e0, neg, el)
    emax2 = jnp.max(el2, axis=-1, keepdims=True)
    e1 = jnp.min(jnp.where(el2 == emax2, lane, big), axis=-1, keepdims=True)
    p0 = 1.0 / esum
    p1 = jnp.exp(emax2 - emax) / esum
    tot = p0 + p1
    w0 = g_top * (p0 / tot)
    w1 = g_top * (p1 / tot)

    hit0 = lane == e0
    hit1 = lane == e1
    onehot = (hit0 | hit1).astype(F32)
    before = (_lane_iota((tm, tm)) < _row_iota((tm, tm))).astype(BF16)
    prefix = jnp.dot(before, onehot.astype(BF16), preferred_element_type=F32) + carry_ref[...]
    r0 = jnp.sum(jnp.where(hit0, prefix, 0.0), axis=-1, keepdims=True)
    r1 = jnp.sum(jnp.where(hit1, prefix, 0.0), axis=-1, keepdims=True)
    carry_ref[...] = carry_ref[...] + jnp.sum(onehot, axis=0, keepdims=True)
    cnt_ref[...] = carry_ref[...]

    rc = jnp.where(lane == 0, e0.astype(F32),
         jnp.where(lane == 1, e1.astype(F32),
         jnp.where(lane == 2, r0,
         jnp.where(lane == 3, r1,
         jnp.where(lane == 4, w0,
         jnp.where(lane == 5, w1, 0.0))))))
    rc_ref[...] = rc
    rt_ref[...] = rc.T[0:SUBLANES, :].astype(jnp.int32)


def _merge(x2, yda, yml, g_attn, w_gate, b_gate, w_da, w_ml, w_out, g_ffn, w_rt, b_rt):
    n_tok, d = x2.shape
    tm = min(TM_MERGE, n_tok)
    grid = (n_tok // tm,)
    tok = lambda width: pl.BlockSpec((tm, width), lambda i: (i, 0))
    full = lambda shape: pl.BlockSpec(shape, lambda i: (0,) * len(shape))
    return pl.pallas_call(
        functools.partial(_merge_kernel, tm=tm),
        grid=grid,
        in_specs=[tok(d), tok(DA_WIDTH), tok(ML_WIDTH), full((1, d)), full((d, 2 * d)), full((1, 2 * d)),
                  full((DA_WIDTH, d)), full((ML_WIDTH, d)), full((d, d)), full((1, d)),
                  full((d, LANES)), full((1, LANES))],
        out_specs=[tok(d), tok(d), tok(LANES), pl.BlockSpec((SUBLANES, tm), lambda i: (0, i)),
                   full((1, LANES))],
        out_shape=[
            jax.ShapeDtypeStruct((n_tok, d), F32),
            jax.ShapeDtypeStruct((n_tok, d), F32),
            jax.ShapeDtypeStruct((n_tok, LANES), F32),
            jax.ShapeDtypeStruct((SUBLANES, n_tok), jnp.int32),
            jax.ShapeDtypeStruct((1, LANES), F32),
        ],
        scratch_shapes=[pltpu.VMEM((1, LANES), F32)],
        compiler_params=_cparams(("arbitrary",)),
        name="merge",
    )(x2, yda, yml, g_attn, w_gate, b_gate, w_da, w_ml, w_out, g_ffn, w_rt, b_rt)


def _dest_kernel(layout_ref, rt_ref, o_ref):
    tile = rt_ref[...]
    first = jnp.zeros_like(tile)
    for ex in range(N_EXPERTS):
        first = jnp.where(tile == ex, layout_ref[ex], first)
    o_ref[...] = first + pltpu.roll(tile, SUBLANES - 2, axis=0)


def _dest(layout, rt):
    _, n_tok = rt.shape
    tm = min(TM_ROWS, n_tok)
    return pl.pallas_call(
        _dest_kernel,
        grid_spec=pltpu.PrefetchScalarGridSpec(
            num_scalar_prefetch=1,
            grid=(n_tok // tm,),
            in_specs=[pl.BlockSpec((SUBLANES, tm), lambda i, ly: (0, i))],
            out_specs=pl.BlockSpec((SUBLANES, tm), lambda i, ly: (0, i)),
        ),
        out_shape=jax.ShapeDtypeStruct((SUBLANES, n_tok), jnp.int32),
        compiler_params=_cparams(("parallel",)),
        name="dest",
    )(layout, rt)


def _load_indices(idx_refs, smem_refs, sem, base, tm):
    copies = [pltpu.make_async_copy(src.at[pl.ds(base, tm)], dst, sem.at[n])
              for n, (src, dst) in enumerate(zip(idx_refs, smem_refs))]
    for cp in copies:
        cp.start()
    for cp in copies:
        cp.wait()


def _dispatch_kernel(pstart_ref, d0_ref, d1_ref, hn_ref, xs_ref,
                     d0_s, d1_s, zblk, isem, sem, zsem, *, tm, n_blocks):

    @pl.when(pl.program_id(0) == 0)
    def _():
        zblk[...] = jnp.zeros_like(zblk)

        def zero_copy(row0):
            return pltpu.make_async_copy(zblk, xs_ref.at[pl.ds(pl.multiple_of(row0, EXPERT_BLK), EXPERT_BLK)], zsem)

        def last_block(e, n):
            end = pstart_ref[N_EXPERTS + e]
            has = end > pstart_ref[e]

            @pl.when(has)
            def _():
                zero_copy(end - EXPERT_BLK).start()

            return n + has.astype(jnp.int32)

        def tail_block(j, n):
            zero_copy(j * EXPERT_BLK).start()
            return n + 1

        n = lax.fori_loop(0, N_EXPERTS, last_block, 0)
        n = lax.fori_loop(pstart_ref[2 * N_EXPERTS], n_blocks, tail_block, n)

        def drain(_, c):
            zero_copy(0).wait()
            return c

        lax.fori_loop(0, n, drain, 0)

    base = pl.multiple_of(pl.program_id(0) * tm, tm)
    _load_indices((d0_ref, d1_ref), (d0_s, d1_s), isem, base, tm)

    def row_copy(t, dest):
        return pltpu.make_async_copy(hn_ref.at[pl.ds(t, 1)], xs_ref.at[pl.ds(dest, 1)], sem)

    def issue(t, _):
        row_copy(t, d0_s[t]).start(priority=0)
        row_copy(t, d1_s[t]).start(priority=1)
        return 0

    lax.fori_loop(0, tm, issue, 0, unroll=ISSUE_UNROLL)

    for _ in range(2):
        pltpu.make_async_copy(hn_ref, xs_ref.at[pl.ds(0, tm)], sem).wait()


def _dispatch(layout, d0, d1, hn, cap):
    n_tok, d = hn.shape
    tm = min(TM_ROWS, n_tok)
    grid = (n_tok // tm,)
    any_spec = pl.BlockSpec(memory_space=pl.ANY)
    return pl.pallas_call(
        functools.partial(_dispatch_kernel, tm=tm, n_blocks=cap // EXPERT_BLK),
        grid_spec=pltpu.PrefetchScalarGridSpec(
            num_scalar_prefetch=1,
            grid=grid,
            in_specs=[any_spec, any_spec, pl.BlockSpec((tm, d), lambda i, ps: (i, 0))],
            out_specs=any_spec,
            scratch_shapes=[pltpu.SMEM((tm,), jnp.int32)] * 2
            + [pltpu.VMEM((EXPERT_BLK, d), F32), pltpu.SemaphoreType.DMA((2,)), pltpu.SemaphoreType.DMA,
               pltpu.SemaphoreType.DMA],
        ),
        out_shape=jax.ShapeDtypeStruct((cap, d), F32),
        compiler_params=_cparams(("arbitrary",)),
        name="dispatch",
    )(layout, d0, d1, hn)


def _experts_kernel(be_ref, nb_ref, xs_ref, w1_ref, w3_ref, w2_ref, y_ref, w1b, w3b, w2b):
    i = pl.program_id(0)
    used = i < nb_ref[0]

    @pl.when(used & ((i == 0) | (be_ref[i] != be_ref[jnp.maximum(i - 1, 0)])))
    def _():
        w1b[...] = w1_ref[0].astype(BF16)
        w3b[...] = w3_ref[0].astype(BF16)
        w2b[...] = w2_ref[0].astype(BF16)

    @pl.when(used)
    def _():
        xb = xs_ref[...].astype(BF16)
        a = jnp.dot(xb, w1b[...], preferred_element_type=F32)
        b = jnp.dot(xb, w3b[...], preferred_element_type=F32)
        hid = (a * jax.nn.sigmoid(a)) * b
        y_ref[...] = jnp.dot(hid.astype(BF16), w2b[...], preferred_element_type=F32)

    @pl.when(jnp.logical_not(used))
    def _():
        y_ref[...] = jnp.zeros_like(y_ref)


def _experts(blk_expert, n_blocks_used, xs, w1, w3, w2):
    cap, d = xs.shape
    n_blocks = cap // EXPERT_BLK
    rows = lambda i, be, nb: (jnp.minimum(i, nb[0] - 1), 0)
    wsel = lambda i, be, nb: (be[jnp.minimum(i, nb[0] - 1)], 0, 0)
    return pl.pallas_call(
        _experts_kernel,
        grid_spec=pltpu.PrefetchScalarGridSpec(
            num_scalar_prefetch=2,
            grid=(n_blocks,),
            in_specs=[pl.BlockSpec((EXPERT_BLK, d), rows),
                      pl.BlockSpec((1, d, EXPERT_FF), wsel),
                      pl.BlockSpec((1, d, EXPERT_FF), wsel),
                      pl.BlockSpec((1, EXPERT_FF, d), wsel)],
            out_specs=pl.BlockSpec((EXPERT_BLK, d), lambda i, be, nb: (i, 0)),
            scratch_shapes=[pltpu.VMEM((d, EXPERT_FF), BF16), pltpu.VMEM((d, EXPERT_FF), BF16),
                            pltpu.VMEM((EXPERT_FF, d), BF16)],
        ),
        out_shape=jax.ShapeDtypeStruct((cap, d), F32),
        compiler_params=_cparams(("arbitrary",)),
        name="experts",
    )(blk_expert, n_blocks_used, xs, w1, w3, w2)


def _combine_kernel(d0_ref, d1_ref, x1_ref, rc_ref, ys_ref, o_ref, d0_s, d1_s, ybuf, isem, sem, *, tm):
    base = pl.multiple_of(pl.program_id(0) * tm, tm)
    _load_indices((d0_ref, d1_ref), (d0_s, d1_s), isem, base, tm)

    def row_copy(slot, t, src):
        return pltpu.make_async_copy(ys_ref.at[pl.ds(src, 1)], ybuf.at[slot, pl.ds(t, 1)], sem)

    def issue(t, _):
        row_copy(0, t, d0_s[t]).start(priority=0)
        row_copy(1, t, d1_s[t]).start(priority=1)
        return 0

    lax.fori_loop(0, tm, issue, 0, unroll=ISSUE_UNROLL)

    for slot in range(2):
        pltpu.make_async_copy(ys_ref.at[pl.ds(0, tm)], ybuf.at[slot], sem).wait()

    rc = rc_ref[...]
    o_ref[...] = x1_ref[...] + (rc[:, 4:5] * ybuf[0] + rc[:, 5:6] * ybuf[1])


def _combine(d0, d1, x1, rc, ys):
    n_tok, d = x1.shape
    tm = min(TM_ROWS, n_tok)
    any_spec = pl.BlockSpec(memory_space=pl.ANY)
    return pl.pallas_call(
        functools.partial(_combine_kernel, tm=tm),
        grid=(n_tok // tm,),
        in_specs=[any_spec, any_spec,
                  pl.BlockSpec((tm, d), lambda i: (i, 0)),
                  pl.BlockSpec((tm, LANES), lambda i: (i, 0)),
                  any_spec],
        out_specs=pl.BlockSpec((tm, d), lambda i: (i, 0)),
        scratch_shapes=[pltpu.SMEM((tm,), jnp.int32)] * 2
        + [pltpu.VMEM((2, tm, d), F32), pltpu.SemaphoreType.DMA((2,)), pltpu.SemaphoreType.DMA],
        out_shape=jax.ShapeDtypeStruct((n_tok, d), F32),
        compiler_params=_cparams(("arbitrary",)),
        name="combine",
    )(d0, d1, x1, rc, ys)


def _pad_lanes(vec, width=LANES):
    return jnp.zeros((1, width), F32).at[0, :vec.shape[0]].set(vec.astype(F32))


def _layout_w_in(w_in):
    d = w_in.shape[0]
    gates = jnp.concatenate([w_in[:, 3584:3592], jnp.zeros((d, LANES - 2 * ML_HEADS), w_in.dtype)], axis=-1)
    w_all = jnp.concatenate([w_in[:, 512:1024], w_in[:, 1536:3584], gates], axis=-1).astype(BF16)
    w_t = jnp.concatenate([w_in[:, 0:512], w_in[:, 1024:1536]], axis=-1).T.astype(BF16)
    return w_all, w_t


def kernel(x, attn_norm_g, w_in, da_q_norm_g, da_k_norm_g, da_lambda_q1, da_lambda_k1, da_lambda_q2, da_lambda_k2, da_out_norm_g, ml_conv_w, ml_conv_b, ml_i_bias, ml_f_bias, ml_out_norm_g, w_branch_da, w_branch_ml, w_gate, b_gate, w_out, ffn_norm_g, w_group, b_group, w_router, b_router, w1, w3, w2):
    b_sz, s_len, d = x.shape
    n_tok = b_sz * s_len
    ly = 0

    w_all, w_t = _layout_w_in(w_in[ly])
    g_attn = attn_norm_g[ly].reshape(1, d)
    gq_col = _pad_lanes(da_q_norm_g[ly]).reshape(LANES, 1)
    gk = _pad_lanes(da_k_norm_g[ly])
    lam_vecs = jnp.concatenate([_pad_lanes(v[ly]) for v in
                                (da_lambda_q1, da_lambda_k1, da_lambda_q2, da_lambda_k2)], axis=0)
    gate_b = _pad_lanes(jnp.concatenate([ml_i_bias[ly], ml_f_bias[ly]]))
    w_rt = jnp.concatenate([w_router[ly], w_group[ly],
                            jnp.zeros((d, LANES - N_EXPERTS - N_GROUPS), F32)], axis=-1).astype(BF16)
    b_rt = _pad_lanes(jnp.concatenate([b_router[ly], b_group[ly]]))

    qt, ka, vt, mqk, mv, mo, gif = _in_proj(x, g_attn, w_all, w_t, gq_col, gk)
    y_da = _attn(lam_vecs, qt, ka, vt, da_out_norm_g[ly].reshape(DA_V_DIM, 1))
    y_ml = _mlstm(mqk, mv, mo, gif, ml_conv_w[ly], ml_conv_b[ly].reshape(1, -1), gate_b,
                  ml_out_norm_g[ly].reshape(1, ML_WIDTH))

    x1, hn, rc, rt, counts = _merge(
        x.reshape(n_tok, d), y_da.reshape(n_tok, DA_WIDTH), y_ml.reshape(n_tok, ML_WIDTH), g_attn,
        w_gate[ly].astype(BF16), b_gate[ly].reshape(1, -1), w_branch_da[ly].astype(BF16),
        w_branch_ml[ly].astype(BF16), w_out[ly].astype(BF16), ffn_norm_g[ly].reshape(1, d), w_rt, b_rt)

    cnt = counts[0, :N_EXPERTS].astype(jnp.int32)
    padded = (cnt + EXPERT_BLK - 1) // EXPERT_BLK * EXPERT_BLK
    pend = jnp.cumsum(padded)
    pstart = (pend - padded).astype(jnp.int32)
    cap = 2 * n_tok + N_EXPERTS * EXPERT_BLK
    n_blocks = cap // EXPERT_BLK
    blk_row0 = jnp.arange(n_blocks, dtype=jnp.int32) * EXPERT_BLK
    blk_expert = jnp.minimum(jnp.sum((pend[None, :] <= blk_row0[:, None]).astype(jnp.int32), axis=1),
                             N_EXPERTS - 1)
    n_used = (pend[-1:] // EXPERT_BLK).astype(jnp.int32)
    layout = jnp.concatenate([pstart, pend.astype(jnp.int32), n_used])
    dest = _dest(layout, rt)
    d0, d1 = dest[0], dest[1]

    xs = _dispatch(layout, d0, d1, hn, cap)
    ys = _experts(blk_expert, n_used, xs, w1[ly], w3[ly], w2[ly])
    out = _combine(d0, d1, x1, rc, ys)
    return out.reshape(b_sz, s_len, d)
```

```python
import functools
import math

import jax
import jax.numpy as jnp
import numpy as np
from jax import lax
from jax.experimental import pallas as pl
from jax.experimental.pallas import tpu as pltpu

F32 = jnp.float32
BF16 = jnp.bfloat16

D_MODEL = 1024
DA_HEADS = 4
DA_HEAD_DIM = 64
DA_V_DIM = 128
DA_WIDTH = 512
ML_HEADS = 4
ML_HEAD_DIM = 128
ML_WIDTH = 512
CONV_WIDTH = 4
N_GROUPS = 4
EXPERTS_PER_GROUP = 8
N_EXPERTS = 32
EXPERT_FF = 512
NORM_EPS = 1e-6
LAM_INIT = 0.8 - 0.6 * math.exp(-0.3 * 0)

LANES = 128
SUBLANES = 8
VMEM_LIMIT = 56 * 1024 * 1024

SEG_K = 0
SEG_MQK = SEG_K + 512
SEG_MO = SEG_MQK + 1024
SEG_IF = SEG_MO + 512
W_ALL = SEG_IF + LANES

LOG2E = math.log2(math.e)
ALIBI_PIECES = 4
AUX0 = DA_HEAD_DIM
POS_PERIOD = 512
DV_AUG = DA_V_DIM + 16


def _bf16_round(val):
    bits = np.float32(val).view(np.uint32)
    bits = (bits + np.uint32(0x7FFF) + ((bits >> np.uint32(16)) & np.uint32(1))) & np.uint32(0xFFFF0000)
    return float(bits.view(np.float32))


def _alibi_pieces(hd):
    rest, pieces = 2.0 ** (-8.0 * (hd + 1) / DA_HEADS) * LOG2E, []
    for _ in range(ALIBI_PIECES):
        pieces.append(_bf16_round(rest))
        rest -= pieces[-1]
    return pieces

TM_PROJ = 512
TQ_ATTN = 512
ATTN_HEADS = 4
L_MLSTM = 256
TM_MERGE = 512
TM_ROWS = 1024
EXPERT_BLK = 256
ISSUE_UNROLL = 8


def _cparams(sem):
    return pltpu.CompilerParams(dimension_semantics=sem, vmem_limit_bytes=VMEM_LIMIT)


def _lane_iota(shape):
    return lax.broadcasted_iota(jnp.int32, shape, len(shape) - 1)


def _row_iota(shape):
    return lax.broadcasted_iota(jnp.int32, shape, len(shape) - 2)


def _in_proj_kernel(x_ref, g_ref, w_ref, wt_ref, gq_ref, gk_ref, aux_ref,
                    qt_ref, ka_ref, vt_ref, mqk_ref, mvt_ref, mo_ref, gif_ref, *, tm):
    x = x_ref[0]
    ms = jnp.mean(x * x, axis=-1, keepdims=True)
    h = ((x * lax.rsqrt(ms + NORM_EPS)) * g_ref[...]).astype(BF16)

    def seg(lo, width):
        return jnp.dot(h, w_ref[:, lo:lo + width], preferred_element_type=F32)

    inv_dh = 1.0 / DA_HEAD_DIM

    t_all = lax.dot_general(wt_ref[...], h, (((1,), (1,)), ((), ())), preferred_element_type=F32)
    dh = DA_HEAD_DIM
    for hd in range(DA_HEADS):
        aux_rows = jnp.broadcast_to(aux_ref[hd, dh:LANES, :], (LANES - dh, tm)).astype(BF16)
        for mp in range(2):
            off = (hd * 2 + mp) * dh
            q = t_all[off:off + dh, :]
            qn = (q * lax.rsqrt(jnp.sum(q * q, axis=0, keepdims=True) * inv_dh + NORM_EPS)) * gq_ref[0:dh, :]
            qt_ref[0, hd, mp * LANES:mp * LANES + dh, :] = (qn * (dh ** -0.5 * LOG2E)).astype(BF16)
            qt_ref[0, hd, mp * LANES + dh:(mp + 1) * LANES, :] = aux_rows
        v_lo = DA_HEADS * 2 * dh + hd * DA_V_DIM
        vt_ref[0, hd, 0:DA_V_DIM, :] = t_all[v_lo:v_lo + DA_V_DIM, :].astype(BF16)
        ones_row = (_row_iota((DV_AUG - DA_V_DIM, tm)) == 0).astype(BF16)
        vt_ref[0, hd, DA_V_DIM:DV_AUG, :] = ones_row

    lane = _lane_iota((tm, LANES))
    pos = pl.program_id(1) * tm + _row_iota((tm, LANES))
    pos_lo = (pos & 255).astype(F32)
    pos_hi = (pos & 256).astype(F32)
    is_aux = (lane >= AUX0) & (lane < AUX0 + 2 * ALIBI_PIECES)
    is_lo = is_aux & ((lane & 1) == 0)
    is_hi = is_aux & ((lane & 1) == 1)
    real = lane < dh
    k_all = seg(SEG_K, DA_HEADS * 2 * dh)
    for hd in range(DA_HEADS):
        k_pair = k_all[:, hd * LANES:(hd + 1) * LANES]
        for mp in range(2):
            k = jnp.where(real, k_pair if mp == 0 else pltpu.roll(k_pair, dh, axis=1), 0.0)
            kn = (k * lax.rsqrt(jnp.sum(k * k, axis=-1, keepdims=True) * inv_dh + NORM_EPS)) * gk_ref[...]
            kn = jnp.where(is_lo, pos_lo, jnp.where(is_hi, pos_hi, kn))
            ka_ref[0, hd, :, mp * LANES:(mp + 1) * LANES] = kn.astype(BF16)

    mqk_ref[0] = seg(SEG_MQK, 2 * ML_WIDTH)
    mvt_ref[0] = t_all[DA_HEADS * 2 * dh + DA_WIDTH:, :].astype(BF16)
    mo_ref[0] = seg(SEG_MO, ML_WIDTH)
    gif_ref[0] = seg(SEG_IF, LANES)


def _in_proj(x, g_attn, w_all, w_t, gq_col, gk):
    b_sz, s_len, d = x.shape
    tm = min(TM_PROJ, s_len)
    grid = (b_sz, s_len // tm)
    tok = lambda width: pl.BlockSpec((1, tm, width), lambda b, s: (b, s, 0))
    full = lambda shape: pl.BlockSpec(shape, lambda b, s: (0,) * len(shape))
    n_t = w_t.shape[0]
    aux = np.zeros((DA_HEADS, LANES, 1), np.float32)
    for hd in range(DA_HEADS):
        aux[hd, AUX0:AUX0 + 2 * ALIBI_PIECES, 0] = np.repeat(_alibi_pieces(hd), 2)
    return pl.pallas_call(
        functools.partial(_in_proj_kernel, tm=tm),
        grid=grid,
        in_specs=[tok(d), full((1, d)), full((d, W_ALL)), full((n_t, d)), full((LANES, 1)), full((1, LANES)),
                  full((DA_HEADS, LANES, 1))],
        out_specs=[pl.BlockSpec((1, DA_HEADS, 2 * LANES, tm), lambda b, s: (b, 0, 0, s)),
                   pl.BlockSpec((1, DA_HEADS, tm, 2 * LANES), lambda b, s: (b, 0, s, 0)),
                   pl.BlockSpec((1, DA_HEADS, DV_AUG, tm), lambda b, s: (b, 0, 0, s)),
                   tok(2 * ML_WIDTH), pl.BlockSpec((1, ML_WIDTH, tm), lambda b, s: (b, 0, s)), tok(ML_WIDTH),
                   tok(LANES)],
        out_shape=[
            jax.ShapeDtypeStruct((b_sz, DA_HEADS, 2 * LANES, s_len), BF16),
            jax.ShapeDtypeStruct((b_sz, DA_HEADS, s_len, 2 * LANES), BF16),
            jax.ShapeDtypeStruct((b_sz, DA_HEADS, DV_AUG, s_len), BF16),
            jax.ShapeDtypeStruct((b_sz, s_len, 2 * ML_WIDTH), F32),
            jax.ShapeDtypeStruct((b_sz, ML_WIDTH, s_len), BF16),
            jax.ShapeDtypeStruct((b_sz, s_len, ML_WIDTH), F32),
            jax.ShapeDtypeStruct((b_sz, s_len, LANES), F32),
        ],
        compiler_params=_cparams(("parallel", "parallel")),
        name="in_proj",
    )(x, g_attn, w_all, w_t, gq_col, gk, jnp.asarray(aux))


def _attn_kernel(lam_ref, qt_ref, ka_ref, vt_ref, go_ref, o_ref, acc_ref, *, tq):
    hp = pl.program_id(1)
    qi = pl.program_id(2)
    slopes = [sum(_alibi_pieces(h)) for h in range(DA_HEADS)]
    pair_slopes = []
    for hh in range(ATTN_HEADS):
        sel = slopes[hh]
        for step in range(1, DA_HEADS // ATTN_HEADS):
            sel = jnp.where(hp == step, slopes[step * ATTN_HEADS + hh], sel)
        pair_slopes.append(sel)
    chains = [(hh, mp) for hh in range(ATTN_HEADS) for mp in range(2)]
    qts = {(hh, mp): qt_ref[0, hh, mp * LANES:(mp + 1) * LANES, :] for hh, mp in chains}
    causal = _row_iota((tq, tq)) <= _lane_iota((tq, tq))
    acc_ref[...] = jnp.zeros_like(acc_ref)

    def tile(kv, carry, masked):
        start = pl.multiple_of(kv * tq, tq)
        base = ((jnp.zeros((1, tq), jnp.int32) + (start // POS_PERIOD) * POS_PERIOD) - qi * tq).astype(F32)
        scores = {}
        for hh in range(ATTN_HEADS):
            k = ka_ref[0, hh, pl.ds(start, tq), :]
            for mp in range(2):
                scores[hh, mp] = jnp.dot(k[:, mp * LANES:(mp + 1) * LANES], qts[hh, mp],
                                         preferred_element_type=F32)
        out = {}
        for hh in range(ATTN_HEADS):
            c = base * pair_slopes[hh]
            probs = []
            for mp in range(2):
                s = scores[hh, mp]
                if masked:
                    s = jnp.where(causal, s, -jnp.inf)
                m_old = carry[chains.index((hh, mp))]
                m_new = jnp.maximum(m_old, jnp.max(s, axis=0, keepdims=True) + c)
                probs.append((jnp.exp2(m_old - m_new), jnp.exp2(s - (m_new - c)).astype(BF16)))
                out[hh, mp] = m_new
            vt = vt_ref[0, hh, :, pl.ds(start, tq)]
            for mp in range(2):
                alpha, p = probs[mp]
                acc_ref[hh, mp] = alpha * acc_ref[hh, mp] + jnp.dot(vt, p, preferred_element_type=F32)
        return tuple(out[ch] for ch in chains)

    init = tuple(jnp.full((1, tq), -jnp.inf, F32) for _ in chains)
    carry = lax.fori_loop(0, qi, lambda kv, cr: tile(kv, cr, False), init)
    tile(qi, carry, True)

    lam_v = lam_ref[...]
    lam = (jnp.exp(jnp.sum(lam_v[0:1] * lam_v[1:2], axis=-1, keepdims=True))
           - jnp.exp(jnp.sum(lam_v[2:3] * lam_v[3:4], axis=-1, keepdims=True)) + LAM_INIT)
    for hh in range(ATTN_HEADS):
        l1 = acc_ref[hh, 0, DA_V_DIM:DA_V_DIM + 1, :]
        l2 = acc_ref[hh, 1, DA_V_DIM:DA_V_DIM + 1, :]
        o = acc_ref[hh, 0, 0:DA_V_DIM, :] / l1 - lam * (acc_ref[hh, 1, 0:DA_V_DIM, :] / l2)
        o = (o * lax.rsqrt(jnp.mean(o * o, axis=0, keepdims=True) + NORM_EPS)) * go_ref[...]
        o_ref[0, :, hh * DA_V_DIM:(hh + 1) * DA_V_DIM] = (o * (1.0 - LAM_INIT)).T.astype(BF16)


def _attn(lam_vecs, qt, ka, vt, g_out_col):
    b_sz, _, s_len, _ = ka.shape
    tq = min(TQ_ATTN, s_len)
    grid = (b_sz, DA_HEADS // ATTN_HEADS, s_len // tq)
    return pl.pallas_call(
        functools.partial(_attn_kernel, tq=tq),
        grid=grid,
        in_specs=[
            pl.BlockSpec((4, LANES), lambda b, h, i: (0, 0)),
            pl.BlockSpec((1, ATTN_HEADS, 2 * LANES, tq), lambda b, h, i: (b, h, 0, i)),
            pl.BlockSpec((1, ATTN_HEADS, s_len, 2 * LANES), lambda b, h, i: (b, h, 0, 0)),
            pl.BlockSpec((1, ATTN_HEADS, DV_AUG, s_len), lambda b, h, i: (b, h, 0, 0)),
            pl.BlockSpec((DA_V_DIM, 1), lambda b, h, i: (0, 0)),
        ],
        out_specs=pl.BlockSpec((1, tq, ATTN_HEADS * DA_V_DIM), lambda b, h, i: (b, i, h)),
        out_shape=jax.ShapeDtypeStruct((b_sz, s_len, DA_WIDTH), BF16),
        scratch_shapes=[pltpu.VMEM((ATTN_HEADS, 2, DV_AUG, tq), F32)],
        compiler_params=_cparams(("parallel", "parallel", "arbitrary")),
        name="attn",
    )(lam_vecs, qt, ka, vt, g_out_col)


def _log_sigmoid(x):
    return jnp.minimum(x, 0.0) - jnp.log1p(jnp.exp(-jnp.abs(x)))


def _mlstm_kernel(mqk_ref, mvt_ref, mo_ref, gif_ref, cw_ref, cb_ref, gb_ref, go_ref,
                  y_ref, buf_ref, c_ref, n_ref, m_ref, *, L):
    halo = SUBLANES

    @pl.when(pl.program_id(1) == 0)
    def _():
        buf_ref[0:halo, :] = jnp.zeros((halo, 2 * ML_WIDTH), F32)
        c_ref[...] = jnp.zeros_like(c_ref)
        n_ref[...] = jnp.zeros_like(n_ref)
        m_ref[...] = jnp.zeros_like(m_ref)

    buf_ref[halo:halo + L, :] = mqk_ref[0]
    conv = jnp.broadcast_to(cb_ref[...], (L, 2 * ML_WIDTH))
    for j in range(CONV_WIDTH):
        lo = halo - (CONV_WIDTH - 1) + j
        conv = conv + buf_ref[lo:lo + L, :] * cw_ref[j:j + 1, :]
    buf_ref[0:halo, :] = buf_ref[L:L + halo, :]
    qk = conv * jax.nn.sigmoid(conv)

    g = gif_ref[0] + gb_ref[...]
    lf = _log_sigmoid(g)
    lf_t = lf.T

    row = _row_iota((L, L))
    col = _lane_iota((L, L))
    upper = row <= col
    nt = (((1,), (1,)), ((), ()))

    for hd in range(ML_HEADS):
        sl = slice(hd * ML_HEAD_DIM, (hd + 1) * ML_HEAD_DIM)
        q = qk[:, sl] * (ML_HEAD_DIM ** -0.5)
        k = qk[:, ML_WIDTH + hd * ML_HEAD_DIM:ML_WIDTH + (hd + 1) * ML_HEAD_DIM]
        vt = mvt_ref[0, sl, :]
        i_col = g[:, hd:hd + 1]
        f_col = lf[:, ML_HEADS + hd:ML_HEADS + hd + 1]
        f_row = lf_t[ML_HEADS + hd:ML_HEADS + hd + 1, :]

        b_col = jnp.sum(jnp.where(col <= row, f_row, 0.0), axis=-1, keepdims=True)
        b_row = jnp.sum(jnp.where(upper, f_col, 0.0), axis=0, keepdims=True)
        b_last = b_row[:, L - 1:L]

        m_state = m_ref[hd:hd + 1, 0:1]
        n_state = n_ref[hd:hd + 1, :]
        ct_state = c_ref[hd]

        dmat = jnp.where(upper, b_row + (i_col - b_col), -jnp.inf)
        r = jnp.max(dmat, axis=0, keepdims=True)
        p = jnp.exp(dmat - r)
        qb = q.astype(BF16)
        kb = k.astype(BF16)
        sc = lax.dot_general(kb, qb, nt, preferred_element_type=F32) * p
        num_a = jnp.dot(vt, sc.astype(BF16), preferred_element_type=F32)
        den_a = jnp.sum(sc, axis=0, keepdims=True)

        inter = b_row + m_state
        m_row = jnp.maximum(r, inter)
        e_a = jnp.exp(r - m_row)
        e_b = jnp.exp(inter - m_row)
        q_c = lax.dot_general(ct_state.astype(BF16), qb, nt, preferred_element_type=F32)
        n_rows = jnp.broadcast_to(n_state, (SUBLANES, ML_HEAD_DIM)).astype(BF16)
        q_n = lax.dot_general(n_rows, qb, nt, preferred_element_type=F32)[0:1, :]
        num = e_a * num_a + e_b * q_c
        den = e_a * den_a + e_b * q_n
        hval = num / jnp.maximum(jnp.abs(den), jnp.exp(-m_row))

        w_col = b_last - b_col + i_col
        a = jnp.max(w_col, axis=0, keepdims=True)
        m_new = jnp.maximum(b_last + m_state, a)
        kw = k * jnp.exp(w_col - a)
        d_c = jnp.dot(vt, kw.astype(BF16), preferred_element_type=F32)
        d_n = jnp.sum(kw, axis=0, keepdims=True)
        decay = jnp.exp(b_last + m_state - m_new)
        gain = jnp.exp(a - m_new)
        c_ref[hd] = decay * ct_state + gain * d_c
        n_ref[hd:hd + 1, :] = decay * n_state + gain * d_n
        m_ref[hd:hd + 1, :] = jnp.broadcast_to(m_new, (1, LANES))

        hn_t = hval * lax.rsqrt(jnp.mean(hval * hval, axis=0, keepdims=True) + NORM_EPS)
        y_ref[0, :, sl] = (hn_t.T * go_ref[:, sl] * jax.nn.sigmoid(mo_ref[0, :, sl])).astype(BF16)


def _mlstm(mqk, mvt, mo, gif, conv_w, conv_b, gate_b, g_out):
    b_sz, s_len, _ = mqk.shape
    L = min(L_MLSTM, s_len)
    grid = (b_sz, s_len // L)
    tok = lambda width: pl.BlockSpec((1, L, width), lambda b, c: (b, c, 0))
    full = lambda shape: pl.BlockSpec(shape, lambda b, c: (0,) * len(shape))
    return pl.pallas_call(
        functools.partial(_mlstm_kernel, L=L),
        grid=grid,
        in_specs=[tok(2 * ML_WIDTH), pl.BlockSpec((1, ML_WIDTH, L), lambda b, c: (b, 0, c)), tok(ML_WIDTH),
                  tok(LANES),
                  full((CONV_WIDTH, 2 * ML_WIDTH)), full((1, 2 * ML_WIDTH)), full((1, LANES)),
                  full((1, ML_WIDTH))],
        out_specs=tok(ML_WIDTH),
        out_shape=jax.ShapeDtypeStruct((b_sz, s_len, ML_WIDTH), BF16),
        scratch_shapes=[
            pltpu.VMEM((L + SUBLANES, 2 * ML_WIDTH), F32),
            pltpu.VMEM((ML_HEADS, ML_HEAD_DIM, ML_HEAD_DIM), F32),
            pltpu.VMEM((SUBLANES, LANES), F32),
            pltpu.VMEM((SUBLANES, LANES), F32),
        ],
        compiler_params=_cparams(("parallel", "arbitrary")),
        name="mlstm",
    )(mqk, mvt, mo, gif, conv_w, conv_b, gate_b, g_out)


def _merge_kernel(x_ref, yda_ref, yml_ref, ga_ref, wg_ref, bg_ref, wda_ref, wml_ref, wo_ref,
                  gf_ref, wr_ref, br_ref,
                  x1_ref, hn_ref, rc_ref, rt_ref, cnt_ref, carry_ref, *, tm):
    @pl.when(pl.program_id(0) == 0)
    def _():
        carry_ref[...] = jnp.zeros_like(carry_ref)

    x = x_ref[...]
    h = ((x * lax.rsqrt(jnp.mean(x * x, axis=-1, keepdims=True) + NORM_EPS)) * ga_ref[...]).astype(BF16)
    gates = jax.nn.sigmoid(jnp.dot(h, wg_ref[...], preferred_element_type=F32) + bg_ref[...])
    a = jnp.dot(yda_ref[...], wda_ref[...], preferred_element_type=F32)
    c = jnp.dot(yml_ref[...], wml_ref[...], preferred_element_type=F32)
    mixed = gates[:, :D_MODEL] * a + gates[:, D_MODEL:] * c
    x1 = x + jnp.dot(mixed.astype(BF16), wo_ref[...], preferred_element_type=F32)
    x1_ref[...] = x1
    hn = (x1 * lax.rsqrt(jnp.mean(x1 * x1, axis=-1, keepdims=True) + NORM_EPS)) * gf_ref[...]
    hn_ref[...] = hn

    logits = jnp.dot(hn.astype(BF16), wr_ref[...], preferred_element_type=F32) + br_ref[...]
    lane = _lane_iota((tm, LANES))
    neg = -jnp.inf
    big = jnp.int32(LANES)

    gl = jnp.where((lane >= N_EXPERTS) & (lane < N_EXPERTS + N_GROUPS), logits, neg)
    gmax = jnp.max(gl, axis=-1, keepdims=True)
    gsum = jnp.sum(jnp.exp(gl - gmax), axis=-1, keepdims=True)
    g_top = 1.0 / gsum
    g_idx = jnp.min(jnp.where(gl == gmax, lane, big), axis=-1, keepdims=True) - N_EXPERTS

    el = jnp.where((lane < N_EXPERTS) & ((lane >> 3) == g_idx), logits, neg)
    emax = jnp.max(el, axis=-1, keepdims=True)
    esum = jnp.sum(jnp.exp(el - emax), axis=-1, keepdims=True)
    e0 = jnp.min(jnp.where(el == emax, lane, big), axis=-1, keepdims=True)
    el2 = jnp.where(lane == e0, neg, el)
    emax2 = jnp.max(el2, axis=-1, keepdims=True)
    e1 = jnp.min(jnp.where(el2 == emax2, lane, big), axis=-1, keepdims=True)
    p0 = 1.0 / esum
    p1 = jnp.exp(emax2 - emax) / esum
    tot = p0 + p1
    w0 = g_top * (p0 / tot)
    w1 = g_top * (p1 / tot)

    hit0 = lane == e0
    hit1 = lane == e1
    onehot = (hit0 | hit1).astype(F32)
    before = (_lane_iota((tm, tm)) < _row_iota((tm, tm))).astype(BF16)
    prefix = jnp.dot(before, onehot.astype(BF16), preferred_element_type=F32) + carry_ref[...]
    r0 = jnp.sum(jnp.where(hit0, prefix, 0.0), axis=-1, keepdims=True)
    r1 = jnp.sum(jnp.where(hit1, prefix, 0.0), axis=-1, keepdims=True)
    carry_ref[...] = carry_ref[...] + jnp.sum(onehot, axis=0, keepdims=True)
    cnt_ref[...] = carry_ref[...]

    rc = jnp.where(lane == 0, e0.astype(F32),
         jnp.where(lane == 1, e1.astype(F32),
         jnp.where(lane == 2, r0,
         jnp.where(lane == 3, r1,
         jnp.where(lane == 4, w0,
         jnp.where(lane == 5, w1, 0.0))))))
    rc_ref[...] = rc
    rt_ref[...] = rc.T[0:SUBLANES, :].astype(jnp.int32)


def _merge(x2, yda, yml, g_attn, w_gate, b_gate, w_da, w_ml, w_out, g_ffn, w_rt, b_rt):
    n_tok, d = x2.shape
    tm = min(TM_MERGE, n_tok)
    grid = (n_tok // tm,)
    tok = lambda width: pl.BlockSpec((tm, width), lambda i: (i, 0))
    full = lambda shape: pl.BlockSpec(shape, lambda i: (0,) * len(shape))
    return pl.pallas_call(
        functools.partial(_merge_kernel, tm=tm),
        grid=grid,
        in_specs=[tok(d), tok(DA_WIDTH), tok(ML_WIDTH), full((1, d)), full((d, 2 * d)), full((1, 2 * d)),
                  full((DA_WIDTH, d)), full((ML_WIDTH, d)), full((d, d)), full((1, d)),
                  full((d, LANES)), full((1, LANES))],
        out_specs=[tok(d), tok(d), tok(LANES), pl.BlockSpec((SUBLANES, tm), lambda i: (0, i)),
                   full((1, LANES))],
        out_shape=[
            jax.ShapeDtypeStruct((n_tok, d), F32),
            jax.ShapeDtypeStruct((n_tok, d), F32),
            jax.ShapeDtypeStruct((n_tok, LANES), F32),
            jax.ShapeDtypeStruct((SUBLANES, n_tok), jnp.int32),
            jax.ShapeDtypeStruct((1, LANES), F32),
        ],
        scratch_shapes=[pltpu.VMEM((1, LANES), F32)],
        compiler_params=_cparams(("arbitrary",)),
        name="merge",
    )(x2, yda, yml, g_attn, w_gate, b_gate, w_da, w_ml, w_out, g_ffn, w_rt, b_rt)


def _dest_kernel(layout_ref, rt_ref, o_ref):
    tile = rt_ref[...]
    first = jnp.zeros_like(tile)
    for ex in range(N_EXPERTS):
        first = jnp.where(tile == ex, layout_ref[ex], first)
    o_ref[...] = first + pltpu.roll(tile, SUBLANES - 2, axis=0)


def _dest(layout, rt):
    _, n_tok = rt.shape
    tm = min(TM_ROWS, n_tok)
    return pl.pallas_call(
        _dest_kernel,
        grid_spec=pltpu.PrefetchScalarGridSpec(
            num_scalar_prefetch=1,
            grid=(n_tok // tm,),
            in_specs=[pl.BlockSpec((SUBLANES, tm), lambda i, ly: (0, i))],
            out_specs=pl.BlockSpec((SUBLANES, tm), lambda i, ly: (0, i)),
        ),
        out_shape=jax.ShapeDtypeStruct((SUBLANES, n_tok), jnp.int32),
        compiler_params=_cparams(("parallel",)),
        name="dest",
    )(layout, rt)


def _load_indices(idx_refs, smem_refs, sem, base, tm):
    copies = [pltpu.make_async_copy(src.at[pl.ds(base, tm)], dst, sem.at[n])
              for n, (src, dst) in enumerate(zip(idx_refs, smem_refs))]
    for cp in copies:
        cp.start()
    for cp in copies:
        cp.wait()


def _dispatch_kernel(pstart_ref, d0_ref, d1_ref, hn_ref, xs_ref,
                     d0_s, d1_s, zblk, isem, sem, zsem, *, tm, n_blocks):

    @pl.when(pl.program_id(0) == 0)
    def _():
        zblk[...] = jnp.zeros_like(zblk)

        def zero_copy(row0):
            return pltpu.make_async_copy(zblk, xs_ref.at[pl.ds(pl.multiple_of(row0, EXPERT_BLK), EXPERT_BLK)], zsem)

        def last_block(e, n):
            end = pstart_ref[N_EXPERTS + e]
            has = end > pstart_ref[e]

            @pl.when(has)
            def _():
                zero_copy(end - EXPERT_BLK).start()

            return n + has.astype(jnp.int32)

        def tail_block(j, n):
            zero_copy(j * EXPERT_BLK).start()
            return n + 1

        n = lax.fori_loop(0, N_EXPERTS, last_block, 0)
        n = lax.fori_loop(pstart_ref[2 * N_EXPERTS], n_blocks, tail_block, n)

        def drain(_, c):
            zero_copy(0).wait()
            return c

        lax.fori_loop(0, n, drain, 0)

    base = pl.multiple_of(pl.program_id(0) * tm, tm)
    _load_indices((d0_ref, d1_ref), (d0_s, d1_s), isem, base, tm)

    def row_copy(t, dest):
        return pltpu.make_async_copy(hn_ref.at[pl.ds(t, 1)], xs_ref.at[pl.ds(dest, 1)], sem)

    def issue(t, _):
        row_copy(t, d0_s[t]).start(priority=0)
        row_copy(t, d1_s[t]).start(priority=1)
        return 0

    lax.fori_loop(0, tm, issue, 0, unroll=ISSUE_UNROLL)

    for _ in range(2):
        pltpu.make_async_copy(hn_ref, xs_ref.at[pl.ds(0, tm)], sem).wait()


def _dispatch(layout, d0, d1, hn, cap):
    n_tok, d = hn.shape
    tm = min(TM_ROWS, n_tok)
    grid = (n_tok // tm,)
    any_spec = pl.BlockSpec(memory_space=pl.ANY)
    return pl.pallas_call(
        functools.partial(_dispatch_kernel, tm=tm, n_blocks=cap // EXPERT_BLK),
        grid_spec=pltpu.PrefetchScalarGridSpec(
            num_scalar_prefetch=1,
            grid=grid,
            in_specs=[any_spec, any_spec, pl.BlockSpec((tm, d), lambda i, ps: (i, 0))],
            out_specs=any_spec,
            scratch_shapes=[pltpu.SMEM((tm,), jnp.int32)] * 2
            + [pltpu.VMEM((EXPERT_BLK, d), F32), pltpu.SemaphoreType.DMA((2,)), pltpu.SemaphoreType.DMA,
               pltpu.SemaphoreType.DMA],
        ),
        out_shape=jax.ShapeDtypeStruct((cap, d), F32),
        compiler_params=_cparams(("arbitrary",)),
        name="dispatch",
    )(layout, d0, d1, hn)


def _experts_kernel(be_ref, nb_ref, xs_ref, w1_ref, w3_ref, w2_ref, y_ref, w1b, w3b, w2b):
    i = pl.program_id(0)
    used = i < nb_ref[0]

    @pl.when(used & ((i == 0) | (be_ref[i] != be_ref[jnp.maximum(i - 1, 0)])))
    def _():
        w1b[...] = w1_ref[0].astype(BF16)
        w3b[...] = w3_ref[0].astype(BF16)
        w2b[...] = w2_ref[0].astype(BF16)

    @pl.when(used)
    def _():
        xb = xs_ref[...].astype(BF16)
        a = jnp.dot(xb, w1b[...], preferred_element_type=F32)
        b = jnp.dot(xb, w3b[...], preferred_element_type=F32)
        hid = (a * jax.nn.sigmoid(a)) * b
        y_ref[...] = jnp.dot(hid.astype(BF16), w2b[...], preferred_element_type=F32)

    @pl.when(jnp.logical_not(used))
    def _():
        y_ref[...] = jnp.zeros_like(y_ref)


def _experts(blk_expert, n_blocks_used, xs, w1, w3, w2):
    cap, d = xs.shape
    n_blocks = cap // EXPERT_BLK
    rows = lambda i, be, nb: (jnp.minimum(i, nb[0] - 1), 0)
    wsel = lambda i, be, nb: (be[jnp.minimum(i, nb[0] - 1)], 0, 0)
    return pl.pallas_call(
        _experts_kernel,
        grid_spec=pltpu.PrefetchScalarGridSpec(
            num_scalar_prefetch=2,
            grid=(n_blocks,),
            in_specs=[pl.BlockSpec((EXPERT_BLK, d), rows),
                      pl.BlockSpec((1, d, EXPERT_FF), wsel),
                      pl.BlockSpec((1, d, EXPERT_FF), wsel),
                      pl.BlockSpec((1, EXPERT_FF, d), wsel)],
            out_specs=pl.BlockSpec((EXPERT_BLK, d), lambda i, be, nb: (i, 0)),
            scratch_shapes=[pltpu.VMEM((d, EXPERT_FF), BF16), pltpu.VMEM((d, EXPERT_FF), BF16),
                            pltpu.VMEM((EXPERT_FF, d), BF16)],
        ),
        out_shape=jax.ShapeDtypeStruct((cap, d), F32),
        compiler_params=_cparams(("arbitrary",)),
        name="experts",
    )(blk_expert, n_blocks_used, xs, w1, w3, w2)


def _combine_kernel(d0_ref, d1_ref, x1_ref, rc_ref, ys_ref, o_ref, d0_s, d1_s, ybuf, isem, sem, *, tm):
    base = pl.multiple_of(pl.program_id(0) * tm, tm)
    _load_indices((d0_ref, d1_ref), (d0_s, d1_s), isem, base, tm)

    def row_copy(slot, t, src):
        return pltpu.make_async_copy(ys_ref.at[pl.ds(src, 1)], ybuf.at[slot, pl.ds(t, 1)], sem)

    def issue(t, _):
        row_copy(0, t, d0_s[t]).start(priority=0)
        row_copy(1, t, d1_s[t]).start(priority=1)
        return 0

    lax.fori_loop(0, tm, issue, 0, unroll=ISSUE_UNROLL)

    for slot in range(2):
        pltpu.make_async_copy(ys_ref.at[pl.ds(0, tm)], ybuf.at[slot], sem).wait()

    rc = rc_ref[...]
    o_ref[...] = x1_ref[...] + (rc[:, 4:5] * ybuf[0] + rc[:, 5:6] * ybuf[1])


def _combine(d0, d1, x1, rc, ys):
    n_tok, d = x1.shape
    tm = min(TM_ROWS, n_tok)
    any_spec = pl.BlockSpec(memory_space=pl.ANY)
    return pl.pallas_call(
        functools.partial(_combine_kernel, tm=tm),
        grid=(n_tok // tm,),
        in_specs=[any_spec, any_spec,
                  pl.BlockSpec((tm, d), lambda i: (i, 0)),
                  pl.BlockSpec((tm, LANES), lambda i: (i, 0)),
                  any_spec],
        out_specs=pl.BlockSpec((tm, d), lambda i: (i, 0)),
        scratch_shapes=[pltpu.SMEM((tm,), jnp.int32)] * 2
        + [pltpu.VMEM((2, tm, d), F32), pltpu.SemaphoreType.DMA((2,)), pltpu.SemaphoreType.DMA],
        out_shape=jax.ShapeDtypeStruct((n_tok, d), F32),
        compiler_params=_cparams(("arbitrary",)),
        name="combine",
    )(d0, d1, x1, rc, ys)


def _pad_lanes(vec, width=LANES):
    return jnp.zeros((1, width), F32).at[0, :vec.shape[0]].set(vec.astype(F32))


def _layout_w_in(w_in):
    d = w_in.shape[0]
    gates = jnp.concatenate([w_in[:, 3584:3592], jnp.zeros((d, LANES - 2 * ML_HEADS), w_in.dtype)], axis=-1)
    w_all = jnp.concatenate([w_in[:, 512:1024], w_in[:, 1536:2560], w_in[:, 3072:3584], gates],
                            axis=-1).astype(BF16)
    w_t = jnp.concatenate([w_in[:, 0:512], w_in[:, 1024:1536], w_in[:, 2560:3072]], axis=-1).T.astype(BF16)
    return w_all, w_t


def kernel(x, attn_norm_g, w_in, da_q_norm_g, da_k_norm_g, da_lambda_q1, da_lambda_k1, da_lambda_q2, da_lambda_k2, da_out_norm_g, ml_conv_w, ml_conv_b, ml_i_bias, ml_f_bias, ml_out_norm_g, w_branch_da, w_branch_ml, w_gate, b_gate, w_out, ffn_norm_g, w_group, b_group, w_router, b_router, w1, w3, w2):
    b_sz, s_len, d = x.shape
    n_tok = b_sz * s_len
    ly = 0

    w_all, w_t = _layout_w_in(w_in[ly])
    g_attn = attn_norm_g[ly].reshape(1, d)
    gq_col = _pad_lanes(da_q_norm_g[ly]).reshape(LANES, 1)
    gk = _pad_lanes(da_k_norm_g[ly])
    lam_vecs = jnp.concatenate([_pad_lanes(v[ly]) for v in
                                (da_lambda_q1, da_lambda_k1, da_lambda_q2, da_lambda_k2)], axis=0)
    gate_b = _pad_lanes(jnp.concatenate([ml_i_bias[ly], ml_f_bias[ly]]))
    w_rt = jnp.concatenate([w_router[ly], w_group[ly],
                            jnp.zeros((d, LANES - N_EXPERTS - N_GROUPS), F32)], axis=-1).astype(BF16)
    b_rt = _pad_lanes(jnp.concatenate([b_router[ly], b_group[ly]]))

    qt, ka, vt, mqk, mvt, mo, gif = _in_proj(x, g_attn, w_all, w_t, gq_col, gk)
    y_da = _attn(lam_vecs, qt, ka, vt, da_out_norm_g[ly].reshape(DA_V_DIM, 1))
    y_ml = _mlstm(mqk, mvt, mo, gif, ml_conv_w[ly], ml_conv_b[ly].reshape(1, -1), gate_b,
                  ml_out_norm_g[ly].reshape(1, ML_WIDTH))

    x1, hn, rc, rt, counts = _merge(
        x.reshape(n_tok, d), y_da.reshape(n_tok, DA_WIDTH), y_ml.reshape(n_tok, ML_WIDTH), g_attn,
        w_gate[ly].astype(BF16), b_gate[ly].reshape(1, -1), w_branch_da[ly].astype(BF16),
        w_branch_ml[ly].astype(BF16), w_out[ly].astype(BF16), ffn_norm_g[ly].reshape(1, d), w_rt, b_rt)

    cnt = counts[0, :N_EXPERTS].astype(jnp.int32)
    padded = (cnt + EXPERT_BLK - 1) // EXPERT_BLK * EXPERT_BLK
    pend = jnp.cumsum(padded)
    pstart = (pend - padded).astype(jnp.int32)
    cap = 2 * n_tok + N_EXPERTS * EXPERT_BLK
    n_blocks = cap // EXPERT_BLK
    blk_row0 = jnp.arange(n_blocks, dtype=jnp.int32) * EXPERT_BLK
    blk_expert = jnp.minimum(jnp.sum((pend[None, :] <= blk_row0[:, None]).astype(jnp.int32), axis=1),
                             N_EXPERTS - 1)
    n_used = (pend[-1:] // EXPERT_BLK).astype(jnp.int32)
    layout = jnp.concatenate([pstart, pend.astype(jnp.int32), n_used])
    dest = _dest(layout, rt)
    d0, d1 = dest[0], dest[1]

    xs = _dispatch(layout, d0, d1, hn, cap)
    ys = _experts(blk_expert, n_used, xs, w1[ly], w3[ly], w2[ly])
    out = _combine(d0, d1, x1, rc, ys)
    return out.reshape(b_sz, s_len, d)
```

```python
import functools
import math

import jax
import jax.numpy as jnp
import numpy as np
from jax import lax
from jax.experimental import pallas as pl
from jax.experimental.pallas import tpu as pltpu

F32 = jnp.float32
BF16 = jnp.bfloat16

D_MODEL = 1024
DA_HEADS = 4
DA_HEAD_DIM = 64
DA_V_DIM = 128
DA_WIDTH = 512
ML_HEADS = 4
ML_HEAD_DIM = 128
ML_WIDTH = 512
CONV_WIDTH = 4
N_GROUPS = 4
EXPERTS_PER_GROUP = 8
N_EXPERTS = 32
EXPERT_FF = 512
NORM_EPS = 1e-6
LAM_INIT = 0.8 - 0.6 * math.exp(-0.3 * 0)

LANES = 128
SUBLANES = 8
VMEM_LIMIT = 56 * 1024 * 1024

SEG_K = 0
SEG_MQK = SEG_K + 512
SEG_MO = SEG_MQK + 1024
SEG_IF = SEG_MO + 512
W_ALL = SEG_IF + LANES

LOG2E = math.log2(math.e)
ALIBI_PIECES = 4
AUX0 = DA_HEAD_DIM
POS_PERIOD = 512
DV_AUG = DA_V_DIM + 16


def _bf16_round(val):
    bits = np.float32(val).view(np.uint32)
    bits = (bits + np.uint32(0x7FFF) + ((bits >> np.uint32(16)) & np.uint32(1))) & np.uint32(0xFFFF0000)
    return float(bits.view(np.float32))


def _alibi_pieces(hd):
    rest, pieces = 2.0 ** (-8.0 * (hd + 1) / DA_HEADS) * LOG2E, []
    for _ in range(ALIBI_PIECES):
        pieces.append(_bf16_round(rest))
        rest -= pieces[-1]
    return pieces

TM_PROJ = 512
TQ_ATTN = 512
ATTN_HEADS = 4
L_MLSTM = 256
TM_MERGE = 512
TM_ROWS = 1024
EXPERT_BLK = 512
ISSUE_UNROLL = 8


def _cparams(sem):
    return pltpu.CompilerParams(dimension_semantics=sem, vmem_limit_bytes=VMEM_LIMIT)


def _lane_iota(shape):
    return lax.broadcasted_iota(jnp.int32, shape, len(shape) - 1)


def _row_iota(shape):
    return lax.broadcasted_iota(jnp.int32, shape, len(shape) - 2)


def _in_proj_kernel(x_ref, g_ref, w_ref, wt_ref, gq_ref, gk_ref, aux_ref,
                    qt_ref, ka_ref, vt_ref, mqk_ref, mvt_ref, mo_ref, gif_ref, *, tm):
    x = x_ref[0]
    ms = jnp.mean(x * x, axis=-1, keepdims=True)
    h = ((x * lax.rsqrt(ms + NORM_EPS)) * g_ref[...]).astype(BF16)

    def seg(lo, width):
        return jnp.dot(h, w_ref[:, lo:lo + width], preferred_element_type=F32)

    inv_dh = 1.0 / DA_HEAD_DIM

    t_all = lax.dot_general(wt_ref[...], h, (((1,), (1,)), ((), ())), preferred_element_type=F32)
    dh = DA_HEAD_DIM
    for hd in range(DA_HEADS):
        aux_rows = jnp.broadcast_to(aux_ref[hd, dh:LANES, :], (LANES - dh, tm)).astype(BF16)
        for mp in range(2):
            off = (hd * 2 + mp) * dh
            q = t_all[off:off + dh, :]
            qn = (q * lax.rsqrt(jnp.sum(q * q, axis=0, keepdims=True) * inv_dh + NORM_EPS)) * gq_ref[0:dh, :]
            qt_ref[0, hd, mp * LANES:mp * LANES + dh, :] = (qn * (dh ** -0.5 * LOG2E)).astype(BF16)
            qt_ref[0, hd, mp * LANES + dh:(mp + 1) * LANES, :] = aux_rows
        v_lo = DA_HEADS * 2 * dh + hd * DA_V_DIM
        vt_ref[0, hd, 0:DA_V_DIM, :] = t_all[v_lo:v_lo + DA_V_DIM, :].astype(BF16)
        ones_row = (_row_iota((DV_AUG - DA_V_DIM, tm)) == 0).astype(BF16)
        vt_ref[0, hd, DA_V_DIM:DV_AUG, :] = ones_row

    lane = _lane_iota((tm, LANES))
    pos = pl.program_id(1) * tm + _row_iota((tm, LANES))
    pos_lo = (pos & 255).astype(F32)
    pos_hi = (pos & 256).astype(F32)
    is_aux = (lane >= AUX0) & (lane < AUX0 + 2 * ALIBI_PIECES)
    is_lo = is_aux & ((lane & 1) == 0)
    is_hi = is_aux & ((lane & 1) == 1)
    real = lane < dh
    k_all = seg(SEG_K, DA_HEADS * 2 * dh)
    for hd in range(DA_HEADS):
        k_pair = k_all[:, hd * LANES:(hd + 1) * LANES]
        for mp in range(2):
            k = jnp.where(real, k_pair if mp == 0 else pltpu.roll(k_pair, dh, axis=1), 0.0)
            kn = (k * lax.rsqrt(jnp.sum(k * k, axis=-1, keepdims=True) * inv_dh + NORM_EPS)) * gk_ref[...]
            kn = jnp.where(is_lo, pos_lo, jnp.where(is_hi, pos_hi, kn))
            ka_ref[0, hd, :, mp * LANES:(mp + 1) * LANES] = kn.astype(BF16)

    mqk_ref[0] = seg(SEG_MQK, 2 * ML_WIDTH)
    mvt_ref[0] = t_all[DA_HEADS * 2 * dh + DA_WIDTH:, :].astype(BF16)
    mo_ref[0] = seg(SEG_MO, ML_WIDTH)
    gif_ref[0] = seg(SEG_IF, LANES)


def _in_proj(x, g_attn, w_all, w_t, gq_col, gk):
    b_sz, s_len, d = x.shape
    tm = min(TM_PROJ, s_len)
    grid = (b_sz, s_len // tm)
    tok = lambda width: pl.BlockSpec((1, tm, width), lambda b, s: (b, s, 0))
    full = lambda shape: pl.BlockSpec(shape, lambda b, s: (0,) * len(shape))
    n_t = w_t.shape[0]
    aux = np.zeros((DA_HEADS, LANES, 1), np.float32)
    for hd in range(DA_HEADS):
        aux[hd, AUX0:AUX0 + 2 * ALIBI_PIECES, 0] = np.repeat(_alibi_pieces(hd), 2)
    return pl.pallas_call(
        functools.partial(_in_proj_kernel, tm=tm),
        grid=grid,
        in_specs=[tok(d), full((1, d)), full((d, W_ALL)), full((n_t, d)), full((LANES, 1)), full((1, LANES)),
                  full((DA_HEADS, LANES, 1))],
        out_specs=[pl.BlockSpec((1, DA_HEADS, 2 * LANES, tm), lambda b, s: (b, 0, 0, s)),
                   pl.BlockSpec((1, DA_HEADS, tm, 2 * LANES), lambda b, s: (b, 0, s, 0)),
                   pl.BlockSpec((1, DA_HEADS, DV_AUG, tm), lambda b, s: (b, 0, 0, s)),
                   tok(2 * ML_WIDTH), pl.BlockSpec((1, ML_WIDTH, tm), lambda b, s: (b, 0, s)), tok(ML_WIDTH),
                   tok(LANES)],
        out_shape=[
            jax.ShapeDtypeStruct((b_sz, DA_HEADS, 2 * LANES, s_len), BF16),
            jax.ShapeDtypeStruct((b_sz, DA_HEADS, s_len, 2 * LANES), BF16),
            jax.ShapeDtypeStruct((b_sz, DA_HEADS, DV_AUG, s_len), BF16),
            jax.ShapeDtypeStruct((b_sz, s_len, 2 * ML_WIDTH), F32),
            jax.ShapeDtypeStruct((b_sz, ML_WIDTH, s_len), BF16),
            jax.ShapeDtypeStruct((b_sz, s_len, ML_WIDTH), F32),
            jax.ShapeDtypeStruct((b_sz, s_len, LANES), F32),
        ],
        compiler_params=_cparams(("parallel", "parallel")),
        name="in_proj",
    )(x, g_attn, w_all, w_t, gq_col, gk, jnp.asarray(aux))


def _attn_kernel(lam_ref, qt_ref, ka_ref, vt_ref, go_ref, o_ref, acc_ref, *, tq):
    hp = pl.program_id(1)
    qi = pl.program_id(2)
    slopes = [sum(_alibi_pieces(h)) for h in range(DA_HEADS)]
    pair_slopes = []
    for hh in range(ATTN_HEADS):
        sel = slopes[hh]
        for step in range(1, DA_HEADS // ATTN_HEADS):
            sel = jnp.where(hp == step, slopes[step * ATTN_HEADS + hh], sel)
        pair_slopes.append(sel)
    chains = [(hh, mp) for hh in range(ATTN_HEADS) for mp in range(2)]
    qts = {(hh, mp): qt_ref[0, hh, mp * LANES:(mp + 1) * LANES, :] for hh, mp in chains}
    causal = _row_iota((tq, tq)) <= _lane_iota((tq, tq))
    acc_ref[...] = jnp.zeros_like(acc_ref)

    def tile(kv, carry, masked):
        start = pl.multiple_of(kv * tq, tq)
        base = ((jnp.zeros((1, tq), jnp.int32) + (start // POS_PERIOD) * POS_PERIOD) - qi * tq).astype(F32)
        scores = {}
        for hh in range(ATTN_HEADS):
            k = ka_ref[0, hh, pl.ds(start, tq), :]
            for mp in range(2):
                scores[hh, mp] = jnp.dot(k[:, mp * LANES:(mp + 1) * LANES], qts[hh, mp],
                                         preferred_element_type=F32)
        out = {}
        for hh in range(ATTN_HEADS):
            c = base * pair_slopes[hh]
            probs = []
            for mp in range(2):
                s = scores[hh, mp]
                if masked:
                    s = jnp.where(causal, s, -jnp.inf)
                m_old = carry[chains.index((hh, mp))]
                m_new = jnp.maximum(m_old, jnp.max(s, axis=0, keepdims=True) + c)
                probs.append((jnp.exp2(m_old - m_new), jnp.exp2(s - (m_new - c)).astype(BF16)))
                out[hh, mp] = m_new
            vt = vt_ref[0, hh, :, pl.ds(start, tq)]
            for mp in range(2):
                alpha, p = probs[mp]
                acc_ref[hh, mp] = alpha * acc_ref[hh, mp] + jnp.dot(vt, p, preferred_element_type=F32)
        return tuple(out[ch] for ch in chains)

    init = tuple(jnp.full((1, tq), -jnp.inf, F32) for _ in chains)
    carry = lax.fori_loop(0, qi, lambda kv, cr: tile(kv, cr, False), init)
    tile(qi, carry, True)

    lam_v = lam_ref[...]
    lam = (jnp.exp(jnp.sum(lam_v[0:1] * lam_v[1:2], axis=-1, keepdims=True))
           - jnp.exp(jnp.sum(lam_v[2:3] * lam_v[3:4], axis=-1, keepdims=True)) + LAM_INIT)
    for hh in range(ATTN_HEADS):
        l1 = acc_ref[hh, 0, DA_V_DIM:DA_V_DIM + 1, :]
        l2 = acc_ref[hh, 1, DA_V_DIM:DA_V_DIM + 1, :]
        o = acc_ref[hh, 0, 0:DA_V_DIM, :] / l1 - lam * (acc_ref[hh, 1, 0:DA_V_DIM, :] / l2)
        o = (o * lax.rsqrt(jnp.mean(o * o, axis=0, keepdims=True) + NORM_EPS)) * go_ref[...]
        o_ref[0, :, hh * DA_V_DIM:(hh + 1) * DA_V_DIM] = (o * (1.0 - LAM_INIT)).T.astype(BF16)


def _attn(lam_vecs, qt, ka, vt, g_out_col):
    b_sz, _, s_len, _ = ka.shape
    tq = min(TQ_ATTN, s_len)
    grid = (b_sz, DA_HEADS // ATTN_HEADS, s_len // tq)
    return pl.pallas_call(
        functools.partial(_attn_kernel, tq=tq),
        grid=grid,
        in_specs=[
            pl.BlockSpec((4, LANES), lambda b, h, i: (0, 0)),
            pl.BlockSpec((1, ATTN_HEADS, 2 * LANES, tq), lambda b, h, i: (b, h, 0, i)),
            pl.BlockSpec((1, ATTN_HEADS, s_len, 2 * LANES), lambda b, h, i: (b, h, 0, 0)),
            pl.BlockSpec((1, ATTN_HEADS, DV_AUG, s_len), lambda b, h, i: (b, h, 0, 0)),
            pl.BlockSpec((DA_V_DIM, 1), lambda b, h, i: (0, 0)),
        ],
        out_specs=pl.BlockSpec((1, tq, ATTN_HEADS * DA_V_DIM), lambda b, h, i: (b, i, h)),
        out_shape=jax.ShapeDtypeStruct((b_sz, s_len, DA_WIDTH), BF16),
        scratch_shapes=[pltpu.VMEM((ATTN_HEADS, 2, DV_AUG, tq), F32)],
        compiler_params=_cparams(("parallel", "parallel", "arbitrary")),
        name="attn",
    )(lam_vecs, qt, ka, vt, g_out_col)


def _log_sigmoid(x):
    return jnp.minimum(x, 0.0) - jnp.log1p(jnp.exp(-jnp.abs(x)))


def _mlstm_kernel(mqk_ref, mvt_ref, mo_ref, gif_ref, cw_ref, cb_ref, gb_ref, go_ref,
                  y_ref, buf_ref, c_ref, n_ref, m_ref, *, L):
    halo = SUBLANES

    @pl.when(pl.program_id(1) == 0)
    def _():
        buf_ref[0:halo, :] = jnp.zeros((halo, 2 * ML_WIDTH), F32)
        c_ref[...] = jnp.zeros_like(c_ref)
        n_ref[...] = jnp.zeros_like(n_ref)
        m_ref[...] = jnp.zeros_like(m_ref)

    buf_ref[halo:halo + L, :] = mqk_ref[0]
    conv = jnp.broadcast_to(cb_ref[...], (L, 2 * ML_WIDTH))
    for j in range(CONV_WIDTH):
        lo = halo - (CONV_WIDTH - 1) + j
        conv = conv + buf_ref[lo:lo + L, :] * cw_ref[j:j + 1, :]
    buf_ref[0:halo, :] = buf_ref[L:L + halo, :]
    qk = conv * jax.nn.sigmoid(conv)

    g = gif_ref[0] + gb_ref[...]
    lf = _log_sigmoid(g)
    lf_t = lf.T

    row = _row_iota((L, L))
    col = _lane_iota((L, L))
    upper = row <= col
    nt = (((1,), (1,)), ((), ()))

    for hd in range(ML_HEADS):
        sl = slice(hd * ML_HEAD_DIM, (hd + 1) * ML_HEAD_DIM)
        q = qk[:, sl] * (ML_HEAD_DIM ** -0.5)
        k = qk[:, ML_WIDTH + hd * ML_HEAD_DIM:ML_WIDTH + (hd + 1) * ML_HEAD_DIM]
        vt = mvt_ref[0, sl, :]
        i_col = g[:, hd:hd + 1]
        f_col = lf[:, ML_HEADS + hd:ML_HEADS + hd + 1]
        f_row = lf_t[ML_HEADS + hd:ML_HEADS + hd + 1, :]

        b_col = jnp.sum(jnp.where(col <= row, f_row, 0.0), axis=-1, keepdims=True)
        b_row = jnp.sum(jnp.where(upper, f_col, 0.0), axis=0, keepdims=True)
        b_last = b_row[:, L - 1:L]

        m_state = m_ref[hd:hd + 1, 0:1]
        n_state = n_ref[hd:hd + 1, :]
        ct_state = c_ref[hd]

        dmat = jnp.where(upper, b_row + (i_col - b_col), -jnp.inf)
        r = jnp.max(dmat, axis=0, keepdims=True)
        p = jnp.exp(dmat - r)
        qb = q.astype(BF16)
        kb = k.astype(BF16)
        sc = lax.dot_general(kb, qb, nt, preferred_element_type=F32) * p
        num_a = jnp.dot(vt, sc.astype(BF16), preferred_element_type=F32)
        den_a = jnp.sum(sc, axis=0, keepdims=True)

        inter = b_row + m_state
        m_row = jnp.maximum(r, inter)
        e_a = jnp.exp(r - m_row)
        e_b = jnp.exp(inter - m_row)
        q_c = lax.dot_general(ct_state.astype(BF16), qb, nt, preferred_element_type=F32)
        n_rows = jnp.broadcast_to(n_state, (SUBLANES, ML_HEAD_DIM)).astype(BF16)
        q_n = lax.dot_general(n_rows, qb, nt, preferred_element_type=F32)[0:1, :]
        num = e_a * num_a + e_b * q_c
        den = e_a * den_a + e_b * q_n
        hval = num / jnp.maximum(jnp.abs(den), jnp.exp(-m_row))

        w_col = b_last - b_col + i_col
        a = jnp.max(w_col, axis=0, keepdims=True)
        m_new = jnp.maximum(b_last + m_state, a)
        kw = k * jnp.exp(w_col - a)
        d_c = jnp.dot(vt, kw.astype(BF16), preferred_element_type=F32)
        d_n = jnp.sum(kw, axis=0, keepdims=True)
        decay = jnp.exp(b_last + m_state - m_new)
        gain = jnp.exp(a - m_new)
        c_ref[hd] = decay * ct_state + gain * d_c
        n_ref[hd:hd + 1, :] = decay * n_state + gain * d_n
        m_ref[hd:hd + 1, :] = jnp.broadcast_to(m_new, (1, LANES))

        hn_t = hval * lax.rsqrt(jnp.mean(hval * hval, axis=0, keepdims=True) + NORM_EPS)
        y_ref[0, :, sl] = (hn_t.T * go_ref[:, sl] * jax.nn.sigmoid(mo_ref[0, :, sl])).astype(BF16)


def _mlstm(mqk, mvt, mo, gif, conv_w, conv_b, gate_b, g_out):
    b_sz, s_len, _ = mqk.shape
    L = min(L_MLSTM, s_len)
    grid = (b_sz, s_len // L)
    tok = lambda width: pl.BlockSpec((1, L, width), lambda b, c: (b, c, 0))
    full = lambda shape: pl.BlockSpec(shape, lambda b, c: (0,) * len(shape))
    return pl.pallas_call(
        functools.partial(_mlstm_kernel, L=L),
        grid=grid,
        in_specs=[tok(2 * ML_WIDTH), pl.BlockSpec((1, ML_WIDTH, L), lambda b, c: (b, 0, c)), tok(ML_WIDTH),
                  tok(LANES),
                  full((CONV_WIDTH, 2 * ML_WIDTH)), full((1, 2 * ML_WIDTH)), full((1, LANES)),
                  full((1, ML_WIDTH))],
        out_specs=tok(ML_WIDTH),
        out_shape=jax.ShapeDtypeStruct((b_sz, s_len, ML_WIDTH), BF16),
        scratch_shapes=[
            pltpu.VMEM((L + SUBLANES, 2 * ML_WIDTH), F32),
            pltpu.VMEM((ML_HEADS, ML_HEAD_DIM, ML_HEAD_DIM), F32),
            pltpu.VMEM((SUBLANES, LANES), F32),
            pltpu.VMEM((SUBLANES, LANES), F32),
        ],
        compiler_params=_cparams(("parallel", "arbitrary")),
        name="mlstm",
    )(mqk, mvt, mo, gif, conv_w, conv_b, gate_b, g_out)


def _merge_kernel(x_ref, yda_ref, yml_ref, ga_ref, wg_ref, bg_ref, wda_ref, wml_ref, wo_ref,
                  gf_ref, wr_ref, br_ref,
                  x1_ref, hn_ref, rc_ref, rt_ref, cnt_ref, carry_ref, *, tm):
    @pl.when(pl.program_id(0) == 0)
    def _():
        carry_ref[...] = jnp.zeros_like(carry_ref)

    x = x_ref[...]
    h = ((x * lax.rsqrt(jnp.mean(x * x, axis=-1, keepdims=True) + NORM_EPS)) * ga_ref[...]).astype(BF16)
    gates = jax.nn.sigmoid(jnp.dot(h, wg_ref[...], preferred_element_type=F32) + bg_ref[...])
    a = jnp.dot(yda_ref[...], wda_ref[...], preferred_element_type=F32)
    c = jnp.dot(yml_ref[...], wml_ref[...], preferred_element_type=F32)
    mixed = gates[:, :D_MODEL] * a + gates[:, D_MODEL:] * c
    x1 = x + jnp.dot(mixed.astype(BF16), wo_ref[...], preferred_element_type=F32)
    x1_ref[...] = x1
    hn = (x1 * lax.rsqrt(jnp.mean(x1 * x1, axis=-1, keepdims=True) + NORM_EPS)) * gf_ref[...]
    hn_ref[...] = hn

    logits = jnp.dot(hn.astype(BF16), wr_ref[...], preferred_element_type=F32) + br_ref[...]
    lane = _lane_iota((tm, LANES))
    neg = -jnp.inf
    big = jnp.int32(LANES)

    gl = jnp.where((lane >= N_EXPERTS) & (lane < N_EXPERTS + N_GROUPS), logits, neg)
    gmax = jnp.max(gl, axis=-1, keepdims=True)
    gsum = jnp.sum(jnp.exp(gl - gmax), axis=-1, keepdims=True)
    g_top = 1.0 / gsum
    g_idx = jnp.min(jnp.where(gl == gmax, lane, big), axis=-1, keepdims=True) - N_EXPERTS

    el = jnp.where((lane < N_EXPERTS) & ((lane >> 3) == g_idx), logits, neg)
    emax = jnp.max(el, axis=-1, keepdims=True)
    esum = jnp.sum(jnp.exp(el - emax), axis=-1, keepdims=True)
    e0 = jnp.min(jnp.where(el == emax, lane, big), axis=-1, keepdims=True)
    el2 = jnp.where(lane == e0, neg, el)
    emax2 = jnp.max(el2, axis=-1, keepdims=True)
    e1 = jnp.min(jnp.where(el2 == emax2, lane, big), axis=-1, keepdims=True)
    p0 = 1.0 / esum
    p1 = jnp.exp(emax2 - emax) / esum
    tot = p0 + p1
    w0 = g_top * (p0 / tot)
    w1 = g_top * (p1 / tot)

    hit0 = lane == e0
    hit1 = lane == e1
    onehot = (hit0 | hit1).astype(F32)
    before = (_lane_iota((tm, tm)) < _row_iota((tm, tm))).astype(BF16)
    prefix = jnp.dot(before, onehot.astype(BF16), preferred_element_type=F32) + carry_ref[...]
    r0 = jnp.sum(jnp.where(hit0, prefix, 0.0), axis=-1, keepdims=True)
    r1 = jnp.sum(jnp.where(hit1, prefix, 0.0), axis=-1, keepdims=True)
    carry_ref[...] = carry_ref[...] + jnp.sum(onehot, axis=0, keepdims=True)
    cnt_ref[...] = carry_ref[...]

    rc = jnp.where(lane == 0, e0.astype(F32),
         jnp.where(lane == 1, e1.astype(F32),
         jnp.where(lane == 2, r0,
         jnp.where(lane == 3, r1,
         jnp.where(lane == 4, w0,
         jnp.where(lane == 5, w1, 0.0))))))
    rc_ref[...] = rc
    rt_ref[...] = rc.T[0:SUBLANES, :].astype(jnp.int32)


def _merge(x2, yda, yml, g_attn, w_gate, b_gate, w_da, w_ml, w_out, g_ffn, w_rt, b_rt):
    n_tok, d = x2.shape
    tm = min(TM_MERGE, n_tok)
    grid = (n_tok // tm,)
    tok = lambda width: pl.BlockSpec((tm, width), lambda i: (i, 0))
    full = lambda shape: pl.BlockSpec(shape, lambda i: (0,) * len(shape))
    return pl.pallas_call(
        functools.partial(_merge_kernel, tm=tm),
        grid=grid,
        in_specs=[tok(d), tok(DA_WIDTH), tok(ML_WIDTH), full((1, d)), full((d, 2 * d)), full((1, 2 * d)),
                  full((DA_WIDTH, d)), full((ML_WIDTH, d)), full((d, d)), full((1, d)),
                  full((d, LANES)), full((1, LANES))],
        out_specs=[tok(d), tok(d), tok(LANES), pl.BlockSpec((SUBLANES, tm), lambda i: (0, i)),
                   full((1, LANES))],
        out_shape=[
            jax.ShapeDtypeStruct((n_tok, d), F32),
            jax.ShapeDtypeStruct((n_tok, d), F32),
            jax.ShapeDtypeStruct((n_tok, LANES), F32),
            jax.ShapeDtypeStruct((SUBLANES, n_tok), jnp.int32),
            jax.ShapeDtypeStruct((1, LANES), F32),
        ],
        scratch_shapes=[pltpu.VMEM((1, LANES), F32)],
        compiler_params=_cparams(("arbitrary",)),
        name="merge",
    )(x2, yda, yml, g_attn, w_gate, b_gate, w_da, w_ml, w_out, g_ffn, w_rt, b_rt)


def _dest_kernel(layout_ref, rt_ref, o_ref):
    tile = rt_ref[...]
    first = jnp.zeros_like(tile)
    for ex in range(N_EXPERTS):
        first = jnp.where(tile == ex, layout_ref[ex], first)
    o_ref[...] = first + pltpu.roll(tile, SUBLANES - 2, axis=0)


def _dest(layout, rt):
    _, n_tok = rt.shape
    tm = min(TM_ROWS, n_tok)
    return pl.pallas_call(
        _dest_kernel,
        grid_spec=pltpu.PrefetchScalarGridSpec(
            num_scalar_prefetch=1,
            grid=(n_tok // tm,),
            in_specs=[pl.BlockSpec((SUBLANES, tm), lambda i, ly: (0, i))],
            out_specs=pl.BlockSpec((SUBLANES, tm), lambda i, ly: (0, i)),
        ),
        out_shape=jax.ShapeDtypeStruct((SUBLANES, n_tok), jnp.int32),
        compiler_params=_cparams(("parallel",)),
        name="dest",
    )(layout, rt)


def _load_indices(idx_refs, smem_refs, sem, base, tm):
    copies = [pltpu.make_async_copy(src.at[pl.ds(base, tm)], dst, sem.at[n])
              for n, (src, dst) in enumerate(zip(idx_refs, smem_refs))]
    for cp in copies:
        cp.start()
    for cp in copies:
        cp.wait()


def _dispatch_kernel(pstart_ref, d0_ref, d1_ref, hn_ref, xs_ref,
                     d0_s, d1_s, zblk, isem, sem, zsem, *, tm, n_blocks):

    @pl.when(pl.program_id(0) == 0)
    def _():
        zblk[...] = jnp.zeros_like(zblk)

        def zero_copy(row0):
            return pltpu.make_async_copy(zblk, xs_ref.at[pl.ds(pl.multiple_of(row0, EXPERT_BLK), EXPERT_BLK)], zsem)

        def last_block(e, n):
            end = pstart_ref[N_EXPERTS + e]
            has = end > pstart_ref[e]

            @pl.when(has)
            def _():
                zero_copy(end - EXPERT_BLK).start()

            return n + has.astype(jnp.int32)

        def tail_block(j, n):
            zero_copy(j * EXPERT_BLK).start()
            return n + 1

        n = lax.fori_loop(0, N_EXPERTS, last_block, 0)
        n = lax.fori_loop(pstart_ref[2 * N_EXPERTS], n_blocks, tail_block, n)

        def drain(_, c):
            zero_copy(0).wait()
            return c

        lax.fori_loop(0, n, drain, 0)

    base = pl.multiple_of(pl.program_id(0) * tm, tm)
    _load_indices((d0_ref, d1_ref), (d0_s, d1_s), isem, base, tm)

    def row_copy(t, dest):
        return pltpu.make_async_copy(hn_ref.at[pl.ds(t, 1)], xs_ref.at[pl.ds(dest, 1)], sem)

    def issue(t, _):
        row_copy(t, d0_s[t]).start(priority=0)
        row_copy(t, d1_s[t]).start(priority=1)
        return 0

    lax.fori_loop(0, tm, issue, 0, unroll=ISSUE_UNROLL)

    for _ in range(2):
        pltpu.make_async_copy(hn_ref, xs_ref.at[pl.ds(0, tm)], sem).wait()


def _dispatch(layout, d0, d1, hn, cap):
    n_tok, d = hn.shape
    tm = min(TM_ROWS, n_tok)
    grid = (n_tok // tm,)
    any_spec = pl.BlockSpec(memory_space=pl.ANY)
    return pl.pallas_call(
        functools.partial(_dispatch_kernel, tm=tm, n_blocks=cap // EXPERT_BLK),
        grid_spec=pltpu.PrefetchScalarGridSpec(
            num_scalar_prefetch=1,
            grid=grid,
            in_specs=[any_spec, any_spec, pl.BlockSpec((tm, d), lambda i, ps: (i, 0))],
            out_specs=any_spec,
            scratch_shapes=[pltpu.SMEM((tm,), jnp.int32)] * 2
            + [pltpu.VMEM((EXPERT_BLK, d), F32), pltpu.SemaphoreType.DMA((2,)), pltpu.SemaphoreType.DMA,
               pltpu.SemaphoreType.DMA],
        ),
        out_shape=jax.ShapeDtypeStruct((cap, d), F32),
        compiler_params=_cparams(("arbitrary",)),
        name="dispatch",
    )(layout, d0, d1, hn)


def _experts_kernel(be_ref, nb_ref, xs_ref, w1_ref, w3_ref, w2_ref, y_ref, w1b, w3b, w2b):
    i = pl.program_id(0)
    used = i < nb_ref[0]

    @pl.when(used & ((i == 0) | (be_ref[i] != be_ref[jnp.maximum(i - 1, 0)])))
    def _():
        w1b[...] = w1_ref[0].astype(BF16)
        w3b[...] = w3_ref[0].astype(BF16)
        w2b[...] = w2_ref[0].astype(BF16)

    @pl.when(used)
    def _():
        xb = xs_ref[...].astype(BF16)
        a = jnp.dot(xb, w1b[...], preferred_element_type=F32)
        b = jnp.dot(xb, w3b[...], preferred_element_type=F32)
        hid = (a * jax.nn.sigmoid(a)) * b
        y_ref[...] = jnp.dot(hid.astype(BF16), w2b[...], preferred_element_type=F32)

    @pl.when(jnp.logical_not(used))
    def _():
        y_ref[...] = jnp.zeros_like(y_ref)


def _experts(blk_expert, n_blocks_used, xs, w1, w3, w2):
    cap, d = xs.shape
    n_blocks = cap // EXPERT_BLK
    rows = lambda i, be, nb: (jnp.minimum(i, nb[0] - 1), 0)
    wsel = lambda i, be, nb: (be[jnp.minimum(i, nb[0] - 1)], 0, 0)
    return pl.pallas_call(
        _experts_kernel,
        grid_spec=pltpu.PrefetchScalarGridSpec(
            num_scalar_prefetch=2,
            grid=(n_blocks,),
            in_specs=[pl.BlockSpec((EXPERT_BLK, d), rows),
                      pl.BlockSpec((1, d, EXPERT_FF), wsel),
                      pl.BlockSpec((1, d, EXPERT_FF), wsel),
                      pl.BlockSpec((1, EXPERT_FF, d), wsel)],
            out_specs=pl.BlockSpec((EXPERT_BLK, d), lambda i, be, nb: (i, 0)),
            scratch_shapes=[pltpu.VMEM((d, EXPERT_FF), BF16), pltpu.VMEM((d, EXPERT_FF), BF16),
                            pltpu.VMEM((EXPERT_FF, d), BF16)],
        ),
        out_shape=jax.ShapeDtypeStruct((cap, d), F32),
        compiler_params=_cparams(("arbitrary",)),
        name="experts",
    )(blk_expert, n_blocks_used, xs, w1, w3, w2)


def _combine_kernel(d0_ref, d1_ref, x1_ref, rc_ref, ys_ref, o_ref, d0_s, d1_s, ybuf, isem, sem, *, tm):
    base = pl.multiple_of(pl.program_id(0) * tm, tm)
    _load_indices((d0_ref, d1_ref), (d0_s, d1_s), isem, base, tm)

    def row_copy(slot, t, src):
        return pltpu.make_async_copy(ys_ref.at[pl.ds(src, 1)], ybuf.at[slot, pl.ds(t, 1)], sem)

    def issue(t, _):
        row_copy(0, t, d0_s[t]).start(priority=0)
        row_copy(1, t, d1_s[t]).start(priority=1)
        return 0

    lax.fori_loop(0, tm, issue, 0, unroll=ISSUE_UNROLL)

    for slot in range(2):
        pltpu.make_async_copy(ys_ref.at[pl.ds(0, tm)], ybuf.at[slot], sem).wait()

    rc = rc_ref[...]
    o_ref[...] = x1_ref[...] + (rc[:, 4:5] * ybuf[0] + rc[:, 5:6] * ybuf[1])


def _combine(d0, d1, x1, rc, ys):
    n_tok, d = x1.shape
    tm = min(TM_ROWS, n_tok)
    any_spec = pl.BlockSpec(memory_space=pl.ANY)
    return pl.pallas_call(
        functools.partial(_combine_kernel, tm=tm),
        grid=(n_tok // tm,),
        in_specs=[any_spec, any_spec,
                  pl.BlockSpec((tm, d), lambda i: (i, 0)),
                  pl.BlockSpec((tm, LANES), lambda i: (i, 0)),
                  any_spec],
        out_specs=pl.BlockSpec((tm, d), lambda i: (i, 0)),
        scratch_shapes=[pltpu.SMEM((tm,), jnp.int32)] * 2
        + [pltpu.VMEM((2, tm, d), F32), pltpu.SemaphoreType.DMA((2,)), pltpu.SemaphoreType.DMA],
        out_shape=jax.ShapeDtypeStruct((n_tok, d), F32),
        compiler_params=_cparams(("arbitrary",)),
        name="combine",
    )(d0, d1, x1, rc, ys)


def _pad_lanes(vec, width=LANES):
    return jnp.zeros((1, width), F32).at[0, :vec.shape[0]].set(vec.astype(F32))


def _layout_w_in(w_in):
    d = w_in.shape[0]
    gates = jnp.concatenate([w_in[:, 3584:3592], jnp.zeros((d, LANES - 2 * ML_HEADS), w_in.dtype)], axis=-1)
    w_all = jnp.concatenate([w_in[:, 512:1024], w_in[:, 1536:2560], w_in[:, 3072:3584], gates],
                            axis=-1).astype(BF16)
    w_t = jnp.concatenate([w_in[:, 0:512], w_in[:, 1024:1536], w_in[:, 2560:3072]], axis=-1).T.astype(BF16)
    return w_all, w_t


def kernel(x, attn_norm_g, w_in, da_q_norm_g, da_k_norm_g, da_lambda_q1, da_lambda_k1, da_lambda_q2, da_lambda_k2, da_out_norm_g, ml_conv_w, ml_conv_b, ml_i_bias, ml_f_bias, ml_out_norm_g, w_branch_da, w_branch_ml, w_gate, b_gate, w_out, ffn_norm_g, w_group, b_group, w_router, b_router, w1, w3, w2):
    b_sz, s_len, d = x.shape
    n_tok = b_sz * s_len
    ly = 0

    w_all, w_t = _layout_w_in(w_in[ly])
    g_attn = attn_norm_g[ly].reshape(1, d)
    gq_col = _pad_lanes(da_q_norm_g[ly]).reshape(LANES, 1)
    gk = _pad_lanes(da_k_norm_g[ly])
    lam_vecs = jnp.concatenate([_pad_lanes(v[ly]) for v in
                                (da_lambda_q1, da_lambda_k1, da_lambda_q2, da_lambda_k2)], axis=0)
    gate_b = _pad_lanes(jnp.concatenate([ml_i_bias[ly], ml_f_bias[ly]]))
    w_rt = jnp.concatenate([w_router[ly], w_group[ly],
                            jnp.zeros((d, LANES - N_EXPERTS - N_GROUPS), F32)], axis=-1).astype(BF16)
    b_rt = _pad_lanes(jnp.concatenate([b_router[ly], b_group[ly]]))

    qt, ka, vt, mqk, mvt, mo, gif = _in_proj(x, g_attn, w_all, w_t, gq_col, gk)
    y_da = _attn(lam_vecs, qt, ka, vt, da_out_norm_g[ly].reshape(DA_V_DIM, 1))
    y_ml = _mlstm(mqk, mvt, mo, gif, ml_conv_w[ly], ml_conv_b[ly].reshape(1, -1), gate_b,
                  ml_out_norm_g[ly].reshape(1, ML_WIDTH))

    x1, hn, rc, rt, counts = _merge(
        x.reshape(n_tok, d), y_da.reshape(n_tok, DA_WIDTH), y_ml.reshape(n_tok, ML_WIDTH), g_attn,
        w_gate[ly].astype(BF16), b_gate[ly].reshape(1, -1), w_branch_da[ly].astype(BF16),
        w_branch_ml[ly].astype(BF16), w_out[ly].astype(BF16), ffn_norm_g[ly].reshape(1, d), w_rt, b_rt)

    cnt = counts[0, :N_EXPERTS].astype(jnp.int32)
    padded = (cnt + EXPERT_BLK - 1) // EXPERT_BLK * EXPERT_BLK
    pend = jnp.cumsum(padded)
    pstart = (pend - padded).astype(jnp.int32)
    cap = 2 * n_tok + N_EXPERTS * EXPERT_BLK
    n_blocks = cap // EXPERT_BLK
    blk_row0 = jnp.arange(n_blocks, dtype=jnp.int32) * EXPERT_BLK
    blk_expert = jnp.minimum(jnp.sum((pend[None, :] <= blk_row0[:, None]).astype(jnp.int32), axis=1),
                             N_EXPERTS - 1)
    n_used = (pend[-1:] // EXPERT_BLK).astype(jnp.int32)
    layout = jnp.concatenate([pstart, pend.astype(jnp.int32), n_used])
    dest = _dest(layout, rt)
    d0, d1 = dest[0], dest[1]

    xs = _dispatch(layout, d0, d1, hn, cap)
    ys = _experts(blk_expert, n_used, xs, w1[ly], w3[ly], w2[ly])
    out = _combine(d0, d1, x1, rc, ys)
    return out.reshape(b_sz, s_len, d)
```

```python
import functools
import math

import jax
import jax.numpy as jnp
import numpy as np
from jax import lax
from jax.experimental import pallas as pl
from jax.experimental.pallas import tpu as pltpu

F32 = jnp.float32
BF16 = jnp.bfloat16

D_MODEL = 1024
DA_HEADS = 4
DA_HEAD_DIM = 64
DA_V_DIM = 128
DA_WIDTH = 512
ML_HEADS = 4
ML_HEAD_DIM = 128
ML_WIDTH = 512
CONV_WIDTH = 4
N_GROUPS = 4
EXPERTS_PER_GROUP = 8
N_EXPERTS = 32
EXPERT_FF = 512
NORM_EPS = 1e-6
LAM_INIT = 0.8 - 0.6 * math.exp(-0.3 * 0)

LANES = 128
SUBLANES = 8
BF16_SUBLANES = 16
VMEM_LIMIT = 56 * 1024 * 1024

SEG_K = 0
SEG_MQK = SEG_K + 512
SEG_MO = SEG_MQK + 1024
SEG_IF = SEG_MO + 512
W_ALL = SEG_IF + LANES

LOG2E = math.log2(math.e)
ALIBI_PIECES = 4
AUX0 = DA_HEAD_DIM
POS_PERIOD = 512
POS_HALF = POS_PERIOD // 2
DV_AUG = DA_V_DIM + BF16_SUBLANES
GROUP_SHIFT = EXPERTS_PER_GROUP.bit_length() - 1


def _bf16_round(val):
    bits = np.float32(val).view(np.uint32)
    bits = (bits + np.uint32(0x7FFF) + ((bits >> np.uint32(16)) & np.uint32(1))) & np.uint32(0xFFFF0000)
    return float(bits.view(np.float32))


def _alibi_pieces(hd):
    rest, pieces = 2.0 ** (-8.0 * (hd + 1) / DA_HEADS) * LOG2E, []
    for _ in range(ALIBI_PIECES):
        pieces.append(_bf16_round(rest))
        rest -= pieces[-1]
    return pieces

TM_PROJ = 512
TQ_ATTN = 512
ATTN_HEADS = 4
L_MLSTM = 256
TM_MERGE = 512
TM_ROWS = 1024
EXPERT_BLK = 512
ISSUE_UNROLL = 8


def _cparams(sem):
    return pltpu.CompilerParams(dimension_semantics=sem, vmem_limit_bytes=VMEM_LIMIT)


def _lane_iota(shape):
    return lax.broadcasted_iota(jnp.int32, shape, len(shape) - 1)


def _row_iota(shape):
    return lax.broadcasted_iota(jnp.int32, shape, len(shape) - 2)


def _in_proj_kernel(x_ref, g_ref, w_ref, wt_ref, gq_ref, gk_ref, aux_ref,
                    qt_ref, ka_ref, vt_ref, mqk_ref, mvt_ref, mo_ref, gif_ref, *, tm):
    x = x_ref[0]
    ms = jnp.mean(x * x, axis=-1, keepdims=True)
    h = ((x * lax.rsqrt(ms + NORM_EPS)) * g_ref[...]).astype(BF16)

    def seg(lo, width):
        return jnp.dot(h, w_ref[:, lo:lo + width], preferred_element_type=F32)

    inv_dh = 1.0 / DA_HEAD_DIM

    t_all = lax.dot_general(wt_ref[...], h, (((1,), (1,)), ((), ())), preferred_element_type=F32)
    dh = DA_HEAD_DIM
    for hd in range(DA_HEADS):
        aux_rows = jnp.broadcast_to(aux_ref[hd, dh:LANES, :], (LANES - dh, tm)).astype(BF16)
        for mp in range(2):
            off = (hd * 2 + mp) * dh
            q = t_all[off:off + dh, :]
            qn = (q * lax.rsqrt(jnp.sum(q * q, axis=0, keepdims=True) * inv_dh + NORM_EPS)) * gq_ref[0:dh, :]
            qt_ref[0, hd, mp * LANES:mp * LANES + dh, :] = (qn * (dh ** -0.5 * LOG2E)).astype(BF16)
            qt_ref[0, hd, mp * LANES + dh:(mp + 1) * LANES, :] = aux_rows
        v_lo = DA_HEADS * 2 * dh + hd * DA_V_DIM
        vt_ref[0, hd, 0:DA_V_DIM, :] = t_all[v_lo:v_lo + DA_V_DIM, :].astype(BF16)
        ones_row = (_row_iota((DV_AUG - DA_V_DIM, tm)) == 0).astype(BF16)
        vt_ref[0, hd, DA_V_DIM:DV_AUG, :] = ones_row

    lane = _lane_iota((tm, LANES))
    pos = pl.program_id(1) * tm + _row_iota((tm, LANES))
    pos_lo = (pos & (POS_HALF - 1)).astype(F32)
    pos_hi = (pos & POS_HALF).astype(F32)
    is_aux = (lane >= AUX0) & (lane < AUX0 + 2 * ALIBI_PIECES)
    is_lo = is_aux & ((lane & 1) == 0)
    is_hi = is_aux & ((lane & 1) == 1)
    real = lane < dh
    k_all = seg(SEG_K, DA_HEADS * 2 * dh)
    for hd in range(DA_HEADS):
        k_pair = k_all[:, hd * LANES:(hd + 1) * LANES]
        for mp in range(2):
            k = jnp.where(real, k_pair if mp == 0 else pltpu.roll(k_pair, dh, axis=1), 0.0)
            kn = (k * lax.rsqrt(jnp.sum(k * k, axis=-1, keepdims=True) * inv_dh + NORM_EPS)) * gk_ref[...]
            kn = jnp.where(is_lo, pos_lo, jnp.where(is_hi, pos_hi, kn))
            ka_ref[0, hd, :, mp * LANES:(mp + 1) * LANES] = kn.astype(BF16)

    mqk_ref[0] = seg(SEG_MQK, 2 * ML_WIDTH)
    mvt_ref[0] = t_all[DA_HEADS * 2 * dh + DA_WIDTH:, :].astype(BF16)
    mo_ref[0] = seg(SEG_MO, ML_WIDTH)
    gif_ref[0] = seg(SEG_IF, LANES)


def _in_proj(x, g_attn, w_all, w_t, gq_col, gk):
    b_sz, s_len, d = x.shape
    tm = min(TM_PROJ, s_len)
    grid = (b_sz, s_len // tm)
    tok = lambda width: pl.BlockSpec((1, tm, width), lambda b, s: (b, s, 0))
    full = lambda shape: pl.BlockSpec(shape, lambda b, s: (0,) * len(shape))
    n_t = w_t.shape[0]
    aux = np.zeros((DA_HEADS, LANES, 1), np.float32)
    for hd in range(DA_HEADS):
        aux[hd, AUX0:AUX0 + 2 * ALIBI_PIECES, 0] = np.repeat(_alibi_pieces(hd), 2)
    return pl.pallas_call(
        functools.partial(_in_proj_kernel, tm=tm),
        grid=grid,
        in_specs=[tok(d), full((1, d)), full((d, W_ALL)), full((n_t, d)), full((LANES, 1)), full((1, LANES)),
                  full((DA_HEADS, LANES, 1))],
        out_specs=[pl.BlockSpec((1, DA_HEADS, 2 * LANES, tm), lambda b, s: (b, 0, 0, s)),
                   pl.BlockSpec((1, DA_HEADS, tm, 2 * LANES), lambda b, s: (b, 0, s, 0)),
                   pl.BlockSpec((1, DA_HEADS, DV_AUG, tm), lambda b, s: (b, 0, 0, s)),
                   tok(2 * ML_WIDTH), pl.BlockSpec((1, ML_WIDTH, tm), lambda b, s: (b, 0, s)), tok(ML_WIDTH),
                   tok(LANES)],
        out_shape=[
            jax.ShapeDtypeStruct((b_sz, DA_HEADS, 2 * LANES, s_len), BF16),
            jax.ShapeDtypeStruct((b_sz, DA_HEADS, s_len, 2 * LANES), BF16),
            jax.ShapeDtypeStruct((b_sz, DA_HEADS, DV_AUG, s_len), BF16),
            jax.ShapeDtypeStruct((b_sz, s_len, 2 * ML_WIDTH), F32),
            jax.ShapeDtypeStruct((b_sz, ML_WIDTH, s_len), BF16),
            jax.ShapeDtypeStruct((b_sz, s_len, ML_WIDTH), F32),
            jax.ShapeDtypeStruct((b_sz, s_len, LANES), F32),
        ],
        compiler_params=_cparams(("parallel", "parallel")),
        name="in_proj",
    )(x, g_attn, w_all, w_t, gq_col, gk, jnp.asarray(aux))


def _attn_kernel(lam_ref, qt_ref, ka_ref, vt_ref, go_ref, o_ref, acc_ref, *, tq):
    hp = pl.program_id(1)
    qi = pl.program_id(2)
    slopes = [sum(_alibi_pieces(h)) for h in range(DA_HEADS)]
    pair_slopes = []
    for hh in range(ATTN_HEADS):
        sel = slopes[hh]
        for step in range(1, DA_HEADS // ATTN_HEADS):
            sel = jnp.where(hp == step, slopes[step * ATTN_HEADS + hh], sel)
        pair_slopes.append(sel)
    chains = [(hh, mp) for hh in range(ATTN_HEADS) for mp in range(2)]
    qts = {(hh, mp): qt_ref[0, hh, mp * LANES:(mp + 1) * LANES, :] for hh, mp in chains}
    causal = _row_iota((tq, tq)) <= _lane_iota((tq, tq))
    acc_ref[...] = jnp.zeros_like(acc_ref)

    def tile(kv, carry, masked):
        start = pl.multiple_of(kv * tq, tq)
        base = ((jnp.zeros((1, tq), jnp.int32) + (start // POS_PERIOD) * POS_PERIOD) - qi * tq).astype(F32)
        scores = {}

        def issue_scores(hh):
            k = ka_ref[0, hh, pl.ds(start, tq), :]
            for mp in range(2):
                scores[hh, mp] = jnp.dot(k[:, mp * LANES:(mp + 1) * LANES], qts[hh, mp],
                                         preferred_element_type=F32)

        issue_scores(0)
        out = {}
        for hh in range(ATTN_HEADS):
            if hh + 1 < ATTN_HEADS:
                issue_scores(hh + 1)
            c = base * pair_slopes[hh]
            probs = []
            for mp in range(2):
                s = scores[hh, mp]
                if masked:
                    s = jnp.where(causal, s, -jnp.inf)
                m_old = carry[chains.index((hh, mp))]
                m_new = jnp.maximum(m_old, jnp.max(s, axis=0, keepdims=True) + c)
                probs.append((jnp.exp2(m_old - m_new), jnp.exp2(s - (m_new - c)).astype(BF16)))
                out[hh, mp] = m_new
            vt = vt_ref[0, hh, :, pl.ds(start, tq)]
            for mp in range(2):
                alpha, p = probs[mp]
                acc_ref[hh, mp] = alpha * acc_ref[hh, mp] + jnp.dot(vt, p, preferred_element_type=F32)
        return tuple(out[ch] for ch in chains)

    init = tuple(jnp.full((1, tq), -jnp.inf, F32) for _ in chains)
    carry = lax.fori_loop(0, qi, lambda kv, cr: tile(kv, cr, False), init)
    tile(qi, carry, True)

    lam_v = lam_ref[...]
    lam = (jnp.exp(jnp.sum(lam_v[0:1] * lam_v[1:2], axis=-1, keepdims=True))
           - jnp.exp(jnp.sum(lam_v[2:3] * lam_v[3:4], axis=-1, keepdims=True)) + LAM_INIT)
    for hh in range(ATTN_HEADS):
        l1 = acc_ref[hh, 0, DA_V_DIM:DA_V_DIM + 1, :]
        l2 = acc_ref[hh, 1, DA_V_DIM:DA_V_DIM + 1, :]
        o = acc_ref[hh, 0, 0:DA_V_DIM, :] / l1 - lam * (acc_ref[hh, 1, 0:DA_V_DIM, :] / l2)
        o = (o * lax.rsqrt(jnp.mean(o * o, axis=0, keepdims=True) + NORM_EPS)) * go_ref[...]
        o_ref[0, :, hh * DA_V_DIM:(hh + 1) * DA_V_DIM] = (o * (1.0 - LAM_INIT)).T.astype(BF16)


def _attn(lam_vecs, qt, ka, vt, g_out_col):
    b_sz, _, s_len, _ = ka.shape
    tq = min(TQ_ATTN, s_len)
    grid = (b_sz, DA_HEADS // ATTN_HEADS, s_len // tq)
    return pl.pallas_call(
        functools.partial(_attn_kernel, tq=tq),
        grid=grid,
        in_specs=[
            pl.BlockSpec((4, LANES), lambda b, h, i: (0, 0)),
            pl.BlockSpec((1, ATTN_HEADS, 2 * LANES, tq), lambda b, h, i: (b, h, 0, i)),
            pl.BlockSpec((1, ATTN_HEADS, s_len, 2 * LANES), lambda b, h, i: (b, h, 0, 0)),
            pl.BlockSpec((1, ATTN_HEADS, DV_AUG, s_len), lambda b, h, i: (b, h, 0, 0)),
            pl.BlockSpec((DA_V_DIM, 1), lambda b, h, i: (0, 0)),
        ],
        out_specs=pl.BlockSpec((1, tq, ATTN_HEADS * DA_V_DIM), lambda b, h, i: (b, i, h)),
        out_shape=jax.ShapeDtypeStruct((b_sz, s_len, DA_WIDTH), BF16),
        scratch_shapes=[pltpu.VMEM((ATTN_HEADS, 2, DV_AUG, tq), F32)],
        compiler_params=_cparams(("parallel", "parallel", "arbitrary")),
        name="attn",
    )(lam_vecs, qt, ka, vt, g_out_col)


def _log_sigmoid(x):
    return jnp.minimum(x, 0.0) - jnp.log1p(jnp.exp(-jnp.abs(x)))


def _mlstm_kernel(mqk_ref, mvt_ref, mo_ref, gif_ref, cw_ref, cb_ref, gb_ref, go_ref,
                  y_ref, buf_ref, c_ref, n_ref, m_ref, *, L):
    halo = SUBLANES

    @pl.when(pl.program_id(1) == 0)
    def _():
        buf_ref[0:halo, :] = jnp.zeros((halo, 2 * ML_WIDTH), F32)
        c_ref[...] = jnp.zeros_like(c_ref)
        n_ref[...] = jnp.zeros_like(n_ref)
        m_ref[...] = jnp.zeros_like(m_ref)

    buf_ref[halo:halo + L, :] = mqk_ref[0]
    conv = jnp.broadcast_to(cb_ref[...], (L, 2 * ML_WIDTH))
    for j in range(CONV_WIDTH):
        lo = halo - (CONV_WIDTH - 1) + j
        conv = conv + buf_ref[lo:lo + L, :] * cw_ref[j:j + 1, :]
    buf_ref[0:halo, :] = buf_ref[L:L + halo, :]
    qk = conv * jax.nn.sigmoid(conv)

    g = gif_ref[0] + gb_ref[...]
    lf = _log_sigmoid(g)
    lf_t = lf.T

    row = _row_iota((L, L))
    col = _lane_iota((L, L))
    upper = row <= col
    nt = (((1,), (1,)), ((), ()))

    for hd in range(ML_HEADS):
        sl = slice(hd * ML_HEAD_DIM, (hd + 1) * ML_HEAD_DIM)
        q = qk[:, sl] * (ML_HEAD_DIM ** -0.5)
        k = qk[:, ML_WIDTH + hd * ML_HEAD_DIM:ML_WIDTH + (hd + 1) * ML_HEAD_DIM]
        vt = mvt_ref[0, sl, :]
        i_col = g[:, hd:hd + 1]
        f_col = lf[:, ML_HEADS + hd:ML_HEADS + hd + 1]
        f_row = lf_t[ML_HEADS + hd:ML_HEADS + hd + 1, :]

        b_col = jnp.sum(jnp.where(col <= row, f_row, 0.0), axis=-1, keepdims=True)
        b_row = jnp.sum(jnp.where(upper, f_col, 0.0), axis=0, keepdims=True)
        b_last = b_row[:, L - 1:L]

        m_state = m_ref[hd:hd + 1, 0:1]
        n_state = n_ref[hd:hd + 1, :]
        ct_state = c_ref[hd]

        dmat = jnp.where(upper, b_row + (i_col - b_col), -jnp.inf)
        r = jnp.max(dmat, axis=0, keepdims=True)
        p = jnp.exp(dmat - r)
        qb = q.astype(BF16)
        kb = k.astype(BF16)
        sc = lax.dot_general(kb, qb, nt, preferred_element_type=F32) * p
        num_a = jnp.dot(vt, sc.astype(BF16), preferred_element_type=F32)
        den_a = jnp.sum(sc, axis=0, keepdims=True)

        inter = b_row + m_state
        m_row = jnp.maximum(r, inter)
        e_a = jnp.exp(r - m_row)
        e_b = jnp.exp(inter - m_row)
        q_c = lax.dot_general(ct_state.astype(BF16), qb, nt, preferred_element_type=F32)
        n_rows = jnp.broadcast_to(n_state, (SUBLANES, ML_HEAD_DIM)).astype(BF16)
        q_n = lax.dot_general(n_rows, qb, nt, preferred_element_type=F32)[0:1, :]
        num = e_a * num_a + e_b * q_c
        den = e_a * den_a + e_b * q_n
        hval = num / jnp.maximum(jnp.abs(den), jnp.exp(-m_row))

        w_col = b_last - b_col + i_col
        a = jnp.max(w_col, axis=0, keepdims=True)
        m_new = jnp.maximum(b_last + m_state, a)
        kw = k * jnp.exp(w_col - a)
        d_c = jnp.dot(vt, kw.astype(BF16), preferred_element_type=F32)
        d_n = jnp.sum(kw, axis=0, keepdims=True)
        decay = jnp.exp(b_last + m_state - m_new)
        gain = jnp.exp(a - m_new)
        c_ref[hd] = decay * ct_state + gain * d_c
        n_ref[hd:hd + 1, :] = decay * n_state + gain * d_n
        m_ref[hd:hd + 1, :] = jnp.broadcast_to(m_new, (1, LANES))

        hn_t = hval * lax.rsqrt(jnp.mean(hval * hval, axis=0, keepdims=True) + NORM_EPS)
        y_ref[0, :, sl] = (hn_t.T * go_ref[:, sl] * jax.nn.sigmoid(mo_ref[0, :, sl])).astype(BF16)


def _mlstm(mqk, mvt, mo, gif, conv_w, conv_b, gate_b, g_out):
    b_sz, s_len, _ = mqk.shape
    L = min(L_MLSTM, s_len)
    grid = (b_sz, s_len // L)
    tok = lambda width: pl.BlockSpec((1, L, width), lambda b, c: (b, c, 0))
    full = lambda shape: pl.BlockSpec(shape, lambda b, c: (0,) * len(shape))
    return pl.pallas_call(
        functools.partial(_mlstm_kernel, L=L),
        grid=grid,
        in_specs=[tok(2 * ML_WIDTH), pl.BlockSpec((1, ML_WIDTH, L), lambda b, c: (b, 0, c)), tok(ML_WIDTH),
                  tok(LANES),
                  full((CONV_WIDTH, 2 * ML_WIDTH)), full((1, 2 * ML_WIDTH)), full((1, LANES)),
                  full((1, ML_WIDTH))],
        out_specs=tok(ML_WIDTH),
        out_shape=jax.ShapeDtypeStruct((b_sz, s_len, ML_WIDTH), BF16),
        scratch_shapes=[
            pltpu.VMEM((L + SUBLANES, 2 * ML_WIDTH), F32),
            pltpu.VMEM((ML_HEADS, ML_HEAD_DIM, ML_HEAD_DIM), F32),
            pltpu.VMEM((SUBLANES, LANES), F32),
            pltpu.VMEM((SUBLANES, LANES), F32),
        ],
        compiler_params=_cparams(("parallel", "arbitrary")),
        name="mlstm",
    )(mqk, mvt, mo, gif, conv_w, conv_b, gate_b, g_out)


def _merge_kernel(x_ref, yda_ref, yml_ref, ga_ref, wg_ref, bg_ref, wda_ref, wml_ref, wo_ref,
                  gf_ref, wr_ref, br_ref,
                  x1_ref, hn_ref, rc_ref, rt_ref, cnt_ref, carry_ref, *, tm):
    @pl.when(pl.program_id(0) == 0)
    def _():
        carry_ref[...] = jnp.zeros_like(carry_ref)

    x = x_ref[...]
    h = ((x * lax.rsqrt(jnp.mean(x * x, axis=-1, keepdims=True) + NORM_EPS)) * ga_ref[...]).astype(BF16)
    gates = jax.nn.sigmoid(jnp.dot(h, wg_ref[...], preferred_element_type=F32) + bg_ref[...])
    a = jnp.dot(yda_ref[...], wda_ref[...], preferred_element_type=F32)
    c = jnp.dot(yml_ref[...], wml_ref[...], preferred_element_type=F32)
    mixed = gates[:, :D_MODEL] * a + gates[:, D_MODEL:] * c
    x1 = x + jnp.dot(mixed.astype(BF16), wo_ref[...], preferred_element_type=F32)
    x1_ref[...] = x1
    hn = (x1 * lax.rsqrt(jnp.mean(x1 * x1, axis=-1, keepdims=True) + NORM_EPS)) * gf_ref[...]
    hn_ref[...] = hn

    logits = jnp.dot(hn.astype(BF16), wr_ref[...], preferred_element_type=F32) + br_ref[...]
    lane = _lane_iota((tm, LANES))
    neg = -jnp.inf
    big = jnp.int32(LANES)

    gl = jnp.where((lane >= N_EXPERTS) & (lane < N_EXPERTS + N_GROUPS), logits, neg)
    gmax = jnp.max(gl, axis=-1, keepdims=True)
    gsum = jnp.sum(jnp.exp(gl - gmax), axis=-1, keepdims=True)
    g_top = 1.0 / gsum
    g_idx = jnp.min(jnp.where(gl == gmax, lane, big), axis=-1, keepdims=True) - N_EXPERTS

    el = jnp.where((lane < N_EXPERTS) & ((lane >> GROUP_SHIFT) == g_idx), logits, neg)
    emax = jnp.max(el, axis=-1, keepdims=True)
    esum = jnp.sum(jnp.exp(el - emax), axis=-1, keepdims=True)
    e0 = jnp.min(jnp.where(el == emax, lane, big), axis=-1, keepdims=True)
    el2 = jnp.where(lane == e0, neg, el)
    emax2 = jnp.max(el2, axis=-1, keepdims=True)
    e1 = jnp.min(jnp.where(el2 == emax2, lane, big), axis=-1, keepdims=True)
    p0 = 1.0 / esum
    p1 = jnp.exp(emax2 - emax) / esum
    tot = p0 + p1
    w0 = g_top * (p0 / tot)
    w1 = g_top * (p1 / tot)

    hit0 = lane == e0
    hit1 = lane == e1
    onehot = (hit0 | hit1).astype(F32)
    before = (_lane_iota((tm, tm)) < _row_iota((tm, tm))).astype(BF16)
    prefix = jnp.dot(before, onehot.astype(BF16), preferred_element_type=F32) + carry_ref[...]
    r0 = jnp.sum(jnp.where(hit0, prefix, 0.0), axis=-1, keepdims=True)
    r1 = jnp.sum(jnp.where(hit1, prefix, 0.0), axis=-1, keepdims=True)
    carry_ref[...] = carry_ref[...] + jnp.sum(onehot, axis=0, keepdims=True)
    cnt_ref[...] = carry_ref[...]

    rc = jnp.where(lane == 0, e0.astype(F32),
         jnp.where(lane == 1, e1.astype(F32),
         jnp.where(lane == 2, r0,
         jnp.where(lane == 3, r1,
         jnp.where(lane == 4, w0,
         jnp.where(lane == 5, w1, 0.0))))))
    rc_ref[...] = rc
    rt_ref[...] = rc.T[0:SUBLANES, :].astype(jnp.int32)


def _merge(x2, yda, yml, g_attn, w_gate, b_gate, w_da, w_ml, w_out, g_ffn, w_rt, b_rt):
    n_tok, d = x2.shape
    tm = min(TM_MERGE, n_tok)
    grid = (n_tok // tm,)
    tok = lambda width: pl.BlockSpec((tm, width), lambda i: (i, 0))
    full = lambda shape: pl.BlockSpec(shape, lambda i: (0,) * len(shape))
    return pl.pallas_call(
        functools.partial(_merge_kernel, tm=tm),
        grid=grid,
        in_specs=[tok(d), tok(DA_WIDTH), tok(ML_WIDTH), full((1, d)), full((d, 2 * d)), full((1, 2 * d)),
                  full((DA_WIDTH, d)), full((ML_WIDTH, d)), full((d, d)), full((1, d)),
                  full((d, LANES)), full((1, LANES))],
        out_specs=[tok(d), tok(d), tok(LANES), pl.BlockSpec((SUBLANES, tm), lambda i: (0, i)),
                   full((1, LANES))],
        out_shape=[
            jax.ShapeDtypeStruct((n_tok, d), F32),
            jax.ShapeDtypeStruct((n_tok, d), F32),
            jax.ShapeDtypeStruct((n_tok, LANES), F32),
            jax.ShapeDtypeStruct((SUBLANES, n_tok), jnp.int32),
            jax.ShapeDtypeStruct((1, LANES), F32),
        ],
        scratch_shapes=[pltpu.VMEM((1, LANES), F32)],
        compiler_params=_cparams(("arbitrary",)),
        name="merge",
    )(x2, yda, yml, g_attn, w_gate, b_gate, w_da, w_ml, w_out, g_ffn, w_rt, b_rt)


def _dest_kernel(layout_ref, rt_ref, o_ref):
    tile = rt_ref[...]
    first = jnp.zeros_like(tile)
    for ex in range(N_EXPERTS):
        first = jnp.where(tile == ex, layout_ref[ex], first)
    o_ref[...] = first + pltpu.roll(tile, SUBLANES - 2, axis=0)


def _dest(layout, rt):
    _, n_tok = rt.shape
    tm = min(TM_ROWS, n_tok)
    return pl.pallas_call(
        _dest_kernel,
        grid_spec=pltpu.PrefetchScalarGridSpec(
            num_scalar_prefetch=1,
            grid=(n_tok // tm,),
            in_specs=[pl.BlockSpec((SUBLANES, tm), lambda i, ly: (0, i))],
            out_specs=pl.BlockSpec((SUBLANES, tm), lambda i, ly: (0, i)),
        ),
        out_shape=jax.ShapeDtypeStruct((SUBLANES, n_tok), jnp.int32),
        compiler_params=_cparams(("parallel",)),
        name="dest",
    )(layout, rt)


def _load_indices(idx_refs, smem_refs, sem, base, tm):
    copies = [pltpu.make_async_copy(src.at[pl.ds(base, tm)], dst, sem.at[n])
              for n, (src, dst) in enumerate(zip(idx_refs, smem_refs))]
    for cp in copies:
        cp.start()
    for cp in copies:
        cp.wait()


def _dispatch_kernel(pstart_ref, d0_ref, d1_ref, hn_ref, xs_ref,
                     d0_s, d1_s, zblk, isem, sem, zsem, *, tm, n_blocks):

    @pl.when(pl.program_id(0) == 0)
    def _():
        zblk[...] = jnp.zeros_like(zblk)

        def zero_copy(row0):
            return pltpu.make_async_copy(zblk, xs_ref.at[pl.ds(pl.multiple_of(row0, EXPERT_BLK), EXPERT_BLK)], zsem)

        def last_block(e, n):
            end = pstart_ref[N_EXPERTS + e]
            has = end > pstart_ref[e]

            @pl.when(has)
            def _():
                zero_copy(end - EXPERT_BLK).start()

            return n + has.astype(jnp.int32)

        def tail_block(j, n):
            zero_copy(j * EXPERT_BLK).start()
            return n + 1

        n = lax.fori_loop(0, N_EXPERTS, last_block, 0)
        n = lax.fori_loop(pstart_ref[2 * N_EXPERTS], n_blocks, tail_block, n)

        def drain(_, c):
            zero_copy(0).wait()
            return c

        lax.fori_loop(0, n, drain, 0)

    base = pl.multiple_of(pl.program_id(0) * tm, tm)
    _load_indices((d0_ref, d1_ref), (d0_s, d1_s), isem, base, tm)

    def row_copy(t, dest):
        return pltpu.make_async_copy(hn_ref.at[pl.ds(t, 1)], xs_ref.at[pl.ds(dest, 1)], sem)

    def issue(t, _):
        row_copy(t, d0_s[t]).start(priority=0)
        row_copy(t, d1_s[t]).start(priority=1)
        return 0

    lax.fori_loop(0, tm, issue, 0, unroll=ISSUE_UNROLL)

    for _ in range(2):
        pltpu.make_async_copy(hn_ref, xs_ref.at[pl.ds(0, tm)], sem).wait()


def _dispatch(layout, d0, d1, hn, cap):
    n_tok, d = hn.shape
    tm = min(TM_ROWS, n_tok)
    grid = (n_tok // tm,)
    any_spec = pl.BlockSpec(memory_space=pl.ANY)
    return pl.pallas_call(
        functools.partial(_dispatch_kernel, tm=tm, n_blocks=cap // EXPERT_BLK),
        grid_spec=pltpu.PrefetchScalarGridSpec(
            num_scalar_prefetch=1,
            grid=grid,
            in_specs=[any_spec, any_spec, pl.BlockSpec((tm, d), lambda i, ps: (i, 0))],
            out_specs=any_spec,
            scratch_shapes=[pltpu.SMEM((tm,), jnp.int32)] * 2
            + [pltpu.VMEM((EXPERT_BLK, d), F32), pltpu.SemaphoreType.DMA((2,)), pltpu.SemaphoreType.DMA,
               pltpu.SemaphoreType.DMA],
        ),
        out_shape=jax.ShapeDtypeStruct((cap, d), F32),
        compiler_params=_cparams(("arbitrary",)),
        name="dispatch",
    )(layout, d0, d1, hn)


def _experts_kernel(be_ref, nb_ref, xs_ref, w1_ref, w3_ref, w2_ref, y_ref, w1b, w3b, w2b):
    i = pl.program_id(0)
    used = i < nb_ref[0]

    @pl.when(used & ((i == 0) | (be_ref[i] != be_ref[jnp.maximum(i - 1, 0)])))
    def _():
        w1b[...] = w1_ref[0].astype(BF16)
        w3b[...] = w3_ref[0].astype(BF16)
        w2b[...] = w2_ref[0].astype(BF16)

    @pl.when(used)
    def _():
        xb = xs_ref[...].astype(BF16)
        a = jnp.dot(xb, w1b[...], preferred_element_type=F32)
        b = jnp.dot(xb, w3b[...], preferred_element_type=F32)
        hid = (a * jax.nn.sigmoid(a)) * b
        y_ref[...] = jnp.dot(hid.astype(BF16), w2b[...], preferred_element_type=F32)

    @pl.when(jnp.logical_not(used))
    def _():
        y_ref[...] = jnp.zeros_like(y_ref)


def _experts(blk_expert, n_blocks_used, xs, w1, w3, w2):
    cap, d = xs.shape
    n_blocks = cap // EXPERT_BLK
    rows = lambda i, be, nb: (jnp.minimum(i, nb[0] - 1), 0)
    wsel = lambda i, be, nb: (be[jnp.minimum(i, nb[0] - 1)], 0, 0)
    return pl.pallas_call(
        _experts_kernel,
        grid_spec=pltpu.PrefetchScalarGridSpec(
            num_scalar_prefetch=2,
            grid=(n_blocks,),
            in_specs=[pl.BlockSpec((EXPERT_BLK, d), rows),
                      pl.BlockSpec((1, d, EXPERT_FF), wsel),
                      pl.BlockSpec((1, d, EXPERT_FF), wsel),
                      pl.BlockSpec((1, EXPERT_FF, d), wsel)],
            out_specs=pl.BlockSpec((EXPERT_BLK, d), lambda i, be, nb: (i, 0)),
            scratch_shapes=[pltpu.VMEM((d, EXPERT_FF), BF16), pltpu.VMEM((d, EXPERT_FF), BF16),
                            pltpu.VMEM((EXPERT_FF, d), BF16)],
        ),
        out_shape=jax.ShapeDtypeStruct((cap, d), F32),
        compiler_params=_cparams(("arbitrary",)),
        name="experts",
    )(blk_expert, n_blocks_used, xs, w1, w3, w2)


def _combine_kernel(d0_ref, d1_ref, x1_ref, rc_ref, ys_ref, o_ref, d0_s, d1_s, ybuf, isem, sem, *, tm):
    base = pl.multiple_of(pl.program_id(0) * tm, tm)
    _load_indices((d0_ref, d1_ref), (d0_s, d1_s), isem, base, tm)

    def row_copy(slot, t, src):
        return pltpu.make_async_copy(ys_ref.at[pl.ds(src, 1)], ybuf.at[slot, pl.ds(t, 1)], sem)

    def issue(t, _):
        row_copy(0, t, d0_s[t]).start(priority=0)
        row_copy(1, t, d1_s[t]).start(priority=1)
        return 0

    lax.fori_loop(0, tm, issue, 0, unroll=ISSUE_UNROLL)

    for slot in range(2):
        pltpu.make_async_copy(ys_ref.at[pl.ds(0, tm)], ybuf.at[slot], sem).wait()

    rc = rc_ref[...]
    o_ref[...] = x1_ref[...] + (rc[:, 4:5] * ybuf[0] + rc[:, 5:6] * ybuf[1])


def _combine(d0, d1, x1, rc, ys):
    n_tok, d = x1.shape
    tm = min(TM_ROWS, n_tok)
    any_spec = pl.BlockSpec(memory_space=pl.ANY)
    return pl.pallas_call(
        functools.partial(_combine_kernel, tm=tm),
        grid=(n_tok // tm,),
        in_specs=[any_spec, any_spec,
                  pl.BlockSpec((tm, d), lambda i: (i, 0)),
                  pl.BlockSpec((tm, LANES), lambda i: (i, 0)),
                  any_spec],
        out_specs=pl.BlockSpec((tm, d), lambda i: (i, 0)),
        scratch_shapes=[pltpu.SMEM((tm,), jnp.int32)] * 2
        + [pltpu.VMEM((2, tm, d), F32), pltpu.SemaphoreType.DMA((2,)), pltpu.SemaphoreType.DMA],
        out_shape=jax.ShapeDtypeStruct((n_tok, d), F32),
        compiler_params=_cparams(("arbitrary",)),
        name="combine",
    )(d0, d1, x1, rc, ys)


def _pad_lanes(vec, width=LANES):
    return jnp.zeros((1, width), F32).at[0, :vec.shape[0]].set(vec.astype(F32))


def _layout_w_in(w_in):
    d = w_in.shape[0]
    gates = jnp.concatenate([w_in[:, 3584:3592], jnp.zeros((d, LANES - 2 * ML_HEADS), w_in.dtype)], axis=-1)
    w_all = jnp.concatenate([w_in[:, 512:1024], w_in[:, 1536:2560], w_in[:, 3072:3584], gates],
                            axis=-1).astype(BF16)
    w_t = jnp.concatenate([w_in[:, 0:512], w_in[:, 1024:1536], w_in[:, 2560:3072]], axis=-1).T.astype(BF16)
    return w_all, w_t


def kernel(x, attn_norm_g, w_in, da_q_norm_g, da_k_norm_g, da_lambda_q1, da_lambda_k1, da_lambda_q2, da_lambda_k2, da_out_norm_g, ml_conv_w, ml_conv_b, ml_i_bias, ml_f_bias, ml_out_norm_g, w_branch_da, w_branch_ml, w_gate, b_gate, w_out, ffn_norm_g, w_group, b_group, w_router, b_router, w1, w3, w2):
    b_sz, s_len, d = x.shape
    n_tok = b_sz * s_len
    ly = 0

    w_all, w_t = _layout_w_in(w_in[ly])
    g_attn = attn_norm_g[ly].reshape(1, d)
    gq_col = _pad_lanes(da_q_norm_g[ly]).reshape(LANES, 1)
    gk = _pad_lanes(da_k_norm_g[ly])
    lam_vecs = jnp.concatenate([_pad_lanes(v[ly]) for v in
                                (da_lambda_q1, da_lambda_k1, da_lambda_q2, da_lambda_k2)], axis=0)
    gate_b = _pad_lanes(jnp.concatenate([ml_i_bias[ly], ml_f_bias[ly]]))
    w_rt = jnp.concatenate([w_router[ly], w_group[ly],
                            jnp.zeros((d, LANES - N_EXPERTS - N_GROUPS), F32)], axis=-1).astype(BF16)
    b_rt = _pad_lanes(jnp.concatenate([b_router[ly], b_group[ly]]))

    qt, ka, vt, mqk, mvt, mo, gif = _in_proj(x, g_attn, w_all, w_t, gq_col, gk)
    y_da = _attn(lam_vecs, qt, ka, vt, da_out_norm_g[ly].reshape(DA_V_DIM, 1))
    y_ml = _mlstm(mqk, mvt, mo, gif, ml_conv_w[ly], ml_conv_b[ly].reshape(1, -1), gate_b,
                  ml_out_norm_g[ly].reshape(1, ML_WIDTH))

    x1, hn, rc, rt, counts = _merge(
        x.reshape(n_tok, d), y_da.reshape(n_tok, DA_WIDTH), y_ml.reshape(n_tok, ML_WIDTH), g_attn,
        w_gate[ly].astype(BF16), b_gate[ly].reshape(1, -1), w_branch_da[ly].astype(BF16),
        w_branch_ml[ly].astype(BF16), w_out[ly].astype(BF16), ffn_norm_g[ly].reshape(1, d), w_rt, b_rt)

    cnt = counts[0, :N_EXPERTS].astype(jnp.int32)
    padded = (cnt + EXPERT_BLK - 1) // EXPERT_BLK * EXPERT_BLK
    pend = jnp.cumsum(padded)
    pstart = (pend - padded).astype(jnp.int32)
    cap = 2 * n_tok + N_EXPERTS * EXPERT_BLK
    n_blocks = cap // EXPERT_BLK
    blk_row0 = jnp.arange(n_blocks, dtype=jnp.int32) * EXPERT_BLK
    blk_expert = jnp.minimum(jnp.sum((pend[None, :] <= blk_row0[:, None]).astype(jnp.int32), axis=1),
                             N_EXPERTS - 1)
    n_used = (pend[-1:] // EXPERT_BLK).astype(jnp.int32)
    layout = jnp.concatenate([pstart, pend.astype(jnp.int32), n_used])
    dest = _dest(layout, rt)
    d0, d1 = dest[0], dest[1]

    xs = _dispatch(layout, d0, d1, hn, cap)
    ys = _experts(blk_expert, n_used, xs, w1[ly], w3[ly], w2[ly])
    out = _combine(d0, d1, x1, rc, ys)
    return out.reshape(b_sz, s_len, d)
```

```python
import functools
import math

import jax
import jax.numpy as jnp
import numpy as np
from jax import lax
from jax.experimental import pallas as pl
from jax.experimental.pallas import tpu as pltpu

F32 = jnp.float32
BF16 = jnp.bfloat16

D_MODEL = 1024
DA_HEADS = 4
DA_HEAD_DIM = 64
DA_V_DIM = 128
DA_WIDTH = 512
ML_HEADS = 4
ML_HEAD_DIM = 128
ML_WIDTH = 512
CONV_WIDTH = 4
N_GROUPS = 4
EXPERTS_PER_GROUP = 8
N_EXPERTS = 32
EXPERT_FF = 512
NORM_EPS = 1e-6
LAM_INIT = 0.8 - 0.6 * math.exp(-0.3 * 0)

LANES = 128
SUBLANES = 8
BF16_SUBLANES = 16
VMEM_LIMIT = 56 * 1024 * 1024

SEG_K = 0
SEG_MQK = SEG_K + 512
SEG_MO = SEG_MQK + 1024
SEG_IF = SEG_MO + 512
W_ALL = SEG_IF + LANES

LOG2E = math.log2(math.e)
ALIBI_PIECES = 4
AUX0 = DA_HEAD_DIM
POS_PERIOD = 512
POS_HALF = POS_PERIOD // 2
DV_AUG = DA_V_DIM + BF16_SUBLANES
GROUP_SHIFT = EXPERTS_PER_GROUP.bit_length() - 1


def _bf16_round(val):
    bits = np.float32(val).view(np.uint32)
    bits = (bits + np.uint32(0x7FFF) + ((bits >> np.uint32(16)) & np.uint32(1))) & np.uint32(0xFFFF0000)
    return float(bits.view(np.float32))


def _alibi_pieces(hd):
    rest, pieces = 2.0 ** (-8.0 * (hd + 1) / DA_HEADS) * LOG2E, []
    for _ in range(ALIBI_PIECES):
        pieces.append(_bf16_round(rest))
        rest -= pieces[-1]
    return pieces

TM_PROJ = 512
TQ_ATTN = 512
ATTN_HEADS = 4
L_MLSTM = 256
TM_MERGE = 512
TM_ROWS = 1024
EXPERT_BLK = 512
ISSUE_UNROLL = 8


def _cparams(sem):
    return pltpu.CompilerParams(dimension_semantics=sem, vmem_limit_bytes=VMEM_LIMIT)


def _lane_iota(shape):
    return lax.broadcasted_iota(jnp.int32, shape, len(shape) - 1)


def _row_iota(shape):
    return lax.broadcasted_iota(jnp.int32, shape, len(shape) - 2)


def _in_proj_kernel(x_ref, g_ref, w_ref, wt_ref, gq_ref, gk_ref, aux_ref,
                    qt_ref, ka_ref, vt_ref, mqk_ref, mvt_ref, mo_ref, gif_ref, *, tm):
    x = x_ref[0]
    ms = jnp.mean(x * x, axis=-1, keepdims=True)
    h = ((x * lax.rsqrt(ms + NORM_EPS)) * g_ref[...]).astype(BF16)

    def seg(lo, width):
        return jnp.dot(h, w_ref[:, lo:lo + width], preferred_element_type=F32)

    inv_dh = 1.0 / DA_HEAD_DIM

    t_all = lax.dot_general(wt_ref[...], h, (((1,), (1,)), ((), ())), preferred_element_type=F32)
    dh = DA_HEAD_DIM
    for hd in range(DA_HEADS):
        aux_rows = jnp.broadcast_to(aux_ref[hd, dh:LANES, :], (LANES - dh, tm)).astype(BF16)
        for mp in range(2):
            off = (hd * 2 + mp) * dh
            q = t_all[off:off + dh, :]
            qn = (q * lax.rsqrt(jnp.sum(q * q, axis=0, keepdims=True) * inv_dh + NORM_EPS)) * gq_ref[0:dh, :]
            qt_ref[0, hd, mp * LANES:mp * LANES + dh, :] = (qn * (dh ** -0.5 * LOG2E)).astype(BF16)
            qt_ref[0, hd, mp * LANES + dh:(mp + 1) * LANES, :] = aux_rows
        v_lo = DA_HEADS * 2 * dh + hd * DA_V_DIM
        vt_ref[0, hd, 0:DA_V_DIM, :] = t_all[v_lo:v_lo + DA_V_DIM, :].astype(BF16)
        ones_row = (_row_iota((DV_AUG - DA_V_DIM, tm)) == 0).astype(BF16)
        vt_ref[0, hd, DA_V_DIM:DV_AUG, :] = ones_row

    lane = _lane_iota((tm, LANES))
    pos = pl.program_id(1) * tm + _row_iota((tm, LANES))
    pos_lo = (pos & (POS_HALF - 1)).astype(F32)
    pos_hi = (pos & POS_HALF).astype(F32)
    is_aux = (lane >= AUX0) & (lane < AUX0 + 2 * ALIBI_PIECES)
    is_lo = is_aux & ((lane & 1) == 0)
    is_hi = is_aux & ((lane & 1) == 1)
    real = lane < dh
    k_all = seg(SEG_K, DA_HEADS * 2 * dh)
    for hd in range(DA_HEADS):
        k_pair = k_all[:, hd * LANES:(hd + 1) * LANES]
        for mp in range(2):
            k = jnp.where(real, k_pair if mp == 0 else pltpu.roll(k_pair, dh, axis=1), 0.0)
            kn = (k * lax.rsqrt(jnp.sum(k * k, axis=-1, keepdims=True) * inv_dh + NORM_EPS)) * gk_ref[...]
            kn = jnp.where(is_lo, pos_lo, jnp.where(is_hi, pos_hi, kn))
            ka_ref[0, hd, :, mp * LANES:(mp + 1) * LANES] = kn.astype(BF16)

    mqk_ref[0] = seg(SEG_MQK, 2 * ML_WIDTH)
    mvt_ref[0] = t_all[DA_HEADS * 2 * dh + DA_WIDTH:, :].astype(BF16)
    mo_ref[0] = seg(SEG_MO, ML_WIDTH)
    gif_ref[0] = seg(SEG_IF, LANES)


def _in_proj(x, g_attn, w_all, w_t, gq_col, gk):
    b_sz, s_len, d = x.shape
    tm = min(TM_PROJ, s_len)
    grid = (b_sz, s_len // tm)
    tok = lambda width: pl.BlockSpec((1, tm, width), lambda b, s: (b, s, 0))
    full = lambda shape: pl.BlockSpec(shape, lambda b, s: (0,) * len(shape))
    n_t = w_t.shape[0]
    aux = np.zeros((DA_HEADS, LANES, 1), np.float32)
    for hd in range(DA_HEADS):
        aux[hd, AUX0:AUX0 + 2 * ALIBI_PIECES, 0] = np.repeat(_alibi_pieces(hd), 2)
    return pl.pallas_call(
        functools.partial(_in_proj_kernel, tm=tm),
        grid=grid,
        in_specs=[tok(d), full((1, d)), full((d, W_ALL)), full((n_t, d)), full((LANES, 1)), full((1, LANES)),
                  full((DA_HEADS, LANES, 1))],
        out_specs=[pl.BlockSpec((1, DA_HEADS, 2 * LANES, tm), lambda b, s: (b, 0, 0, s)),
                   pl.BlockSpec((1, DA_HEADS, tm, 2 * LANES), lambda b, s: (b, 0, s, 0)),
                   pl.BlockSpec((1, DA_HEADS, DV_AUG, tm), lambda b, s: (b, 0, 0, s)),
                   tok(2 * ML_WIDTH), pl.BlockSpec((1, ML_WIDTH, tm), lambda b, s: (b, 0, s)), tok(ML_WIDTH),
                   tok(LANES)],
        out_shape=[
            jax.ShapeDtypeStruct((b_sz, DA_HEADS, 2 * LANES, s_len), BF16),
            jax.ShapeDtypeStruct((b_sz, DA_HEADS, s_len, 2 * LANES), BF16),
            jax.ShapeDtypeStruct((b_sz, DA_HEADS, DV_AUG, s_len), BF16),
            jax.ShapeDtypeStruct((b_sz, s_len, 2 * ML_WIDTH), F32),
            jax.ShapeDtypeStruct((b_sz, ML_WIDTH, s_len), BF16),
            jax.ShapeDtypeStruct((b_sz, s_len, ML_WIDTH), F32),
            jax.ShapeDtypeStruct((b_sz, s_len, LANES), F32),
        ],
        compiler_params=_cparams(("parallel", "parallel")),
        name="in_proj",
    )(x, g_attn, w_all, w_t, gq_col, gk, jnp.asarray(aux))


def _attn_kernel(lam_ref, qt_ref, ka_ref, vt_ref, go_ref, o_ref, acc_ref, *, tq):
    hp = pl.program_id(1)
    qi = pl.program_id(2)
    slopes = [sum(_alibi_pieces(h)) for h in range(DA_HEADS)]
    pair_slopes = []
    for hh in range(ATTN_HEADS):
        sel = slopes[hh]
        for step in range(1, DA_HEADS // ATTN_HEADS):
            sel = jnp.where(hp == step, slopes[step * ATTN_HEADS + hh], sel)
        pair_slopes.append(sel)
    chains = [(hh, mp) for hh in range(ATTN_HEADS) for mp in range(2)]
    qts = {(hh, mp): qt_ref[0, hh, mp * LANES:(mp + 1) * LANES, :] for hh, mp in chains}
    causal = _row_iota((tq, tq)) <= _lane_iota((tq, tq))
    acc_ref[...] = jnp.zeros_like(acc_ref)

    def tile(kv, carry, masked):
        start = pl.multiple_of(kv * tq, tq)
        base = ((jnp.zeros((1, tq), jnp.int32) + (start // POS_PERIOD) * POS_PERIOD) - qi * tq).astype(F32)
        scores = {}

        def issue_scores(hh):
            k = ka_ref[0, hh, pl.ds(start, tq), :]
            for mp in range(2):
                scores[hh, mp] = jnp.dot(k[:, mp * LANES:(mp + 1) * LANES], qts[hh, mp],
                                         preferred_element_type=F32)

        issue_scores(0)
        out = {}
        for hh in range(ATTN_HEADS):
            if hh + 1 < ATTN_HEADS:
                issue_scores(hh + 1)
            c = base * pair_slopes[hh]
            probs = []
            for mp in range(2):
                s = scores[hh, mp]
                if masked:
                    s = jnp.where(causal, s, -jnp.inf)
                m_old = carry[chains.index((hh, mp))]
                m_new = jnp.maximum(m_old, jnp.max(s, axis=0, keepdims=True) + c)
                probs.append((jnp.exp2(m_old - m_new), jnp.exp2(s - (m_new - c)).astype(BF16)))
                out[hh, mp] = m_new
            vt = vt_ref[0, hh, :, pl.ds(start, tq)]
            for mp in range(2):
                alpha, p = probs[mp]
                acc_ref[hh, mp] = alpha * acc_ref[hh, mp] + jnp.dot(vt, p, preferred_element_type=F32)
        return tuple(out[ch] for ch in chains)

    init = tuple(jnp.full((1, tq), -jnp.inf, F32) for _ in chains)
    carry = lax.fori_loop(0, qi, lambda kv, cr: tile(kv, cr, False), init)
    tile(qi, carry, True)

    lam_v = lam_ref[...]
    lam = (jnp.exp(jnp.sum(lam_v[0:1] * lam_v[1:2], axis=-1, keepdims=True))
           - jnp.exp(jnp.sum(lam_v[2:3] * lam_v[3:4], axis=-1, keepdims=True)) + LAM_INIT)
    for hh in range(ATTN_HEADS):
        l1 = acc_ref[hh, 0, DA_V_DIM:DA_V_DIM + 1, :]
        l2 = acc_ref[hh, 1, DA_V_DIM:DA_V_DIM + 1, :]
        o = acc_ref[hh, 0, 0:DA_V_DIM, :] / l1 - lam * (acc_ref[hh, 1, 0:DA_V_DIM, :] / l2)
        o = (o * lax.rsqrt(jnp.mean(o * o, axis=0, keepdims=True) + NORM_EPS)) * go_ref[...]
        o_ref[0, :, hh * DA_V_DIM:(hh + 1) * DA_V_DIM] = (o * (1.0 - LAM_INIT)).T.astype(BF16)


def _attn(lam_vecs, qt, ka, vt, g_out_col):
    b_sz, _, s_len, _ = ka.shape
    tq = min(TQ_ATTN, s_len)
    grid = (b_sz, DA_HEADS // ATTN_HEADS, s_len // tq)
    return pl.pallas_call(
        functools.partial(_attn_kernel, tq=tq),
        grid=grid,
        in_specs=[
            pl.BlockSpec((4, LANES), lambda b, h, i: (0, 0)),
            pl.BlockSpec((1, ATTN_HEADS, 2 * LANES, tq), lambda b, h, i: (b, h, 0, i)),
            pl.BlockSpec((1, ATTN_HEADS, s_len, 2 * LANES), lambda b, h, i: (b, h, 0, 0)),
            pl.BlockSpec((1, ATTN_HEADS, DV_AUG, s_len), lambda b, h, i: (b, h, 0, 0)),
            pl.BlockSpec((DA_V_DIM, 1), lambda b, h, i: (0, 0)),
        ],
        out_specs=pl.BlockSpec((1, tq, ATTN_HEADS * DA_V_DIM), lambda b, h, i: (b, i, h)),
        out_shape=jax.ShapeDtypeStruct((b_sz, s_len, DA_WIDTH), BF16),
        scratch_shapes=[pltpu.VMEM((ATTN_HEADS, 2, DV_AUG, tq), F32)],
        compiler_params=_cparams(("parallel", "parallel", "arbitrary")),
        name="attn",
    )(lam_vecs, qt, ka, vt, g_out_col)


def _log_sigmoid(x):
    return jnp.minimum(x, 0.0) - jnp.log1p(jnp.exp(-jnp.abs(x)))


def _mlstm_kernel(mqk_ref, mvt_ref, mo_ref, gif_ref, cw_ref, cb_ref, gb_ref, go_ref,
                  y_ref, buf_ref, c_ref, n_ref, m_ref, *, L):
    halo = SUBLANES

    @pl.when(pl.program_id(1) == 0)
    def _():
        buf_ref[0:halo, :] = jnp.zeros((halo, 2 * ML_WIDTH), F32)
        c_ref[...] = jnp.zeros_like(c_ref)
        n_ref[...] = jnp.zeros_like(n_ref)
        m_ref[...] = jnp.zeros_like(m_ref)

    buf_ref[halo:halo + L, :] = mqk_ref[0]
    conv = jnp.broadcast_to(cb_ref[...], (L, 2 * ML_WIDTH))
    for j in range(CONV_WIDTH):
        lo = halo - (CONV_WIDTH - 1) + j
        conv = conv + buf_ref[lo:lo + L, :] * cw_ref[j:j + 1, :]
    buf_ref[0:halo, :] = buf_ref[L:L + halo, :]
    qk = conv * jax.nn.sigmoid(conv)

    g = gif_ref[0] + gb_ref[...]
    lf = _log_sigmoid(g)
    lf_t = lf.T

    row = _row_iota((L, L))
    col = _lane_iota((L, L))
    upper = row <= col
    nt = (((1,), (1,)), ((), ()))

    for hd in range(ML_HEADS):
        sl = slice(hd * ML_HEAD_DIM, (hd + 1) * ML_HEAD_DIM)
        q = qk[:, sl] * (ML_HEAD_DIM ** -0.5)
        k = qk[:, ML_WIDTH + hd * ML_HEAD_DIM:ML_WIDTH + (hd + 1) * ML_HEAD_DIM]
        vt = mvt_ref[0, sl, :]
        i_col = g[:, hd:hd + 1]
        f_col = lf[:, ML_HEADS + hd:ML_HEADS + hd + 1]
        f_row = lf_t[ML_HEADS + hd:ML_HEADS + hd + 1, :]

        b_col = jnp.sum(jnp.where(col <= row, f_row, 0.0), axis=-1, keepdims=True)
        b_row = jnp.sum(jnp.where(upper, f_col, 0.0), axis=0, keepdims=True)
        b_last = b_row[:, L - 1:L]

        m_state = m_ref[hd:hd + 1, 0:1]
        n_state = n_ref[hd:hd + 1, :]
        ct_state = c_ref[hd]

        dmat = jnp.where(upper, b_row + (i_col - b_col), -jnp.inf)
        r = jnp.max(dmat, axis=0, keepdims=True)
        p = jnp.exp(dmat - r)
        qb = q.astype(BF16)
        kb = k.astype(BF16)
        sc = lax.dot_general(kb, qb, nt, preferred_element_type=F32) * p
        num_a = jnp.dot(vt, sc.astype(BF16), preferred_element_type=F32)
        den_a = jnp.sum(sc, axis=0, keepdims=True)

        inter = b_row + m_state
        m_row = jnp.maximum(r, inter)
        e_a = jnp.exp(r - m_row)
        e_b = jnp.exp(inter - m_row)
        q_c = lax.dot_general(ct_state.astype(BF16), qb, nt, preferred_element_type=F32)
        n_rows = jnp.broadcast_to(n_state, (SUBLANES, ML_HEAD_DIM)).astype(BF16)
        q_n = lax.dot_general(n_rows, qb, nt, preferred_element_type=F32)[0:1, :]
        num = e_a * num_a + e_b * q_c
        den = e_a * den_a + e_b * q_n
        hval = num / jnp.maximum(jnp.abs(den), jnp.exp(-m_row))

        w_col = b_last - b_col + i_col
        a = jnp.max(w_col, axis=0, keepdims=True)
        m_new = jnp.maximum(b_last + m_state, a)
        kw = k * jnp.exp(w_col - a)
        d_c = jnp.dot(vt, kw.astype(BF16), preferred_element_type=F32)
        d_n = jnp.sum(kw, axis=0, keepdims=True)
        decay = jnp.exp(b_last + m_state - m_new)
        gain = jnp.exp(a - m_new)
        c_ref[hd] = decay * ct_state + gain * d_c
        n_ref[hd:hd + 1, :] = decay * n_state + gain * d_n
        m_ref[hd:hd + 1, :] = jnp.broadcast_to(m_new, (1, LANES))

        hn_t = hval * lax.rsqrt(jnp.mean(hval * hval, axis=0, keepdims=True) + NORM_EPS)
        y_ref[0, :, sl] = (hn_t.T * go_ref[:, sl] * jax.nn.sigmoid(mo_ref[0, :, sl])).astype(BF16)


def _mlstm(mqk, mvt, mo, gif, conv_w, conv_b, gate_b, g_out):
    b_sz, s_len, _ = mqk.shape
    L = min(L_MLSTM, s_len)
    grid = (b_sz, s_len // L)
    tok = lambda width: pl.BlockSpec((1, L, width), lambda b, c: (b, c, 0))
    full = lambda shape: pl.BlockSpec(shape, lambda b, c: (0,) * len(shape))
    return pl.pallas_call(
        functools.partial(_mlstm_kernel, L=L),
        grid=grid,
        in_specs=[tok(2 * ML_WIDTH), pl.BlockSpec((1, ML_WIDTH, L), lambda b, c: (b, 0, c)), tok(ML_WIDTH),
                  tok(LANES),
                  full((CONV_WIDTH, 2 * ML_WIDTH)), full((1, 2 * ML_WIDTH)), full((1, LANES)),
                  full((1, ML_WIDTH))],
        out_specs=tok(ML_WIDTH),
        out_shape=jax.ShapeDtypeStruct((b_sz, s_len, ML_WIDTH), BF16),
        scratch_shapes=[
            pltpu.VMEM((L + SUBLANES, 2 * ML_WIDTH), F32),
            pltpu.VMEM((ML_HEADS, ML_HEAD_DIM, ML_HEAD_DIM), F32),
            pltpu.VMEM((SUBLANES, LANES), F32),
            pltpu.VMEM((SUBLANES, LANES), F32),
        ],
        compiler_params=_cparams(("parallel", "arbitrary")),
        name="mlstm",
    )(mqk, mvt, mo, gif, conv_w, conv_b, gate_b, g_out)


def _merge_kernel(x_ref, yda_ref, yml_ref, ga_ref, wg_ref, bg_ref, wda_ref, wml_ref, wo_ref,
                  gf_ref, wr_ref, br_ref,
                  x1_ref, hn_ref, rc_ref, rt_ref, cnt_ref, carry_ref, *, tm):
    @pl.when(pl.program_id(0) == 0)
    def _():
        carry_ref[...] = jnp.zeros_like(carry_ref)

    halves = [slice(0, tm // 2), slice(tm // 2, tm)]

    def rms(v, g_ref):
        return (v * lax.rsqrt(jnp.mean(v * v, axis=-1, keepdims=True) + NORM_EPS)) * g_ref[...]

    def mm(lhs, w_ref):
        return jnp.dot(lhs, w_ref[...], preferred_element_type=F32)

    xs = [x_ref[sl, :] for sl in halves]
    gates = [jax.nn.sigmoid(mm(rms(x, ga_ref).astype(BF16), wg_ref) + bg_ref[...]) for x in xs]
    mixed = [g[:, :D_MODEL] * mm(yda_ref[sl, :], wda_ref) + g[:, D_MODEL:] * mm(yml_ref[sl, :], wml_ref)
             for g, sl in zip(gates, halves)]
    x1s = [x + mm(m.astype(BF16), wo_ref) for x, m in zip(xs, mixed)]
    hns = [rms(x1, gf_ref) for x1 in x1s]
    for sl, x1, hn in zip(halves, x1s, hns):
        x1_ref[sl, :] = x1
        hn_ref[sl, :] = hn

    logits = jnp.concatenate([mm(hn.astype(BF16), wr_ref) for hn in hns], axis=0) + br_ref[...]
    lane = _lane_iota((tm, LANES))
    neg = -jnp.inf
    big = jnp.int32(LANES)

    gl = jnp.where((lane >= N_EXPERTS) & (lane < N_EXPERTS + N_GROUPS), logits, neg)
    gmax = jnp.max(gl, axis=-1, keepdims=True)
    gsum = jnp.sum(jnp.exp(gl - gmax), axis=-1, keepdims=True)
    g_top = 1.0 / gsum
    g_idx = jnp.min(jnp.where(gl == gmax, lane, big), axis=-1, keepdims=True) - N_EXPERTS

    el = jnp.where((lane < N_EXPERTS) & ((lane >> GROUP_SHIFT) == g_idx), logits, neg)
    emax = jnp.max(el, axis=-1, keepdims=True)
    esum = jnp.sum(jnp.exp(el - emax), axis=-1, keepdims=True)
    e0 = jnp.min(jnp.where(el == emax, lane, big), axis=-1, keepdims=True)
    el2 = jnp.where(lane == e0, neg, el)
    emax2 = jnp.max(el2, axis=-1, keepdims=True)
    e1 = jnp.min(jnp.where(el2 == emax2, lane, big), axis=-1, keepdims=True)
    p0 = 1.0 / esum
    p1 = jnp.exp(emax2 - emax) / esum
    tot = p0 + p1
    w0 = g_top * (p0 / tot)
    w1 = g_top * (p1 / tot)

    hit0 = lane == e0
    hit1 = lane == e1
    onehot = (hit0 | hit1).astype(F32)
    before = (_lane_iota((tm, tm)) < _row_iota((tm, tm))).astype(BF16)
    prefix = jnp.dot(before, onehot.astype(BF16), preferred_element_type=F32) + carry_ref[...]
    r0 = jnp.sum(jnp.where(hit0, prefix, 0.0), axis=-1, keepdims=True)
    r1 = jnp.sum(jnp.where(hit1, prefix, 0.0), axis=-1, keepdims=True)
    carry_ref[...] = carry_ref[...] + jnp.sum(onehot, axis=0, keepdims=True)
    cnt_ref[...] = carry_ref[...]

    rc = jnp.where(lane == 0, e0.astype(F32),
         jnp.where(lane == 1, e1.astype(F32),
         jnp.where(lane == 2, r0,
         jnp.where(lane == 3, r1,
         jnp.where(lane == 4, w0,
         jnp.where(lane == 5, w1, 0.0))))))
    rc_ref[...] = rc
    rt_ref[...] = rc.T[0:SUBLANES, :].astype(jnp.int32)


def _merge(x2, yda, yml, g_attn, w_gate, b_gate, w_da, w_ml, w_out, g_ffn, w_rt, b_rt):
    n_tok, d = x2.shape
    tm = min(TM_MERGE, n_tok)
    grid = (n_tok // tm,)
    tok = lambda width: pl.BlockSpec((tm, width), lambda i: (i, 0))
    full = lambda shape: pl.BlockSpec(shape, lambda i: (0,) * len(shape))
    return pl.pallas_call(
        functools.partial(_merge_kernel, tm=tm),
        grid=grid,
        in_specs=[tok(d), tok(DA_WIDTH), tok(ML_WIDTH), full((1, d)), full((d, 2 * d)), full((1, 2 * d)),
                  full((DA_WIDTH, d)), full((ML_WIDTH, d)), full((d, d)), full((1, d)),
                  full((d, LANES)), full((1, LANES))],
        out_specs=[tok(d), tok(d), tok(LANES), pl.BlockSpec((SUBLANES, tm), lambda i: (0, i)),
                   full((1, LANES))],
        out_shape=[
            jax.ShapeDtypeStruct((n_tok, d), F32),
            jax.ShapeDtypeStruct((n_tok, d), F32),
            jax.ShapeDtypeStruct((n_tok, LANES), F32),
            jax.ShapeDtypeStruct((SUBLANES, n_tok), jnp.int32),
            jax.ShapeDtypeStruct((1, LANES), F32),
        ],
        scratch_shapes=[pltpu.VMEM((1, LANES), F32)],
        compiler_params=_cparams(("arbitrary",)),
        name="merge",
    )(x2, yda, yml, g_attn, w_gate, b_gate, w_da, w_ml, w_out, g_ffn, w_rt, b_rt)


def _dest_kernel(layout_ref, rt_ref, o_ref):
    tile = rt_ref[...]
    first = jnp.zeros_like(tile)
    for ex in range(N_EXPERTS):
        first = jnp.where(tile == ex, layout_ref[ex], first)
    o_ref[...] = first + pltpu.roll(tile, SUBLANES - 2, axis=0)


def _dest(layout, rt):
    _, n_tok = rt.shape
    tm = min(TM_ROWS, n_tok)
    return pl.pallas_call(
        _dest_kernel,
        grid_spec=pltpu.PrefetchScalarGridSpec(
            num_scalar_prefetch=1,
            grid=(n_tok // tm,),
            in_specs=[pl.BlockSpec((SUBLANES, tm), lambda i, ly: (0, i))],
            out_specs=pl.BlockSpec((SUBLANES, tm), lambda i, ly: (0, i)),
        ),
        out_shape=jax.ShapeDtypeStruct((SUBLANES, n_tok), jnp.int32),
        compiler_params=_cparams(("parallel",)),
        name="dest",
    )(layout, rt)


def _load_indices(idx_refs, smem_refs, sem, base, tm):
    copies = [pltpu.make_async_copy(src.at[pl.ds(base, tm)], dst, sem.at[n])
              for n, (src, dst) in enumerate(zip(idx_refs, smem_refs))]
    for cp in copies:
        cp.start()
    for cp in copies:
        cp.wait()


def _dispatch_kernel(pstart_ref, d0_ref, d1_ref, hn_ref, xs_ref,
                     d0_s, d1_s, zblk, isem, sem, zsem, *, tm, n_blocks):

    @pl.when(pl.program_id(0) == 0)
    def _():
        zblk[...] = jnp.zeros_like(zblk)

        def zero_copy(row0):
            return pltpu.make_async_copy(zblk, xs_ref.at[pl.ds(pl.multiple_of(row0, EXPERT_BLK), EXPERT_BLK)], zsem)

        def last_block(e, n):
            end = pstart_ref[N_EXPERTS + e]
            has = end > pstart_ref[e]

            @pl.when(has)
            def _():
                zero_copy(end - EXPERT_BLK).start()

            return n + has.astype(jnp.int32)

        def tail_block(j, n):
            zero_copy(j * EXPERT_BLK).start()
            return n + 1

        n = lax.fori_loop(0, N_EXPERTS, last_block, 0)
        n = lax.fori_loop(pstart_ref[2 * N_EXPERTS], n_blocks, tail_block, n)

        def drain(_, c):
            zero_copy(0).wait()
            return c

        lax.fori_loop(0, n, drain, 0)

    base = pl.multiple_of(pl.program_id(0) * tm, tm)
    _load_indices((d0_ref, d1_ref), (d0_s, d1_s), isem, base, tm)

    def row_copy(t, dest):
        return pltpu.make_async_copy(hn_ref.at[pl.ds(t, 1)], xs_ref.at[pl.ds(dest, 1)], sem)

    def issue(t, _):
        row_copy(t, d0_s[t]).start(priority=0)
        row_copy(t, d1_s[t]).start(priority=1)
        return 0

    lax.fori_loop(0, tm, issue, 0, unroll=ISSUE_UNROLL)

    for _ in range(2):
        pltpu.make_async_copy(hn_ref, xs_ref.at[pl.ds(0, tm)], sem).wait()


def _dispatch(layout, d0, d1, hn, cap):
    n_tok, d = hn.shape
    tm = min(TM_ROWS, n_tok)
    grid = (n_tok // tm,)
    any_spec = pl.BlockSpec(memory_space=pl.ANY)
    return pl.pallas_call(
        functools.partial(_dispatch_kernel, tm=tm, n_blocks=cap // EXPERT_BLK),
        grid_spec=pltpu.PrefetchScalarGridSpec(
            num_scalar_prefetch=1,
            grid=grid,
            in_specs=[any_spec, any_spec, pl.BlockSpec((tm, d), lambda i, ps: (i, 0))],
            out_specs=any_spec,
            scratch_shapes=[pltpu.SMEM((tm,), jnp.int32)] * 2
            + [pltpu.VMEM((EXPERT_BLK, d), F32), pltpu.SemaphoreType.DMA((2,)), pltpu.SemaphoreType.DMA,
               pltpu.SemaphoreType.DMA],
        ),
        out_shape=jax.ShapeDtypeStruct((cap, d), F32),
        compiler_params=_cparams(("arbitrary",)),
        name="dispatch",
    )(layout, d0, d1, hn)


def _experts_kernel(be_ref, nb_ref, xs_ref, w1_ref, w3_ref, w2_ref, y_ref, w1b, w3b, w2b):
    i = pl.program_id(0)
    used = i < nb_ref[0]

    @pl.when(used & ((i == 0) | (be_ref[i] != be_ref[jnp.maximum(i - 1, 0)])))
    def _():
        w1b[...] = w1_ref[0].astype(BF16)
        w3b[...] = w3_ref[0].astype(BF16)
        w2b[...] = w2_ref[0].astype(BF16)

    @pl.when(used)
    def _():
        xb = xs_ref[...].astype(BF16)
        a = jnp.dot(xb, w1b[...], preferred_element_type=F32)
        b = jnp.dot(xb, w3b[...], preferred_element_type=F32)
        hid = (a * jax.nn.sigmoid(a)) * b
        y_ref[...] = jnp.dot(hid.astype(BF16), w2b[...], preferred_element_type=F32)

    @pl.when(jnp.logical_not(used))
    def _():
        y_ref[...] = jnp.zeros_like(y_ref)


def _experts(blk_expert, n_blocks_used, xs, w1, w3, w2):
    cap, d = xs.shape
    n_blocks = cap // EXPERT_BLK
    rows = lambda i, be, nb: (jnp.minimum(i, nb[0] - 1), 0)
    wsel = lambda i, be, nb: (be[jnp.minimum(i, nb[0] - 1)], 0, 0)
    return pl.pallas_call(
        _experts_kernel,
        grid_spec=pltpu.PrefetchScalarGridSpec(
            num_scalar_prefetch=2,
            grid=(n_blocks,),
            in_specs=[pl.BlockSpec((EXPERT_BLK, d), rows),
                      pl.BlockSpec((1, d, EXPERT_FF), wsel),
                      pl.BlockSpec((1, d, EXPERT_FF), wsel),
                      pl.BlockSpec((1, EXPERT_FF, d), wsel)],
            out_specs=pl.BlockSpec((EXPERT_BLK, d), lambda i, be, nb: (i, 0)),
            scratch_shapes=[pltpu.VMEM((d, EXPERT_FF), BF16), pltpu.VMEM((d, EXPERT_FF), BF16),
                            pltpu.VMEM((EXPERT_FF, d), BF16)],
        ),
        out_shape=jax.ShapeDtypeStruct((cap, d), F32),
        compiler_params=_cparams(("arbitrary",)),
        name="experts",
    )(blk_expert, n_blocks_used, xs, w1, w3, w2)


def _combine_kernel(d0_ref, d1_ref, x1_ref, rc_ref, ys_ref, o_ref, d0_s, d1_s, ybuf, isem, sem, *, tm):
    base = pl.multiple_of(pl.program_id(0) * tm, tm)
    _load_indices((d0_ref, d1_ref), (d0_s, d1_s), isem, base, tm)

    def row_copy(slot, t, src):
        return pltpu.make_async_copy(ys_ref.at[pl.ds(src, 1)], ybuf.at[slot, pl.ds(t, 1)], sem)

    def issue(t, _):
        row_copy(0, t, d0_s[t]).start(priority=0)
        row_copy(1, t, d1_s[t]).start(priority=1)
        return 0

    lax.fori_loop(0, tm, issue, 0, unroll=ISSUE_UNROLL)

    for slot in range(2):
        pltpu.make_async_copy(ys_ref.at[pl.ds(0, tm)], ybuf.at[slot], sem).wait()

    rc = rc_ref[...]
    o_ref[...] = x1_ref[...] + (rc[:, 4:5] * ybuf[0] + rc[:, 5:6] * ybuf[1])


def _combine(d0, d1, x1, rc, ys):
    n_tok, d = x1.shape
    tm = min(TM_ROWS, n_tok)
    any_spec = pl.BlockSpec(memory_space=pl.ANY)
    return pl.pallas_call(
        functools.partial(_combine_kernel, tm=tm),
        grid=(n_tok // tm,),
        in_specs=[any_spec, any_spec,
                  pl.BlockSpec((tm, d), lambda i: (i, 0)),
                  pl.BlockSpec((tm, LANES), lambda i: (i, 0)),
                  any_spec],
        out_specs=pl.BlockSpec((tm, d), lambda i: (i, 0)),
        scratch_shapes=[pltpu.SMEM((tm,), jnp.int32)] * 2
        + [pltpu.VMEM((2, tm, d), F32), pltpu.SemaphoreType.DMA((2,)), pltpu.SemaphoreType.DMA],
        out_shape=jax.ShapeDtypeStruct((n_tok, d), F32),
        compiler_params=_cparams(("arbitrary",)),
        name="combine",
    )(d0, d1, x1, rc, ys)


def _pad_lanes(vec, width=LANES):
    return jnp.zeros((1, width), F32).at[0, :vec.shape[0]].set(vec.astype(F32))


def _layout_w_in(w_in):
    d = w_in.shape[0]
    gates = jnp.concatenate([w_in[:, 3584:3592], jnp.zeros((d, LANES - 2 * ML_HEADS), w_in.dtype)], axis=-1)
    w_all = jnp.concatenate([w_in[:, 512:1024], w_in[:, 1536:2560], w_in[:, 3072:3584], gates],
                            axis=-1).astype(BF16)
    w_t = jnp.concatenate([w_in[:, 0:512], w_in[:, 1024:1536], w_in[:, 2560:3072]], axis=-1).T.astype(BF16)
    return w_all, w_t


def kernel(x, attn_norm_g, w_in, da_q_norm_g, da_k_norm_g, da_lambda_q1, da_lambda_k1, da_lambda_q2, da_lambda_k2, da_out_norm_g, ml_conv_w, ml_conv_b, ml_i_bias, ml_f_bias, ml_out_norm_g, w_branch_da, w_branch_ml, w_gate, b_gate, w_out, ffn_norm_g, w_group, b_group, w_router, b_router, w1, w3, w2):
    b_sz, s_len, d = x.shape
    n_tok = b_sz * s_len
    ly = 0

    w_all, w_t = _layout_w_in(w_in[ly])
    g_attn = attn_norm_g[ly].reshape(1, d)
    gq_col = _pad_lanes(da_q_norm_g[ly]).reshape(LANES, 1)
    gk = _pad_lanes(da_k_norm_g[ly])
    lam_vecs = jnp.concatenate([_pad_lanes(v[ly]) for v in
                                (da_lambda_q1, da_lambda_k1, da_lambda_q2, da_lambda_k2)], axis=0)
    gate_b = _pad_lanes(jnp.concatenate([ml_i_bias[ly], ml_f_bias[ly]]))
    w_rt = jnp.concatenate([w_router[ly], w_group[ly],
                            jnp.zeros((d, LANES - N_EXPERTS - N_GROUPS), F32)], axis=-1).astype(BF16)
    b_rt = _pad_lanes(jnp.concatenate([b_router[ly], b_group[ly]]))

    qt, ka, vt, mqk, mvt, mo, gif = _in_proj(x, g_attn, w_all, w_t, gq_col, gk)
    y_da = _attn(lam_vecs, qt, ka, vt, da_out_norm_g[ly].reshape(DA_V_DIM, 1))
    y_ml = _mlstm(mqk, mvt, mo, gif, ml_conv_w[ly], ml_conv_b[ly].reshape(1, -1), gate_b,
                  ml_out_norm_g[ly].reshape(1, ML_WIDTH))

    x1, hn, rc, rt, counts = _merge(
        x.reshape(n_tok, d), y_da.reshape(n_tok, DA_WIDTH), y_ml.reshape(n_tok, ML_WIDTH), g_attn,
        w_gate[ly].astype(BF16), b_gate[ly].reshape(1, -1), w_branch_da[ly].astype(BF16),
        w_branch_ml[ly].astype(BF16), w_out[ly].astype(BF16), ffn_norm_g[ly].reshape(1, d), w_rt, b_rt)

    cnt = counts[0, :N_EXPERTS].astype(jnp.int32)
    padded = (cnt + EXPERT_BLK - 1) // EXPERT_BLK * EXPERT_BLK
    pend = jnp.cumsum(padded)
    pstart = (pend - padded).astype(jnp.int32)
    cap = 2 * n_tok + N_EXPERTS * EXPERT_BLK
    n_blocks = cap // EXPERT_BLK
    blk_row0 = jnp.arange(n_blocks, dtype=jnp.int32) * EXPERT_BLK
    blk_expert = jnp.minimum(jnp.sum((pend[None, :] <= blk_row0[:, None]).astype(jnp.int32), axis=1),
                             N_EXPERTS - 1)
    n_used = (pend[-1:] // EXPERT_BLK).astype(jnp.int32)
    layout = jnp.concatenate([pstart, pend.astype(jnp.int32), n_used])
    dest = _dest(layout, rt)
    d0, d1 = dest[0], dest[1]

    xs = _dispatch(layout, d0, d1, hn, cap)
    ys = _experts(blk_expert, n_used, xs, w1[ly], w3[ly], w2[ly])
    out = _combine(d0, d1, x1, rc, ys)
    return out.reshape(b_sz, s_len, d)
```

```python
import functools
import math

import jax
import jax.numpy as jnp
import numpy as np
from jax import lax
from jax.experimental import pallas as pl
from jax.experimental.pallas import tpu as pltpu

F32 = jnp.float32
BF16 = jnp.bfloat16

D_MODEL = 1024
DA_HEADS = 4
DA_HEAD_DIM = 64
DA_V_DIM = 128
DA_WIDTH = 512
ML_HEADS = 4
ML_HEAD_DIM = 128
ML_WIDTH = 512
CONV_WIDTH = 4
N_GROUPS = 4
EXPERTS_PER_GROUP = 8
N_EXPERTS = 32
EXPERT_FF = 512
NORM_EPS = 1e-6
LAM_INIT = 0.8 - 0.6 * math.exp(-0.3 * 0)

LANES = 128
SUBLANES = 8
BF16_SUBLANES = 16
VMEM_LIMIT = 56 * 1024 * 1024

SEG_K = 0
SEG_MQK = SEG_K + 512
SEG_MO = SEG_MQK + 1024
SEG_IF = SEG_MO + 512
W_ALL = SEG_IF + LANES

LOG2E = math.log2(math.e)
ALIBI_PIECES = 4
AUX0 = DA_HEAD_DIM
POS_PERIOD = 512
POS_HALF = POS_PERIOD // 2
DV_AUG = DA_V_DIM + BF16_SUBLANES
GROUP_SHIFT = EXPERTS_PER_GROUP.bit_length() - 1


def _bf16_round(val):
    bits = np.float32(val).view(np.uint32)
    bits = (bits + np.uint32(0x7FFF) + ((bits >> np.uint32(16)) & np.uint32(1))) & np.uint32(0xFFFF0000)
    return float(bits.view(np.float32))


def _alibi_pieces(hd):
    rest, pieces = 2.0 ** (-8.0 * (hd + 1) / DA_HEADS) * LOG2E, []
    for _ in range(ALIBI_PIECES):
        pieces.append(_bf16_round(rest))
        rest -= pieces[-1]
    return pieces

TM_PROJ = 512
TQ_ATTN = 512
ATTN_HEADS = 4
L_MLSTM = 256
TM_MERGE = 512
TM_ROWS = 1024
EXPERT_BLK = 512
ISSUE_UNROLL = 8


def _cparams(sem):
    return pltpu.CompilerParams(dimension_semantics=sem, vmem_limit_bytes=VMEM_LIMIT)


def _lane_iota(shape):
    return lax.broadcasted_iota(jnp.int32, shape, len(shape) - 1)


def _row_iota(shape):
    return lax.broadcasted_iota(jnp.int32, shape, len(shape) - 2)


def _in_proj_kernel(x_ref, g_ref, w_ref, wt_ref, gq_ref, gk_ref, aux_ref,
                    qt_ref, ka_ref, vt_ref, mqk_ref, mvt_ref, mo_ref, gif_ref, *, tm):
    n = tm // 2
    for r0 in (0, n):
        _in_proj_rows(x_ref, g_ref, w_ref, wt_ref, gq_ref, gk_ref, aux_ref,
                      qt_ref, ka_ref, vt_ref, mqk_ref, mvt_ref, mo_ref, gif_ref, r0, n, tm)


def _in_proj_rows(x_ref, g_ref, w_ref, wt_ref, gq_ref, gk_ref, aux_ref,
                  qt_ref, ka_ref, vt_ref, mqk_ref, mvt_ref, mo_ref, gif_ref, r0, n, tm):
    rows = slice(r0, r0 + n)
    x = x_ref[0, rows, :]
    ms = jnp.mean(x * x, axis=-1, keepdims=True)
    h = ((x * lax.rsqrt(ms + NORM_EPS)) * g_ref[...]).astype(BF16)

    def seg(lo, width):
        return jnp.dot(h, w_ref[:, lo:lo + width], preferred_element_type=F32)

    inv_dh = 1.0 / DA_HEAD_DIM

    t_all = lax.dot_general(wt_ref[...], h, (((1,), (1,)), ((), ())), preferred_element_type=F32)
    dh = DA_HEAD_DIM
    for hd in range(DA_HEADS):
        aux_rows = jnp.broadcast_to(aux_ref[hd, dh:LANES, :], (LANES - dh, n)).astype(BF16)
        for mp in range(2):
            off = (hd * 2 + mp) * dh
            q = t_all[off:off + dh, :]
            qn = (q * lax.rsqrt(jnp.sum(q * q, axis=0, keepdims=True) * inv_dh + NORM_EPS)) * gq_ref[0:dh, :]
            qt_ref[0, hd, mp * LANES:mp * LANES + dh, rows] = (qn * (dh ** -0.5 * LOG2E)).astype(BF16)
            qt_ref[0, hd, mp * LANES + dh:(mp + 1) * LANES, rows] = aux_rows
        v_lo = DA_HEADS * 2 * dh + hd * DA_V_DIM
        vt_ref[0, hd, 0:DA_V_DIM, rows] = t_all[v_lo:v_lo + DA_V_DIM, :].astype(BF16)
        ones_row = (_row_iota((DV_AUG - DA_V_DIM, n)) == 0).astype(BF16)
        vt_ref[0, hd, DA_V_DIM:DV_AUG, rows] = ones_row

    lane = _lane_iota((n, LANES))
    pos = pl.program_id(1) * tm + r0 + _row_iota((n, LANES))
    pos_lo = (pos & (POS_HALF - 1)).astype(F32)
    pos_hi = (pos & POS_HALF).astype(F32)
    is_aux = (lane >= AUX0) & (lane < AUX0 + 2 * ALIBI_PIECES)
    is_lo = is_aux & ((lane & 1) == 0)
    is_hi = is_aux & ((lane & 1) == 1)
    real = lane < dh
    k_all = seg(SEG_K, DA_HEADS * 2 * dh)
    for hd in range(DA_HEADS):
        k_pair = k_all[:, hd * LANES:(hd + 1) * LANES]
        for mp in range(2):
            k = jnp.where(real, k_pair if mp == 0 else pltpu.roll(k_pair, dh, axis=1), 0.0)
            kn = (k * lax.rsqrt(jnp.sum(k * k, axis=-1, keepdims=True) * inv_dh + NORM_EPS)) * gk_ref[...]
            kn = jnp.where(is_lo, pos_lo, jnp.where(is_hi, pos_hi, kn))
            ka_ref[0, hd, rows, mp * LANES:(mp + 1) * LANES] = kn.astype(BF16)

    mqk_ref[0, rows, :] = seg(SEG_MQK, 2 * ML_WIDTH)
    mvt_ref[0, :, rows] = t_all[DA_HEADS * 2 * dh + DA_WIDTH:, :].astype(BF16)
    mo_ref[0, rows, :] = seg(SEG_MO, ML_WIDTH)
    gif_ref[0, rows, :] = seg(SEG_IF, LANES)


def _in_proj(x, g_attn, w_all, w_t, gq_col, gk):
    b_sz, s_len, d = x.shape
    tm = min(TM_PROJ, s_len)
    grid = (b_sz, s_len // tm)
    tok = lambda width: pl.BlockSpec((1, tm, width), lambda b, s: (b, s, 0))
    full = lambda shape: pl.BlockSpec(shape, lambda b, s: (0,) * len(shape))
    n_t = w_t.shape[0]
    aux = np.zeros((DA_HEADS, LANES, 1), np.float32)
    for hd in range(DA_HEADS):
        aux[hd, AUX0:AUX0 + 2 * ALIBI_PIECES, 0] = np.repeat(_alibi_pieces(hd), 2)
    return pl.pallas_call(
        functools.partial(_in_proj_kernel, tm=tm),
        grid=grid,
        in_specs=[tok(d), full((1, d)), full((d, W_ALL)), full((n_t, d)), full((LANES, 1)), full((1, LANES)),
                  full((DA_HEADS, LANES, 1))],
        out_specs=[pl.BlockSpec((1, DA_HEADS, 2 * LANES, tm), lambda b, s: (b, 0, 0, s)),
                   pl.BlockSpec((1, DA_HEADS, tm, 2 * LANES), lambda b, s: (b, 0, s, 0)),
                   pl.BlockSpec((1, DA_HEADS, DV_AUG, tm), lambda b, s: (b, 0, 0, s)),
                   tok(2 * ML_WIDTH), pl.BlockSpec((1, ML_WIDTH, tm), lambda b, s: (b, 0, s)), tok(ML_WIDTH),
                   tok(LANES)],
        out_shape=[
            jax.ShapeDtypeStruct((b_sz, DA_HEADS, 2 * LANES, s_len), BF16),
            jax.ShapeDtypeStruct((b_sz, DA_HEADS, s_len, 2 * LANES), BF16),
            jax.ShapeDtypeStruct((b_sz, DA_HEADS, DV_AUG, s_len), BF16),
            jax.ShapeDtypeStruct((b_sz, s_len, 2 * ML_WIDTH), F32),
            jax.ShapeDtypeStruct((b_sz, ML_WIDTH, s_len), BF16),
            jax.ShapeDtypeStruct((b_sz, s_len, ML_WIDTH), F32),
            jax.ShapeDtypeStruct((b_sz, s_len, LANES), F32),
        ],
        compiler_params=_cparams(("parallel", "parallel")),
        name="in_proj",
    )(x, g_attn, w_all, w_t, gq_col, gk, jnp.asarray(aux))


def _attn_kernel(lam_ref, qt_ref, ka_ref, vt_ref, go_ref, o_ref, acc_ref, *, tq):
    hp = pl.program_id(1)
    qi = pl.program_id(2)
    slopes = [sum(_alibi_pieces(h)) for h in range(DA_HEADS)]
    pair_slopes = []
    for hh in range(ATTN_HEADS):
        sel = slopes[hh]
        for step in range(1, DA_HEADS // ATTN_HEADS):
            sel = jnp.where(hp == step, slopes[step * ATTN_HEADS + hh], sel)
        pair_slopes.append(sel)
    chains = [(hh, mp) for hh in range(ATTN_HEADS) for mp in range(2)]
    qts = {(hh, mp): qt_ref[0, hh, mp * LANES:(mp + 1) * LANES, :] for hh, mp in chains}
    causal = _row_iota((tq, tq)) <= _lane_iota((tq, tq))
    acc_ref[...] = jnp.zeros_like(acc_ref)

    def tile(kv, carry, masked):
        start = pl.multiple_of(kv * tq, tq)
        base = ((jnp.zeros((1, tq), jnp.int32) + (start // POS_PERIOD) * POS_PERIOD) - qi * tq).astype(F32)
        scores = {}

        def issue_scores(hh):
            k = ka_ref[0, hh, pl.ds(start, tq), :]
            for mp in range(2):
                scores[hh, mp] = jnp.dot(k[:, mp * LANES:(mp + 1) * LANES], qts[hh, mp],
                                         preferred_element_type=F32)

        issue_scores(0)
        out = {}
        for hh in range(ATTN_HEADS):
            if hh + 1 < ATTN_HEADS:
                issue_scores(hh + 1)
            c = base * pair_slopes[hh]
            probs = []
            for mp in range(2):
                s = scores[hh, mp]
                if masked:
                    s = jnp.where(causal, s, -jnp.inf)
                m_old = carry[chains.index((hh, mp))]
                m_new = jnp.maximum(m_old, jnp.max(s, axis=0, keepdims=True) + c)
                probs.append((jnp.exp2(m_old - m_new), jnp.exp2(s - (m_new - c)).astype(BF16)))
                out[hh, mp] = m_new
            vt = vt_ref[0, hh, :, pl.ds(start, tq)]
            for mp in range(2):
                alpha, p = probs[mp]
                acc_ref[hh, mp] = alpha * acc_ref[hh, mp] + jnp.dot(vt, p, preferred_element_type=F32)
        return tuple(out[ch] for ch in chains)

    init = tuple(jnp.full((1, tq), -jnp.inf, F32) for _ in chains)
    carry = lax.fori_loop(0, qi, lambda kv, cr: tile(kv, cr, False), init)
    tile(qi, carry, True)

    lam_v = lam_ref[...]
    lam = (jnp.exp(jnp.sum(lam_v[0:1] * lam_v[1:2], axis=-1, keepdims=True))
           - jnp.exp(jnp.sum(lam_v[2:3] * lam_v[3:4], axis=-1, keepdims=True)) + LAM_INIT)
    for hh in range(ATTN_HEADS):
        l1 = acc_ref[hh, 0, DA_V_DIM:DA_V_DIM + 1, :]
        l2 = acc_ref[hh, 1, DA_V_DIM:DA_V_DIM + 1, :]
        o = acc_ref[hh, 0, 0:DA_V_DIM, :] / l1 - lam * (acc_ref[hh, 1, 0:DA_V_DIM, :] / l2)
        o = (o * lax.rsqrt(jnp.mean(o * o, axis=0, keepdims=True) + NORM_EPS)) * go_ref[...]
        o_ref[0, :, hh * DA_V_DIM:(hh + 1) * DA_V_DIM] = (o * (1.0 - LAM_INIT)).T.astype(BF16)


def _attn(lam_vecs, qt, ka, vt, g_out_col):
    b_sz, _, s_len, _ = ka.shape
    tq = min(TQ_ATTN, s_len)
    grid = (b_sz, DA_HEADS // ATTN_HEADS, s_len // tq)
    return pl.pallas_call(
        functools.partial(_attn_kernel, tq=tq),
        grid=grid,
        in_specs=[
            pl.BlockSpec((4, LANES), lambda b, h, i: (0, 0)),
            pl.BlockSpec((1, ATTN_HEADS, 2 * LANES, tq), lambda b, h, i: (b, h, 0, i)),
            pl.BlockSpec((1, ATTN_HEADS, s_len, 2 * LANES), lambda b, h, i: (b, h, 0, 0)),
            pl.BlockSpec((1, ATTN_HEADS, DV_AUG, s_len), lambda b, h, i: (b, h, 0, 0)),
            pl.BlockSpec((DA_V_DIM, 1), lambda b, h, i: (0, 0)),
        ],
        out_specs=pl.BlockSpec((1, tq, ATTN_HEADS * DA_V_DIM), lambda b, h, i: (b, i, h)),
        out_shape=jax.ShapeDtypeStruct((b_sz, s_len, DA_WIDTH), BF16),
        scratch_shapes=[pltpu.VMEM((ATTN_HEADS, 2, DV_AUG, tq), F32)],
        compiler_params=_cparams(("parallel", "parallel", "arbitrary")),
        name="attn",
    )(lam_vecs, qt, ka, vt, g_out_col)


def _log_sigmoid(x):
    return jnp.minimum(x, 0.0) - jnp.log1p(jnp.exp(-jnp.abs(x)))


def _mlstm_kernel(mqk_ref, mvt_ref, mo_ref, gif_ref, cw_ref, cb_ref, gb_ref, go_ref,
                  y_ref, buf_ref, c_ref, n_ref, m_ref, *, L):
    halo = SUBLANES

    @pl.when(pl.program_id(1) == 0)
    def _():
        buf_ref[0:halo, :] = jnp.zeros((halo, 2 * ML_WIDTH), F32)
        c_ref[...] = jnp.zeros_like(c_ref)
        n_ref[...] = jnp.zeros_like(n_ref)
        m_ref[...] = jnp.zeros_like(m_ref)

    buf_ref[halo:halo + L, :] = mqk_ref[0]
    conv = jnp.broadcast_to(cb_ref[...], (L, 2 * ML_WIDTH))
    for j in range(CONV_WIDTH):
        lo = halo - (CONV_WIDTH - 1) + j
        conv = conv + buf_ref[lo:lo + L, :] * cw_ref[j:j + 1, :]
    buf_ref[0:halo, :] = buf_ref[L:L + halo, :]
    qk = conv * jax.nn.sigmoid(conv)

    g = gif_ref[0] + gb_ref[...]
    lf = _log_sigmoid(g)
    lf_t = lf.T

    row = _row_iota((L, L))
    col = _lane_iota((L, L))
    upper = row <= col
    nt = (((1,), (1,)), ((), ()))

    for hd in range(ML_HEADS):
        sl = slice(hd * ML_HEAD_DIM, (hd + 1) * ML_HEAD_DIM)
        q = qk[:, sl] * (ML_HEAD_DIM ** -0.5)
        k = qk[:, ML_WIDTH + hd * ML_HEAD_DIM:ML_WIDTH + (hd + 1) * ML_HEAD_DIM]
        vt = mvt_ref[0, sl, :]
        i_col = g[:, hd:hd + 1]
        f_col = lf[:, ML_HEADS + hd:ML_HEADS + hd + 1]
        f_row = lf_t[ML_HEADS + hd:ML_HEADS + hd + 1, :]

        b_col = jnp.sum(jnp.where(col <= row, f_row, 0.0), axis=-1, keepdims=True)
        b_row = jnp.sum(jnp.where(upper, f_col, 0.0), axis=0, keepdims=True)
        b_last = b_row[:, L - 1:L]

        m_state = m_ref[hd:hd + 1, 0:1]
        n_state = n_ref[hd:hd + 1, :]
        ct_state = c_ref[hd]

        dmat = jnp.where(upper, b_row + (i_col - b_col), -jnp.inf)
        r = jnp.max(dmat, axis=0, keepdims=True)
        p = jnp.exp(dmat - r)
        qb = q.astype(BF16)
        kb = k.astype(BF16)
        sc = lax.dot_general(kb, qb, nt, preferred_element_type=F32) * p
        num_a = jnp.dot(vt, sc.astype(BF16), preferred_element_type=F32)
        den_a = jnp.sum(sc, axis=0, keepdims=True)

        inter = b_row + m_state
        m_row = jnp.maximum(r, inter)
        e_a = jnp.exp(r - m_row)
        e_b = jnp.exp(inter - m_row)
        q_c = lax.dot_general(ct_state.astype(BF16), qb, nt, preferred_element_type=F32)
        n_rows = jnp.broadcast_to(n_state, (SUBLANES, ML_HEAD_DIM)).astype(BF16)
        q_n = lax.dot_general(n_rows, qb, nt, preferred_element_type=F32)[0:1, :]
        num = e_a * num_a + e_b * q_c
        den = e_a * den_a + e_b * q_n
        hval = num / jnp.maximum(jnp.abs(den), jnp.exp(-m_row))

        w_col = b_last - b_col + i_col
        a = jnp.max(w_col, axis=0, keepdims=True)
        m_new = jnp.maximum(b_last + m_state, a)
        kw = k * jnp.exp(w_col - a)
        d_c = jnp.dot(vt, kw.astype(BF16), preferred_element_type=F32)
        d_n = jnp.sum(kw, axis=0, keepdims=True)
        decay = jnp.exp(b_last + m_state - m_new)
        gain = jnp.exp(a - m_new)
        c_ref[hd] = decay * ct_state + gain * d_c
        n_ref[hd:hd + 1, :] = decay * n_state + gain * d_n
        m_ref[hd:hd + 1, :] = jnp.broadcast_to(m_new, (1, LANES))

        hn_t = hval * lax.rsqrt(jnp.mean(hval * hval, axis=0, keepdims=True) + NORM_EPS)
        y_ref[0, :, sl] = (hn_t.T * go_ref[:, sl] * jax.nn.sigmoid(mo_ref[0, :, sl])).astype(BF16)


def _mlstm(mqk, mvt, mo, gif, conv_w, conv_b, gate_b, g_out):
    b_sz, s_len, _ = mqk.shape
    L = min(L_MLSTM, s_len)
    grid = (b_sz, s_len // L)
    tok = lambda width: pl.BlockSpec((1, L, width), lambda b, c: (b, c, 0))
    full = lambda shape: pl.BlockSpec(shape, lambda b, c: (0,) * len(shape))
    return pl.pallas_call(
        functools.partial(_mlstm_kernel, L=L),
        grid=grid,
        in_specs=[tok(2 * ML_WIDTH), pl.BlockSpec((1, ML_WIDTH, L), lambda b, c: (b, 0, c)), tok(ML_WIDTH),
                  tok(LANES),
                  full((CONV_WIDTH, 2 * ML_WIDTH)), full((1, 2 * ML_WIDTH)), full((1, LANES)),
                  full((1, ML_WIDTH))],
        out_specs=tok(ML_WIDTH),
        out_shape=jax.ShapeDtypeStruct((b_sz, s_len, ML_WIDTH), BF16),
        scratch_shapes=[
            pltpu.VMEM((L + SUBLANES, 2 * ML_WIDTH), F32),
            pltpu.VMEM((ML_HEADS, ML_HEAD_DIM, ML_HEAD_DIM), F32),
            pltpu.VMEM((SUBLANES, LANES), F32),
            pltpu.VMEM((SUBLANES, LANES), F32),
        ],
        compiler_params=_cparams(("parallel", "arbitrary")),
        name="mlstm",
    )(mqk, mvt, mo, gif, conv_w, conv_b, gate_b, g_out)


def _merge_kernel(x_ref, yda_ref, yml_ref, ga_ref, wg_ref, bg_ref, wda_ref, wml_ref, wo_ref,
                  gf_ref, wr_ref, br_ref,
                  x1_ref, hn_ref, rc_ref, rt_ref, cnt_ref, carry_ref, *, tm):
    @pl.when(pl.program_id(0) == 0)
    def _():
        carry_ref[...] = jnp.zeros_like(carry_ref)

    halves = [slice(0, tm // 2), slice(tm // 2, tm)]

    def rms(v, g_ref):
        return (v * lax.rsqrt(jnp.mean(v * v, axis=-1, keepdims=True) + NORM_EPS)) * g_ref[...]

    def mm(lhs, w_ref):
        return jnp.dot(lhs, w_ref[...], preferred_element_type=F32)

    xs = [x_ref[sl, :] for sl in halves]
    gates = [jax.nn.sigmoid(mm(rms(x, ga_ref).astype(BF16), wg_ref) + bg_ref[...]) for x in xs]
    mixed = [g[:, :D_MODEL] * mm(yda_ref[sl, :], wda_ref) + g[:, D_MODEL:] * mm(yml_ref[sl, :], wml_ref)
             for g, sl in zip(gates, halves)]
    x1s = [x + mm(m.astype(BF16), wo_ref) for x, m in zip(xs, mixed)]
    hns = [rms(x1, gf_ref) for x1 in x1s]
    for sl, x1, hn in zip(halves, x1s, hns):
        x1_ref[sl, :] = x1
        hn_ref[sl, :] = hn

    logits = jnp.concatenate([mm(hn.astype(BF16), wr_ref) for hn in hns], axis=0) + br_ref[...]
    lane = _lane_iota((tm, LANES))
    neg = -jnp.inf
    big = jnp.int32(LANES)

    gl = jnp.where((lane >= N_EXPERTS) & (lane < N_EXPERTS + N_GROUPS), logits, neg)
    gmax = jnp.max(gl, axis=-1, keepdims=True)
    gsum = jnp.sum(jnp.exp(gl - gmax), axis=-1, keepdims=True)
    g_top = 1.0 / gsum
    g_idx = jnp.min(jnp.where(gl == gmax, lane, big), axis=-1, keepdims=True) - N_EXPERTS

    el = jnp.where((lane < N_EXPERTS) & ((lane >> GROUP_SHIFT) == g_idx), logits, neg)
    emax = jnp.max(el, axis=-1, keepdims=True)
    esum = jnp.sum(jnp.exp(el - emax), axis=-1, keepdims=True)
    e0 = jnp.min(jnp.where(el == emax, lane, big), axis=-1, keepdims=True)
    el2 = jnp.where(lane == e0, neg, el)
    emax2 = jnp.max(el2, axis=-1, keepdims=True)
    e1 = jnp.min(jnp.where(el2 == emax2, lane, big), axis=-1, keepdims=True)
    p0 = 1.0 / esum
    p1 = jnp.exp(emax2 - emax) / esum
    tot = p0 + p1
    w0 = g_top * (p0 / tot)
    w1 = g_top * (p1 / tot)

    hit0 = lane == e0
    hit1 = lane == e1
    onehot = (hit0 | hit1).astype(F32)
    before = (_lane_iota((tm, tm)) < _row_iota((tm, tm))).astype(BF16)
    prefix = jnp.dot(before, onehot.astype(BF16), preferred_element_type=F32) + carry_ref[...]
    r0 = jnp.sum(jnp.where(hit0, prefix, 0.0), axis=-1, keepdims=True)
    r1 = jnp.sum(jnp.where(hit1, prefix, 0.0), axis=-1, keepdims=True)
    carry_ref[...] = carry_ref[...] + jnp.sum(onehot, axis=0, keepdims=True)
    cnt_ref[...] = carry_ref[...]

    rc = jnp.where(lane == 0, e0.astype(F32),
         jnp.where(lane == 1, e1.astype(F32),
         jnp.where(lane == 2, r0,
         jnp.where(lane == 3, r1,
         jnp.where(lane == 4, w0,
         jnp.where(lane == 5, w1, 0.0))))))
    rc_ref[...] = rc
    rt_ref[...] = rc.T[0:SUBLANES, :].astype(jnp.int32)


def _merge(x2, yda, yml, g_attn, w_gate, b_gate, w_da, w_ml, w_out, g_ffn, w_rt, b_rt):
    n_tok, d = x2.shape
    tm = min(TM_MERGE, n_tok)
    grid = (n_tok // tm,)
    tok = lambda width: pl.BlockSpec((tm, width), lambda i: (i, 0))
    full = lambda shape: pl.BlockSpec(shape, lambda i: (0,) * len(shape))
    return pl.pallas_call(
        functools.partial(_merge_kernel, tm=tm),
        grid=grid,
        in_specs=[tok(d), tok(DA_WIDTH), tok(ML_WIDTH), full((1, d)), full((d, 2 * d)), full((1, 2 * d)),
                  full((DA_WIDTH, d)), full((ML_WIDTH, d)), full((d, d)), full((1, d)),
                  full((d, LANES)), full((1, LANES))],
        out_specs=[tok(d), tok(d), tok(LANES), pl.BlockSpec((SUBLANES, tm), lambda i: (0, i)),
                   full((1, LANES))],
        out_shape=[
            jax.ShapeDtypeStruct((n_tok, d), F32),
            jax.ShapeDtypeStruct((n_tok, d), F32),
            jax.ShapeDtypeStruct((n_tok, LANES), F32),
            jax.ShapeDtypeStruct((SUBLANES, n_tok), jnp.int32),
            jax.ShapeDtypeStruct((1, LANES), F32),
        ],
        scratch_shapes=[pltpu.VMEM((1, LANES), F32)],
        compiler_params=_cparams(("arbitrary",)),
        name="merge",
    )(x2, yda, yml, g_attn, w_gate, b_gate, w_da, w_ml, w_out, g_ffn, w_rt, b_rt)


def _dest_kernel(layout_ref, rt_ref, o_ref):
    tile = rt_ref[...]
    first = jnp.zeros_like(tile)
    for ex in range(N_EXPERTS):
        first = jnp.where(tile == ex, layout_ref[ex], first)
    o_ref[...] = first + pltpu.roll(tile, SUBLANES - 2, axis=0)


def _dest(layout, rt):
    _, n_tok = rt.shape
    tm = min(TM_ROWS, n_tok)
    return pl.pallas_call(
        _dest_kernel,
        grid_spec=pltpu.PrefetchScalarGridSpec(
            num_scalar_prefetch=1,
            grid=(n_tok // tm,),
            in_specs=[pl.BlockSpec((SUBLANES, tm), lambda i, ly: (0, i))],
            out_specs=pl.BlockSpec((SUBLANES, tm), lambda i, ly: (0, i)),
        ),
        out_shape=jax.ShapeDtypeStruct((SUBLANES, n_tok), jnp.int32),
        compiler_params=_cparams(("parallel",)),
        name="dest",
    )(layout, rt)


def _load_indices(idx_refs, smem_refs, sem, base, tm):
    copies = [pltpu.make_async_copy(src.at[pl.ds(base, tm)], dst, sem.at[n])
              for n, (src, dst) in enumerate(zip(idx_refs, smem_refs))]
    for cp in copies:
        cp.start()
    for cp in copies:
        cp.wait()


def _dispatch_kernel(pstart_ref, d0_ref, d1_ref, hn_ref, xs_ref,
                     d0_s, d1_s, zblk, isem, sem, zsem, *, tm, n_blocks):

    @pl.when(pl.program_id(0) == 0)
    def _():
        zblk[...] = jnp.zeros_like(zblk)

        def zero_copy(row0):
            return pltpu.make_async_copy(zblk, xs_ref.at[pl.ds(pl.multiple_of(row0, EXPERT_BLK), EXPERT_BLK)], zsem)

        def last_block(e, n):
            end = pstart_ref[N_EXPERTS + e]
            has = end > pstart_ref[e]

            @pl.when(has)
            def _():
                zero_copy(end - EXPERT_BLK).start()

            return n + has.astype(jnp.int32)

        def tail_block(j, n):
            zero_copy(j * EXPERT_BLK).start()
            return n + 1

        n = lax.fori_loop(0, N_EXPERTS, last_block, 0)
        n = lax.fori_loop(pstart_ref[2 * N_EXPERTS], n_blocks, tail_block, n)

        def drain(_, c):
            zero_copy(0).wait()
            return c

        lax.fori_loop(0, n, drain, 0)

    base = pl.multiple_of(pl.program_id(0) * tm, tm)
    _load_indices((d0_ref, d1_ref), (d0_s, d1_s), isem, base, tm)

    def row_copy(t, dest):
        return pltpu.make_async_copy(hn_ref.at[pl.ds(t, 1)], xs_ref.at[pl.ds(dest, 1)], sem)

    def issue(t, _):
        row_copy(t, d0_s[t]).start(priority=0)
        row_copy(t, d1_s[t]).start(priority=1)
        return 0

    lax.fori_loop(0, tm, issue, 0, unroll=ISSUE_UNROLL)

    for _ in range(2):
        pltpu.make_async_copy(hn_ref, xs_ref.at[pl.ds(0, tm)], sem).wait()


def _dispatch(layout, d0, d1, hn, cap):
    n_tok, d = hn.shape
    tm = min(TM_ROWS, n_tok)
    grid = (n_tok // tm,)
    any_spec = pl.BlockSpec(memory_space=pl.ANY)
    return pl.pallas_call(
        functools.partial(_dispatch_kernel, tm=tm, n_blocks=cap // EXPERT_BLK),
        grid_spec=pltpu.PrefetchScalarGridSpec(
            num_scalar_prefetch=1,
            grid=grid,
            in_specs=[any_spec, any_spec, pl.BlockSpec((tm, d), lambda i, ps: (i, 0))],
            out_specs=any_spec,
            scratch_shapes=[pltpu.SMEM((tm,), jnp.int32)] * 2
            + [pltpu.VMEM((EXPERT_BLK, d), F32), pltpu.SemaphoreType.DMA((2,)), pltpu.SemaphoreType.DMA,
               pltpu.SemaphoreType.DMA],
        ),
        out_shape=jax.ShapeDtypeStruct((cap, d), F32),
        compiler_params=_cparams(("arbitrary",)),
        name="dispatch",
    )(layout, d0, d1, hn)


def _experts_kernel(be_ref, nb_ref, xs_ref, w1_ref, w3_ref, w2_ref, y_ref, w1b, w3b, w2b):
    i = pl.program_id(0)
    used = i < nb_ref[0]

    @pl.when(used & ((i == 0) | (be_ref[i] != be_ref[jnp.maximum(i - 1, 0)])))
    def _():
        w1b[...] = w1_ref[0].astype(BF16)
        w3b[...] = w3_ref[0].astype(BF16)
        w2b[...] = w2_ref[0].astype(BF16)

    @pl.when(used)
    def _():
        xb = xs_ref[...].astype(BF16)
        a = jnp.dot(xb, w1b[...], preferred_element_type=F32)
        b = jnp.dot(xb, w3b[...], preferred_element_type=F32)
        hid = (a * jax.nn.sigmoid(a)) * b
        y_ref[...] = jnp.dot(hid.astype(BF16), w2b[...], preferred_element_type=F32)

    @pl.when(jnp.logical_not(used))
    def _():
        y_ref[...] = jnp.zeros_like(y_ref)


def _experts(blk_expert, n_blocks_used, xs, w1, w3, w2):
    cap, d = xs.shape
    n_blocks = cap // EXPERT_BLK
    rows = lambda i, be, nb: (jnp.minimum(i, nb[0] - 1), 0)
    wsel = lambda i, be, nb: (be[jnp.minimum(i, nb[0] - 1)], 0, 0)
    return pl.pallas_call(
        _experts_kernel,
        grid_spec=pltpu.PrefetchScalarGridSpec(
            num_scalar_prefetch=2,
            grid=(n_blocks,),
            in_specs=[pl.BlockSpec((EXPERT_BLK, d), rows),
                      pl.BlockSpec((1, d, EXPERT_FF), wsel),
                      pl.BlockSpec((1, d, EXPERT_FF), wsel),
                      pl.BlockSpec((1, EXPERT_FF, d), wsel)],
            out_specs=pl.BlockSpec((EXPERT_BLK, d), lambda i, be, nb: (i, 0)),
            scratch_shapes=[pltpu.VMEM((d, EXPERT_FF), BF16), pltpu.VMEM((d, EXPERT_FF), BF16),
                            pltpu.VMEM((EXPERT_FF, d), BF16)],
        ),
        out_shape=jax.ShapeDtypeStruct((cap, d), F32),
        compiler_params=_cparams(("arbitrary",)),
        name="experts",
    )(blk_expert, n_blocks_used, xs, w1, w3, w2)


def _combine_kernel(d0_ref, d1_ref, x1_ref, rc_ref, ys_ref, o_ref, d0_s, d1_s, ybuf, isem, sem, *, tm):
    base = pl.multiple_of(pl.program_id(0) * tm, tm)
    _load_indices((d0_ref, d1_ref), (d0_s, d1_s), isem, base, tm)

    def row_copy(slot, t, src):
        return pltpu.make_async_copy(ys_ref.at[pl.ds(src, 1)], ybuf.at[slot, pl.ds(t, 1)], sem)

    def issue(t, _):
        row_copy(0, t, d0_s[t]).start(priority=0)
        row_copy(1, t, d1_s[t]).start(priority=1)
        return 0

    lax.fori_loop(0, tm, issue, 0, unroll=ISSUE_UNROLL)

    for slot in range(2):
        pltpu.make_async_copy(ys_ref.at[pl.ds(0, tm)], ybuf.at[slot], sem).wait()

    rc = rc_ref[...]
    o_ref[...] = x1_ref[...] + (rc[:, 4:5] * ybuf[0] + rc[:, 5:6] * ybuf[1])


def _combine(d0, d1, x1, rc, ys):
    n_tok, d = x1.shape
    tm = min(TM_ROWS, n_tok)
    any_spec = pl.BlockSpec(memory_space=pl.ANY)
    return pl.pallas_call(
        functools.partial(_combine_kernel, tm=tm),
        grid=(n_tok // tm,),
        in_specs=[any_spec, any_spec,
                  pl.BlockSpec((tm, d), lambda i: (i, 0)),
                  pl.BlockSpec((tm, LANES), lambda i: (i, 0)),
                  any_spec],
        out_specs=pl.BlockSpec((tm, d), lambda i: (i, 0)),
        scratch_shapes=[pltpu.SMEM((tm,), jnp.int32)] * 2
        + [pltpu.VMEM((2, tm, d), F32), pltpu.SemaphoreType.DMA((2,)), pltpu.SemaphoreType.DMA],
        out_shape=jax.ShapeDtypeStruct((n_tok, d), F32),
        compiler_params=_cparams(("arbitrary",)),
        name="combine",
    )(d0, d1, x1, rc, ys)


def _pad_lanes(vec, width=LANES):
    return jnp.zeros((1, width), F32).at[0, :vec.shape[0]].set(vec.astype(F32))


def _layout_w_in(w_in):
    d = w_in.shape[0]
    gates = jnp.concatenate([w_in[:, 3584:3592], jnp.zeros((d, LANES - 2 * ML_HEADS), w_in.dtype)], axis=-1)
    w_all = jnp.concatenate([w_in[:, 512:1024], w_in[:, 1536:2560], w_in[:, 3072:3584], gates],
                            axis=-1).astype(BF16)
    w_t = jnp.concatenate([w_in[:, 0:512], w_in[:, 1024:1536], w_in[:, 2560:3072]], axis=-1).T.astype(BF16)
    return w_all, w_t


def kernel(x, attn_norm_g, w_in, da_q_norm_g, da_k_norm_g, da_lambda_q1, da_lambda_k1, da_lambda_q2, da_lambda_k2, da_out_norm_g, ml_conv_w, ml_conv_b, ml_i_bias, ml_f_bias, ml_out_norm_g, w_branch_da, w_branch_ml, w_gate, b_gate, w_out, ffn_norm_g, w_group, b_group, w_router, b_router, w1, w3, w2):
    b_sz, s_len, d = x.shape
    n_tok = b_sz * s_len
    ly = 0

    w_all, w_t = _layout_w_in(w_in[ly])
    g_attn = attn_norm_g[ly].reshape(1, d)
    gq_col = _pad_lanes(da_q_norm_g[ly]).reshape(LANES, 1)
    gk = _pad_lanes(da_k_norm_g[ly])
    lam_vecs = jnp.concatenate([_pad_lanes(v[ly]) for v in
                                (da_lambda_q1, da_lambda_k1, da_lambda_q2, da_lambda_k2)], axis=0)
    gate_b = _pad_lanes(jnp.concatenate([ml_i_bias[ly], ml_f_bias[ly]]))
    w_rt = jnp.concatenate([w_router[ly], w_group[ly],
                            jnp.zeros((d, LANES - N_EXPERTS - N_GROUPS), F32)], axis=-1).astype(BF16)
    b_rt = _pad_lanes(jnp.concatenate([b_router[ly], b_group[ly]]))

    qt, ka, vt, mqk, mvt, mo, gif = _in_proj(x, g_attn, w_all, w_t, gq_col, gk)
    y_da = _attn(lam_vecs, qt, ka, vt, da_out_norm_g[ly].reshape(DA_V_DIM, 1))
    y_ml = _mlstm(mqk, mvt, mo, gif, ml_conv_w[ly], ml_conv_b[ly].reshape(1, -1), gate_b,
                  ml_out_norm_g[ly].reshape(1, ML_WIDTH))

    x1, hn, rc, rt, counts = _merge(
        x.reshape(n_tok, d), y_da.reshape(n_tok, DA_WIDTH), y_ml.reshape(n_tok, ML_WIDTH), g_attn,
        w_gate[ly].astype(BF16), b_gate[ly].reshape(1, -1), w_branch_da[ly].astype(BF16),
        w_branch_ml[ly].astype(BF16), w_out[ly].astype(BF16), ffn_norm_g[ly].reshape(1, d), w_rt, b_rt)

    cnt = counts[0, :N_EXPERTS].astype(jnp.int32)
    padded = (cnt + EXPERT_BLK - 1) // EXPERT_BLK * EXPERT_BLK
    pend = jnp.cumsum(padded)
    pstart = (pend - padded).astype(jnp.int32)
    cap = 2 * n_tok + N_EXPERTS * EXPERT_BLK
    n_blocks = cap // EXPERT_BLK
    blk_row0 = jnp.arange(n_blocks, dtype=jnp.int32) * EXPERT_BLK
    blk_expert = jnp.minimum(jnp.sum((pend[None, :] <= blk_row0[:, None]).astype(jnp.int32), axis=1),
                             N_EXPERTS - 1)
    n_used = (pend[-1:] // EXPERT_BLK).astype(jnp.int32)
    layout = jnp.concatenate([pstart, pend.astype(jnp.int32), n_used])
    dest = _dest(layout, rt)
    d0, d1 = dest[0], dest[1]

    xs = _dispatch(layout, d0, d1, hn, cap)
    ys = _experts(blk_expert, n_used, xs, w1[ly], w3[ly], w2[ly])
    out = _combine(d0, d1, x1, rc, ys)
    return out.reshape(b_sz, s_len, d)
```

```python
import functools
import math

import jax
import jax.numpy as jnp
import numpy as np
from jax import lax
from jax.experimental import pallas as pl
from jax.experimental.pallas import tpu as pltpu

F32 = jnp.float32
BF16 = jnp.bfloat16

D_MODEL = 1024
DA_HEADS = 4
DA_HEAD_DIM = 64
DA_V_DIM = 128
DA_WIDTH = 512
ML_HEADS = 4
ML_HEAD_DIM = 128
ML_WIDTH = 512
CONV_WIDTH = 4
N_GROUPS = 4
EXPERTS_PER_GROUP = 8
N_EXPERTS = 32
EXPERT_FF = 512
NORM_EPS = 1e-6
LAM_INIT = 0.8 - 0.6 * math.exp(-0.3 * 0)

LANES = 128
SUBLANES = 8
BF16_SUBLANES = 16
VMEM_LIMIT = 56 * 1024 * 1024

SEG_K = 0
SEG_MQK = SEG_K + 512
SEG_MO = SEG_MQK + 1024
SEG_IF = SEG_MO + 512
W_ALL = SEG_IF + LANES

LOG2E = math.log2(math.e)
ALIBI_PIECES = 4
AUX0 = DA_HEAD_DIM
POS_PERIOD = 512
POS_HALF = POS_PERIOD // 2
DV_AUG = DA_V_DIM + BF16_SUBLANES
GROUP_SHIFT = EXPERTS_PER_GROUP.bit_length() - 1


def _bf16_round(val):
    bits = np.float32(val).view(np.uint32)
    bits = (bits + np.uint32(0x7FFF) + ((bits >> np.uint32(16)) & np.uint32(1))) & np.uint32(0xFFFF0000)
    return float(bits.view(np.float32))


def _alibi_pieces(hd):
    rest, pieces = 2.0 ** (-8.0 * (hd + 1) / DA_HEADS) * LOG2E, []
    for _ in range(ALIBI_PIECES):
        pieces.append(_bf16_round(rest))
        rest -= pieces[-1]
    return pieces

TM_PROJ = 1024
PROJ_CHAIN_ROWS = 256
TQ_ATTN = 512
ATTN_HEADS = 4
L_MLSTM = 256
TM_MERGE = 512
TM_ROWS = 1024
EXPERT_BLK = 512
ISSUE_UNROLL = 8


def _cparams(sem):
    return pltpu.CompilerParams(dimension_semantics=sem, vmem_limit_bytes=VMEM_LIMIT)


def _lane_iota(shape):
    return lax.broadcasted_iota(jnp.int32, shape, len(shape) - 1)


def _row_iota(shape):
    return lax.broadcasted_iota(jnp.int32, shape, len(shape) - 2)


def _in_proj_kernel(x_ref, g_ref, w_ref, wt_ref, gq_ref, gk_ref, aux_ref,
                    qt_ref, ka_ref, vt_ref, mqk_ref, mvt_ref, mo_ref, gif_ref, *, tm):
    n = min(PROJ_CHAIN_ROWS, tm)
    for r0 in range(0, tm, n):
        _in_proj_rows(x_ref, g_ref, w_ref, wt_ref, gq_ref, gk_ref, aux_ref,
                      qt_ref, ka_ref, vt_ref, mqk_ref, mvt_ref, mo_ref, gif_ref, r0, n, tm)


def _in_proj_rows(x_ref, g_ref, w_ref, wt_ref, gq_ref, gk_ref, aux_ref,
                  qt_ref, ka_ref, vt_ref, mqk_ref, mvt_ref, mo_ref, gif_ref, r0, n, tm):
    rows = slice(r0, r0 + n)
    x = x_ref[0, rows, :]
    ms = jnp.mean(x * x, axis=-1, keepdims=True)
    h = ((x * lax.rsqrt(ms + NORM_EPS)) * g_ref[...]).astype(BF16)

    def seg(lo, width):
        return jnp.dot(h, w_ref[:, lo:lo + width], preferred_element_type=F32)

    inv_dh = 1.0 / DA_HEAD_DIM

    t_all = lax.dot_general(wt_ref[...], h, (((1,), (1,)), ((), ())), preferred_element_type=F32)
    dh = DA_HEAD_DIM
    for hd in range(DA_HEADS):
        aux_rows = jnp.broadcast_to(aux_ref[hd, dh:LANES, :], (LANES - dh, n)).astype(BF16)
        for mp in range(2):
            off = (hd * 2 + mp) * dh
            q = t_all[off:off + dh, :]
            qn = (q * lax.rsqrt(jnp.sum(q * q, axis=0, keepdims=True) * inv_dh + NORM_EPS)) * gq_ref[0:dh, :]
            qt_ref[0, hd, mp * LANES:mp * LANES + dh, rows] = (qn * (dh ** -0.5 * LOG2E)).astype(BF16)
            qt_ref[0, hd, mp * LANES + dh:(mp + 1) * LANES, rows] = aux_rows
        v_lo = DA_HEADS * 2 * dh + hd * DA_V_DIM
        vt_ref[0, hd, 0:DA_V_DIM, rows] = t_all[v_lo:v_lo + DA_V_DIM, :].astype(BF16)
        ones_row = (_row_iota((DV_AUG - DA_V_DIM, n)) == 0).astype(BF16)
        vt_ref[0, hd, DA_V_DIM:DV_AUG, rows] = ones_row

    lane = _lane_iota((n, LANES))
    pos = pl.program_id(1) * tm + r0 + _row_iota((n, LANES))
    pos_lo = (pos & (POS_HALF - 1)).astype(F32)
    pos_hi = (pos & POS_HALF).astype(F32)
    is_aux = (lane >= AUX0) & (lane < AUX0 + 2 * ALIBI_PIECES)
    is_lo = is_aux & ((lane & 1) == 0)
    is_hi = is_aux & ((lane & 1) == 1)
    real = lane < dh
    k_all = seg(SEG_K, DA_HEADS * 2 * dh)
    for hd in range(DA_HEADS):
        k_pair = k_all[:, hd * LANES:(hd + 1) * LANES]
        for mp in range(2):
            k = jnp.where(real, k_pair if mp == 0 else pltpu.roll(k_pair, dh, axis=1), 0.0)
            kn = (k * lax.rsqrt(jnp.sum(k * k, axis=-1, keepdims=True) * inv_dh + NORM_EPS)) * gk_ref[...]
            kn = jnp.where(is_lo, pos_lo, jnp.where(is_hi, pos_hi, kn))
            ka_ref[0, hd, rows, mp * LANES:(mp + 1) * LANES] = kn.astype(BF16)

    mqk_ref[0, rows, :] = seg(SEG_MQK, 2 * ML_WIDTH)
    mvt_ref[0, :, rows] = t_all[DA_HEADS * 2 * dh + DA_WIDTH:, :].astype(BF16)
    mo_ref[0, rows, :] = seg(SEG_MO, ML_WIDTH)
    gif_ref[0, rows, :] = seg(SEG_IF, LANES)


def _in_proj(x, g_attn, w_all, w_t, gq_col, gk):
    b_sz, s_len, d = x.shape
    tm = min(TM_PROJ, s_len)
    grid = (b_sz, s_len // tm)
    tok = lambda width: pl.BlockSpec((1, tm, width), lambda b, s: (b, s, 0))
    full = lambda shape: pl.BlockSpec(shape, lambda b, s: (0,) * len(shape))
    n_t = w_t.shape[0]
    aux = np.zeros((DA_HEADS, LANES, 1), np.float32)
    for hd in range(DA_HEADS):
        aux[hd, AUX0:AUX0 + 2 * ALIBI_PIECES, 0] = np.repeat(_alibi_pieces(hd), 2)
    return pl.pallas_call(
        functools.partial(_in_proj_kernel, tm=tm),
        grid=grid,
        in_specs=[tok(d), full((1, d)), full((d, W_ALL)), full((n_t, d)), full((LANES, 1)), full((1, LANES)),
                  full((DA_HEADS, LANES, 1))],
        out_specs=[pl.BlockSpec((1, DA_HEADS, 2 * LANES, tm), lambda b, s: (b, 0, 0, s)),
                   pl.BlockSpec((1, DA_HEADS, tm, 2 * LANES), lambda b, s: (b, 0, s, 0)),
                   pl.BlockSpec((1, DA_HEADS, DV_AUG, tm), lambda b, s: (b, 0, 0, s)),
                   tok(2 * ML_WIDTH), pl.BlockSpec((1, ML_WIDTH, tm), lambda b, s: (b, 0, s)), tok(ML_WIDTH),
                   tok(LANES)],
        out_shape=[
            jax.ShapeDtypeStruct((b_sz, DA_HEADS, 2 * LANES, s_len), BF16),
            jax.ShapeDtypeStruct((b_sz, DA_HEADS, s_len, 2 * LANES), BF16),
            jax.ShapeDtypeStruct((b_sz, DA_HEADS, DV_AUG, s_len), BF16),
            jax.ShapeDtypeStruct((b_sz, s_len, 2 * ML_WIDTH), F32),
            jax.ShapeDtypeStruct((b_sz, ML_WIDTH, s_len), BF16),
            jax.ShapeDtypeStruct((b_sz, s_len, ML_WIDTH), F32),
            jax.ShapeDtypeStruct((b_sz, s_len, LANES), F32),
        ],
        compiler_params=_cparams(("parallel", "parallel")),
        name="in_proj",
    )(x, g_attn, w_all, w_t, gq_col, gk, jnp.asarray(aux))


def _attn_kernel(lam_ref, qt_ref, ka_ref, vt_ref, go_ref, o_ref, acc_ref, *, tq):
    hp = pl.program_id(1)
    qi = pl.program_id(2)
    slopes = [sum(_alibi_pieces(h)) for h in range(DA_HEADS)]
    pair_slopes = []
    for hh in range(ATTN_HEADS):
        sel = slopes[hh]
        for step in range(1, DA_HEADS // ATTN_HEADS):
            sel = jnp.where(hp == step, slopes[step * ATTN_HEADS + hh], sel)
        pair_slopes.append(sel)
    chains = [(hh, mp) for hh in range(ATTN_HEADS) for mp in range(2)]
    qts = {(hh, mp): qt_ref[0, hh, mp * LANES:(mp + 1) * LANES, :] for hh, mp in chains}
    causal = _row_iota((tq, tq)) <= _lane_iota((tq, tq))
    acc_ref[...] = jnp.zeros_like(acc_ref)

    def tile(kv, carry, masked):
        start = pl.multiple_of(kv * tq, tq)
        base = ((jnp.zeros((1, tq), jnp.int32) + (start // POS_PERIOD) * POS_PERIOD) - qi * tq).astype(F32)
        scores = {}

        def issue_scores(hh):
            k = ka_ref[0, hh, pl.ds(start, tq), :]
            for mp in range(2):
                scores[hh, mp] = jnp.dot(k[:, mp * LANES:(mp + 1) * LANES], qts[hh, mp],
                                         preferred_element_type=F32)

        issue_scores(0)
        out = {}
        for hh in range(ATTN_HEADS):
            if hh + 1 < ATTN_HEADS:
                issue_scores(hh + 1)
            c = base * pair_slopes[hh]
            probs = []
            for mp in range(2):
                s = scores[hh, mp]
                if masked:
                    s = jnp.where(causal, s, -jnp.inf)
                m_old = carry[chains.index((hh, mp))]
                m_new = jnp.maximum(m_old, jnp.max(s, axis=0, keepdims=True) + c)
                probs.append((jnp.exp2(m_old - m_new), jnp.exp2(s - (m_new - c)).astype(BF16)))
                out[hh, mp] = m_new
            vt = vt_ref[0, hh, :, pl.ds(start, tq)]
            for mp in range(2):
                alpha, p = probs[mp]
                acc_ref[hh, mp] = alpha * acc_ref[hh, mp] + jnp.dot(vt, p, preferred_element_type=F32)
        return tuple(out[ch] for ch in chains)

    init = tuple(jnp.full((1, tq), -jnp.inf, F32) for _ in chains)
    carry = lax.fori_loop(0, qi, lambda kv, cr: tile(kv, cr, False), init)
    tile(qi, carry, True)

    lam_v = lam_ref[...]
    lam = (jnp.exp(jnp.sum(lam_v[0:1] * lam_v[1:2], axis=-1, keepdims=True))
           - jnp.exp(jnp.sum(lam_v[2:3] * lam_v[3:4], axis=-1, keepdims=True)) + LAM_INIT)
    for hh in range(ATTN_HEADS):
        l1 = acc_ref[hh, 0, DA_V_DIM:DA_V_DIM + 1, :]
        l2 = acc_ref[hh, 1, DA_V_DIM:DA_V_DIM + 1, :]
        o = acc_ref[hh, 0, 0:DA_V_DIM, :] / l1 - lam * (acc_ref[hh, 1, 0:DA_V_DIM, :] / l2)
        o = (o * lax.rsqrt(jnp.mean(o * o, axis=0, keepdims=True) + NORM_EPS)) * go_ref[...]
        o_ref[0, :, hh * DA_V_DIM:(hh + 1) * DA_V_DIM] = (o * (1.0 - LAM_INIT)).T.astype(BF16)


def _attn(lam_vecs, qt, ka, vt, g_out_col):
    b_sz, _, s_len, _ = ka.shape
    tq = min(TQ_ATTN, s_len)
    grid = (b_sz, DA_HEADS // ATTN_HEADS, s_len // tq)
    return pl.pallas_call(
        functools.partial(_attn_kernel, tq=tq),
        grid=grid,
        in_specs=[
            pl.BlockSpec((4, LANES), lambda b, h, i: (0, 0)),
            pl.BlockSpec((1, ATTN_HEADS, 2 * LANES, tq), lambda b, h, i: (b, h, 0, i)),
            pl.BlockSpec((1, ATTN_HEADS, s_len, 2 * LANES), lambda b, h, i: (b, h, 0, 0)),
            pl.BlockSpec((1, ATTN_HEADS, DV_AUG, s_len), lambda b, h, i: (b, h, 0, 0)),
            pl.BlockSpec((DA_V_DIM, 1), lambda b, h, i: (0, 0)),
        ],
        out_specs=pl.BlockSpec((1, tq, ATTN_HEADS * DA_V_DIM), lambda b, h, i: (b, i, h)),
        out_shape=jax.ShapeDtypeStruct((b_sz, s_len, DA_WIDTH), BF16),
        scratch_shapes=[pltpu.VMEM((ATTN_HEADS, 2, DV_AUG, tq), F32)],
        compiler_params=_cparams(("parallel", "parallel", "arbitrary")),
        name="attn",
    )(lam_vecs, qt, ka, vt, g_out_col)


def _log_sigmoid(x):
    return jnp.minimum(x, 0.0) - jnp.log1p(jnp.exp(-jnp.abs(x)))


def _mlstm_kernel(mqk_ref, mvt_ref, mo_ref, gif_ref, cw_ref, cb_ref, gb_ref, go_ref,
                  y_ref, buf_ref, c_ref, n_ref, m_ref, *, L):
    halo = SUBLANES

    @pl.when(pl.program_id(1) == 0)
    def _():
        buf_ref[0:halo, :] = jnp.zeros((halo, 2 * ML_WIDTH), F32)
        c_ref[...] = jnp.zeros_like(c_ref)
        n_ref[...] = jnp.zeros_like(n_ref)
        m_ref[...] = jnp.zeros_like(m_ref)

    buf_ref[halo:halo + L, :] = mqk_ref[0]
    conv = jnp.broadcast_to(cb_ref[...], (L, 2 * ML_WIDTH))
    for j in range(CONV_WIDTH):
        lo = halo - (CONV_WIDTH - 1) + j
        conv = conv + buf_ref[lo:lo + L, :] * cw_ref[j:j + 1, :]
    buf_ref[0:halo, :] = buf_ref[L:L + halo, :]
    qk = conv * jax.nn.sigmoid(conv)

    g = gif_ref[0] + gb_ref[...]
    lf = _log_sigmoid(g)
    lf_t = lf.T

    row = _row_iota((L, L))
    col = _lane_iota((L, L))
    upper = row <= col
    nt = (((1,), (1,)), ((), ()))

    for hd in range(ML_HEADS):
        sl = slice(hd * ML_HEAD_DIM, (hd + 1) * ML_HEAD_DIM)
        q = qk[:, sl] * (ML_HEAD_DIM ** -0.5)
        k = qk[:, ML_WIDTH + hd * ML_HEAD_DIM:ML_WIDTH + (hd + 1) * ML_HEAD_DIM]
        vt = mvt_ref[0, sl, :]
        i_col = g[:, hd:hd + 1]
        f_col = lf[:, ML_HEADS + hd:ML_HEADS + hd + 1]
        f_row = lf_t[ML_HEADS + hd:ML_HEADS + hd + 1, :]

        b_col = jnp.sum(jnp.where(col <= row, f_row, 0.0), axis=-1, keepdims=True)
        b_row = jnp.sum(jnp.where(upper, f_col, 0.0), axis=0, keepdims=True)
        b_last = b_row[:, L - 1:L]

        m_state = m_ref[hd:hd + 1, 0:1]
        n_state = n_ref[hd:hd + 1, :]
        ct_state = c_ref[hd]

        dmat = jnp.where(upper, b_row + (i_col - b_col), -jnp.inf)
        r = jnp.max(dmat, axis=0, keepdims=True)
        p = jnp.exp(dmat - r)
        qb = q.astype(BF16)
        kb = k.astype(BF16)
        sc = lax.dot_general(kb, qb, nt, preferred_element_type=F32) * p
        num_a = jnp.dot(vt, sc.astype(BF16), preferred_element_type=F32)
        den_a = jnp.sum(sc, axis=0, keepdims=True)

        inter = b_row + m_state
        m_row = jnp.maximum(r, inter)
        e_a = jnp.exp(r - m_row)
        e_b = jnp.exp(inter - m_row)
        q_c = lax.dot_general(ct_state.astype(BF16), qb, nt, preferred_element_type=F32)
        n_rows = jnp.broadcast_to(n_state, (SUBLANES, ML_HEAD_DIM)).astype(BF16)
        q_n = lax.dot_general(n_rows, qb, nt, preferred_element_type=F32)[0:1, :]
        num = e_a * num_a + e_b * q_c
        den = e_a * den_a + e_b * q_n
        hval = num / jnp.maximum(jnp.abs(den), jnp.exp(-m_row))

        w_col = b_last - b_col + i_col
        a = jnp.max(w_col, axis=0, keepdims=True)
        m_new = jnp.maximum(b_last + m_state, a)
        kw = k * jnp.exp(w_col - a)
        d_c = jnp.dot(vt, kw.astype(BF16), preferred_element_type=F32)
        d_n = jnp.sum(kw, axis=0, keepdims=True)
        decay = jnp.exp(b_last + m_state - m_new)
        gain = jnp.exp(a - m_new)
        c_ref[hd] = decay * ct_state + gain * d_c
        n_ref[hd:hd + 1, :] = decay * n_state + gain * d_n
        m_ref[hd:hd + 1, :] = jnp.broadcast_to(m_new, (1, LANES))

        hn_t = hval * lax.rsqrt(jnp.mean(hval * hval, axis=0, keepdims=True) + NORM_EPS)
        y_ref[0, :, sl] = (hn_t.T * go_ref[:, sl] * jax.nn.sigmoid(mo_ref[0, :, sl])).astype(BF16)


def _mlstm(mqk, mvt, mo, gif, conv_w, conv_b, gate_b, g_out):
    b_sz, s_len, _ = mqk.shape
    L = min(L_MLSTM, s_len)
    grid = (b_sz, s_len // L)
    tok = lambda width: pl.BlockSpec((1, L, width), lambda b, c: (b, c, 0))
    full = lambda shape: pl.BlockSpec(shape, lambda b, c: (0,) * len(shape))
    return pl.pallas_call(
        functools.partial(_mlstm_kernel, L=L),
        grid=grid,
        in_specs=[tok(2 * ML_WIDTH), pl.BlockSpec((1, ML_WIDTH, L), lambda b, c: (b, 0, c)), tok(ML_WIDTH),
                  tok(LANES),
                  full((CONV_WIDTH, 2 * ML_WIDTH)), full((1, 2 * ML_WIDTH)), full((1, LANES)),
                  full((1, ML_WIDTH))],
        out_specs=tok(ML_WIDTH),
        out_shape=jax.ShapeDtypeStruct((b_sz, s_len, ML_WIDTH), BF16),
        scratch_shapes=[
            pltpu.VMEM((L + SUBLANES, 2 * ML_WIDTH), F32),
            pltpu.VMEM((ML_HEADS, ML_HEAD_DIM, ML_HEAD_DIM), F32),
            pltpu.VMEM((SUBLANES, LANES), F32),
            pltpu.VMEM((SUBLANES, LANES), F32),
        ],
        compiler_params=_cparams(("parallel", "arbitrary")),
        name="mlstm",
    )(mqk, mvt, mo, gif, conv_w, conv_b, gate_b, g_out)


def _merge_kernel(x_ref, yda_ref, yml_ref, ga_ref, wg_ref, bg_ref, wda_ref, wml_ref, wo_ref,
                  gf_ref, wr_ref, br_ref,
                  x1_ref, hn_ref, rc_ref, rt_ref, cnt_ref, carry_ref, *, tm):
    @pl.when(pl.program_id(0) == 0)
    def _():
        carry_ref[...] = jnp.zeros_like(carry_ref)

    halves = [slice(0, tm // 2), slice(tm // 2, tm)]

    def rms(v, g_ref):
        return (v * lax.rsqrt(jnp.mean(v * v, axis=-1, keepdims=True) + NORM_EPS)) * g_ref[...]

    def mm(lhs, w_ref):
        return jnp.dot(lhs, w_ref[...], preferred_element_type=F32)

    xs = [x_ref[sl, :] for sl in halves]
    gates = [jax.nn.sigmoid(mm(rms(x, ga_ref).astype(BF16), wg_ref) + bg_ref[...]) for x in xs]
    mixed = [g[:, :D_MODEL] * mm(yda_ref[sl, :], wda_ref) + g[:, D_MODEL:] * mm(yml_ref[sl, :], wml_ref)
             for g, sl in zip(gates, halves)]
    x1s = [x + mm(m.astype(BF16), wo_ref) for x, m in zip(xs, mixed)]
    hns = [rms(x1, gf_ref) for x1 in x1s]
    for sl, x1, hn in zip(halves, x1s, hns):
        x1_ref[sl, :] = x1
        hn_ref[sl, :] = hn

    logits = jnp.concatenate([mm(hn.astype(BF16), wr_ref) for hn in hns], axis=0) + br_ref[...]
    lane = _lane_iota((tm, LANES))
    neg = -jnp.inf
    big = jnp.int32(LANES)

    gl = jnp.where((lane >= N_EXPERTS) & (lane < N_EXPERTS + N_GROUPS), logits, neg)
    gmax = jnp.max(gl, axis=-1, keepdims=True)
    gsum = jnp.sum(jnp.exp(gl - gmax), axis=-1, keepdims=True)
    g_top = 1.0 / gsum
    g_idx = jnp.min(jnp.where(gl == gmax, lane, big), axis=-1, keepdims=True) - N_EXPERTS

    el = jnp.where((lane < N_EXPERTS) & ((lane >> GROUP_SHIFT) == g_idx), logits, neg)
    emax = jnp.max(el, axis=-1, keepdims=True)
    esum = jnp.sum(jnp.exp(el - emax), axis=-1, keepdims=True)
    e0 = jnp.min(jnp.where(el == emax, lane, big), axis=-1, keepdims=True)
    el2 = jnp.where(lane == e0, neg, el)
    emax2 = jnp.max(el2, axis=-1, keepdims=True)
    e1 = jnp.min(jnp.where(el2 == emax2, lane, big), axis=-1, keepdims=True)
    p0 = 1.0 / esum
    p1 = jnp.exp(emax2 - emax) / esum
    tot = p0 + p1
    w0 = g_top * (p0 / tot)
    w1 = g_top * (p1 / tot)

    hit0 = lane == e0
    hit1 = lane == e1
    onehot = (hit0 | hit1).astype(F32)
    before = (_lane_iota((tm, tm)) < _row_iota((tm, tm))).astype(BF16)
    prefix = jnp.dot(before, onehot.astype(BF16), preferred_element_type=F32) + carry_ref[...]
    r0 = jnp.sum(jnp.where(hit0, prefix, 0.0), axis=-1, keepdims=True)
    r1 = jnp.sum(jnp.where(hit1, prefix, 0.0), axis=-1, keepdims=True)
    carry_ref[...] = carry_ref[...] + jnp.sum(onehot, axis=0, keepdims=True)
    cnt_ref[...] = carry_ref[...]

    rc = jnp.where(lane == 0, e0.astype(F32),
         jnp.where(lane == 1, e1.astype(F32),
         jnp.where(lane == 2, r0,
         jnp.where(lane == 3, r1,
         jnp.where(lane == 4, w0,
         jnp.where(lane == 5, w1, 0.0))))))
    rc_ref[...] = rc
    rt_ref[...] = rc.T[0:SUBLANES, :].astype(jnp.int32)


def _merge(x2, yda, yml, g_attn, w_gate, b_gate, w_da, w_ml, w_out, g_ffn, w_rt, b_rt):
    n_tok, d = x2.shape
    tm = min(TM_MERGE, n_tok)
    grid = (n_tok // tm,)
    tok = lambda width: pl.BlockSpec((tm, width), lambda i: (i, 0))
    full = lambda shape: pl.BlockSpec(shape, lambda i: (0,) * len(shape))
    return pl.pallas_call(
        functools.partial(_merge_kernel, tm=tm),
        grid=grid,
        in_specs=[tok(d), tok(DA_WIDTH), tok(ML_WIDTH), full((1, d)), full((d, 2 * d)), full((1, 2 * d)),
                  full((DA_WIDTH, d)), full((ML_WIDTH, d)), full((d, d)), full((1, d)),
                  full((d, LANES)), full((1, LANES))],
        out_specs=[tok(d), tok(d), tok(LANES), pl.BlockSpec((SUBLANES, tm), lambda i: (0, i)),
                   full((1, LANES))],
        out_shape=[
            jax.ShapeDtypeStruct((n_tok, d), F32),
            jax.ShapeDtypeStruct((n_tok, d), F32),
            jax.ShapeDtypeStruct((n_tok, LANES), F32),
            jax.ShapeDtypeStruct((SUBLANES, n_tok), jnp.int32),
            jax.ShapeDtypeStruct((1, LANES), F32),
        ],
        scratch_shapes=[pltpu.VMEM((1, LANES), F32)],
        compiler_params=_cparams(("arbitrary",)),
        name="merge",
    )(x2, yda, yml, g_attn, w_gate, b_gate, w_da, w_ml, w_out, g_ffn, w_rt, b_rt)


def _dest_kernel(layout_ref, rt_ref, o_ref):
    tile = rt_ref[...]
    first = jnp.zeros_like(tile)
    for ex in range(N_EXPERTS):
        first = jnp.where(tile == ex, layout_ref[ex], first)
    o_ref[...] = first + pltpu.roll(tile, SUBLANES - 2, axis=0)


def _dest(layout, rt):
    _, n_tok = rt.shape
    tm = min(TM_ROWS, n_tok)
    return pl.pallas_call(
        _dest_kernel,
        grid_spec=pltpu.PrefetchScalarGridSpec(
            num_scalar_prefetch=1,
            grid=(n_tok // tm,),
            in_specs=[pl.BlockSpec((SUBLANES, tm), lambda i, ly: (0, i))],
            out_specs=pl.BlockSpec((SUBLANES, tm), lambda i, ly: (0, i)),
        ),
        out_shape=jax.ShapeDtypeStruct((SUBLANES, n_tok), jnp.int32),
        compiler_params=_cparams(("parallel",)),
        name="dest",
    )(layout, rt)


def _load_indices(idx_refs, smem_refs, sem, base, tm):
    copies = [pltpu.make_async_copy(src.at[pl.ds(base, tm)], dst, sem.at[n])
              for n, (src, dst) in enumerate(zip(idx_refs, smem_refs))]
    for cp in copies:
        cp.start()
    for cp in copies:
        cp.wait()


def _dispatch_kernel(pstart_ref, d0_ref, d1_ref, hn_ref, xs_ref,
                     d0_s, d1_s, zblk, isem, sem, zsem, *, tm, n_blocks):

    @pl.when(pl.program_id(0) == 0)
    def _():
        zblk[...] = jnp.zeros_like(zblk)

        def zero_copy(row0):
            return pltpu.make_async_copy(zblk, xs_ref.at[pl.ds(pl.multiple_of(row0, EXPERT_BLK), EXPERT_BLK)], zsem)

        def last_block(e, n):
            end = pstart_ref[N_EXPERTS + e]
            has = end > pstart_ref[e]

            @pl.when(has)
            def _():
                zero_copy(end - EXPERT_BLK).start()

            return n + has.astype(jnp.int32)

        def tail_block(j, n):
            zero_copy(j * EXPERT_BLK).start()
            return n + 1

        n = lax.fori_loop(0, N_EXPERTS, last_block, 0)
        n = lax.fori_loop(pstart_ref[2 * N_EXPERTS], n_blocks, tail_block, n)

        def drain(_, c):
            zero_copy(0).wait()
            return c

        lax.fori_loop(0, n, drain, 0)

    base = pl.multiple_of(pl.program_id(0) * tm, tm)
    _load_indices((d0_ref, d1_ref), (d0_s, d1_s), isem, base, tm)

    def row_copy(t, dest):
        return pltpu.make_async_copy(hn_ref.at[pl.ds(t, 1)], xs_ref.at[pl.ds(dest, 1)], sem)

    def issue(t, _):
        row_copy(t, d0_s[t]).start(priority=0)
        row_copy(t, d1_s[t]).start(priority=1)
        return 0

    lax.fori_loop(0, tm, issue, 0, unroll=ISSUE_UNROLL)

    for _ in range(2):
        pltpu.make_async_copy(hn_ref, xs_ref.at[pl.ds(0, tm)], sem).wait()


def _dispatch(layout, d0, d1, hn, cap):
    n_tok, d = hn.shape
    tm = min(TM_ROWS, n_tok)
    grid = (n_tok // tm,)
    any_spec = pl.BlockSpec(memory_space=pl.ANY)
    return pl.pallas_call(
        functools.partial(_dispatch_kernel, tm=tm, n_blocks=cap // EXPERT_BLK),
        grid_spec=pltpu.PrefetchScalarGridSpec(
            num_scalar_prefetch=1,
            grid=grid,
            in_specs=[any_spec, any_spec, pl.BlockSpec((tm, d), lambda i, ps: (i, 0))],
            out_specs=any_spec,
            scratch_shapes=[pltpu.SMEM((tm,), jnp.int32)] * 2
            + [pltpu.VMEM((EXPERT_BLK, d), F32), pltpu.SemaphoreType.DMA((2,)), pltpu.SemaphoreType.DMA,
               pltpu.SemaphoreType.DMA],
        ),
        out_shape=jax.ShapeDtypeStruct((cap, d), F32),
        compiler_params=_cparams(("arbitrary",)),
        name="dispatch",
    )(layout, d0, d1, hn)


def _experts_kernel(be_ref, nb_ref, xs_ref, w1_ref, w3_ref, w2_ref, y_ref, w1b, w3b, w2b):
    i = pl.program_id(0)
    used = i < nb_ref[0]

    @pl.when(used & ((i == 0) | (be_ref[i] != be_ref[jnp.maximum(i - 1, 0)])))
    def _():
        w1b[...] = w1_ref[0].astype(BF16)
        w3b[...] = w3_ref[0].astype(BF16)
        w2b[...] = w2_ref[0].astype(BF16)

    @pl.when(used)
    def _():
        xb = xs_ref[...].astype(BF16)
        a = jnp.dot(xb, w1b[...], preferred_element_type=F32)
        b = jnp.dot(xb, w3b[...], preferred_element_type=F32)
        hid = (a * jax.nn.sigmoid(a)) * b
        y_ref[...] = jnp.dot(hid.astype(BF16), w2b[...], preferred_element_type=F32)

    @pl.when(jnp.logical_not(used))
    def _():
        y_ref[...] = jnp.zeros_like(y_ref)


def _experts(blk_expert, n_blocks_used, xs, w1, w3, w2):
    cap, d = xs.shape
    n_blocks = cap // EXPERT_BLK
    rows = lambda i, be, nb: (jnp.minimum(i, nb[0] - 1), 0)
    wsel = lambda i, be, nb: (be[jnp.minimum(i, nb[0] - 1)], 0, 0)
    return pl.pallas_call(
        _experts_kernel,
        grid_spec=pltpu.PrefetchScalarGridSpec(
            num_scalar_prefetch=2,
            grid=(n_blocks,),
            in_specs=[pl.BlockSpec((EXPERT_BLK, d), rows),
                      pl.BlockSpec((1, d, EXPERT_FF), wsel),
                      pl.BlockSpec((1, d, EXPERT_FF), wsel),
                      pl.BlockSpec((1, EXPERT_FF, d), wsel)],
            out_specs=pl.BlockSpec((EXPERT_BLK, d), lambda i, be, nb: (i, 0)),
            scratch_shapes=[pltpu.VMEM((d, EXPERT_FF), BF16), pltpu.VMEM((d, EXPERT_FF), BF16),
                            pltpu.VMEM((EXPERT_FF, d), BF16)],
        ),
        out_shape=jax.ShapeDtypeStruct((cap, d), F32),
        compiler_params=_cparams(("arbitrary",)),
        name="experts",
    )(blk_expert, n_blocks_used, xs, w1, w3, w2)


def _combine_kernel(d0_ref, d1_ref, x1_ref, rc_ref, ys_ref, o_ref, d0_s, d1_s, ybuf, isem, sem, *, tm):
    base = pl.multiple_of(pl.program_id(0) * tm, tm)
    _load_indices((d0_ref, d1_ref), (d0_s, d1_s), isem, base, tm)

    def row_copy(slot, t, src):
        return pltpu.make_async_copy(ys_ref.at[pl.ds(src, 1)], ybuf.at[slot, pl.ds(t, 1)], sem)

    def issue(t, _):
        row_copy(0, t, d0_s[t]).start(priority=0)
        row_copy(1, t, d1_s[t]).start(priority=1)
        return 0

    lax.fori_loop(0, tm, issue, 0, unroll=ISSUE_UNROLL)

    for slot in range(2):
        pltpu.make_async_copy(ys_ref.at[pl.ds(0, tm)], ybuf.at[slot], sem).wait()

    rc = rc_ref[...]
    o_ref[...] = x1_ref[...] + (rc[:, 4:5] * ybuf[0] + rc[:, 5:6] * ybuf[1])


def _combine(d0, d1, x1, rc, ys):
    n_tok, d = x1.shape
    tm = min(TM_ROWS, n_tok)
    any_spec = pl.BlockSpec(memory_space=pl.ANY)
    return pl.pallas_call(
        functools.partial(_combine_kernel, tm=tm),
        grid=(n_tok // tm,),
        in_specs=[any_spec, any_spec,
                  pl.BlockSpec((tm, d), lambda i: (i, 0)),
                  pl.BlockSpec((tm, LANES), lambda i: (i, 0)),
                  any_spec],
        out_specs=pl.BlockSpec((tm, d), lambda i: (i, 0)),
        scratch_shapes=[pltpu.SMEM((tm,), jnp.int32)] * 2
        + [pltpu.VMEM((2, tm, d), F32), pltpu.SemaphoreType.DMA((2,)), pltpu.SemaphoreType.DMA],
        out_shape=jax.ShapeDtypeStruct((n_tok, d), F32),
        compiler_params=_cparams(("arbitrary",)),
        name="combine",
    )(d0, d1, x1, rc, ys)


def _pad_lanes(vec, width=LANES):
    return jnp.zeros((1, width), F32).at[0, :vec.shape[0]].set(vec.astype(F32))


def _layout_w_in(w_in):
    d = w_in.shape[0]
    gates = jnp.concatenate([w_in[:, 3584:3592], jnp.zeros((d, LANES - 2 * ML_HEADS), w_in.dtype)], axis=-1)
    w_all = jnp.concatenate([w_in[:, 512:1024], w_in[:, 1536:2560], w_in[:, 3072:3584], gates],
                            axis=-1).astype(BF16)
    w_t = jnp.concatenate([w_in[:, 0:512], w_in[:, 1024:1536], w_in[:, 2560:3072]], axis=-1).T.astype(BF16)
    return w_all, w_t


def kernel(x, attn_norm_g, w_in, da_q_norm_g, da_k_norm_g, da_lambda_q1, da_lambda_k1, da_lambda_q2, da_lambda_k2, da_out_norm_g, ml_conv_w, ml_conv_b, ml_i_bias, ml_f_bias, ml_out_norm_g, w_branch_da, w_branch_ml, w_gate, b_gate, w_out, ffn_norm_g, w_group, b_group, w_router, b_router, w1, w3, w2):
    b_sz, s_len, d = x.shape
    n_tok = b_sz * s_len
    ly = 0

    w_all, w_t = _layout_w_in(w_in[ly])
    g_attn = attn_norm_g[ly].reshape(1, d)
    gq_col = _pad_lanes(da_q_norm_g[ly]).reshape(LANES, 1)
    gk = _pad_lanes(da_k_norm_g[ly])
    lam_vecs = jnp.concatenate([_pad_lanes(v[ly]) for v in
                                (da_lambda_q1, da_lambda_k1, da_lambda_q2, da_lambda_k2)], axis=0)
    gate_b = _pad_lanes(jnp.concatenate([ml_i_bias[ly], ml_f_bias[ly]]))
    w_rt = jnp.concatenate([w_router[ly], w_group[ly],
                            jnp.zeros((d, LANES - N_EXPERTS - N_GROUPS), F32)], axis=-1).astype(BF16)
    b_rt = _pad_lanes(jnp.concatenate([b_router[ly], b_group[ly]]))

    qt, ka, vt, mqk, mvt, mo, gif = _in_proj(x, g_attn, w_all, w_t, gq_col, gk)
    y_da = _attn(lam_vecs, qt, ka, vt, da_out_norm_g[ly].reshape(DA_V_DIM, 1))
    y_ml = _mlstm(mqk, mvt, mo, gif, ml_conv_w[ly], ml_conv_b[ly].reshape(1, -1), gate_b,
                  ml_out_norm_g[ly].reshape(1, ML_WIDTH))

    x1, hn, rc, rt, counts = _merge(
        x.reshape(n_tok, d), y_da.reshape(n_tok, DA_WIDTH), y_ml.reshape(n_tok, ML_WIDTH), g_attn,
        w_gate[ly].astype(BF16), b_gate[ly].reshape(1, -1), w_branch_da[ly].astype(BF16),
        w_branch_ml[ly].astype(BF16), w_out[ly].astype(BF16), ffn_norm_g[ly].reshape(1, d), w_rt, b_rt)

    cnt = counts[0, :N_EXPERTS].astype(jnp.int32)
    padded = (cnt + EXPERT_BLK - 1) // EXPERT_BLK * EXPERT_BLK
    pend = jnp.cumsum(padded)
    pstart = (pend - padded).astype(jnp.int32)
    cap = 2 * n_tok + N_EXPERTS * EXPERT_BLK
    n_blocks = cap // EXPERT_BLK
    blk_row0 = jnp.arange(n_blocks, dtype=jnp.int32) * EXPERT_BLK
    blk_expert = jnp.minimum(jnp.sum((pend[None, :] <= blk_row0[:, None]).astype(jnp.int32), axis=1),
                             N_EXPERTS - 1)
    n_used = (pend[-1:] // EXPERT_BLK).astype(jnp.int32)
    layout = jnp.concatenate([pstart, pend.astype(jnp.int32), n_used])
    dest = _dest(layout, rt)
    d0, d1 = dest[0], dest[1]

    xs = _dispatch(layout, d0, d1, hn, cap)
    ys = _experts(blk_expert, n_used, xs, w1[ly], w3[ly], w2[ly])
    out = _combine(d0, d1, x1, rc, ys)
    return out.reshape(b_sz, s_len, d)
```
